```python
import math
import jax
import jax.numpy as jnp
from jax import lax
import numpy as np

D_MODEL = 1024
BATCH = 4
SEQ = 4096
DEPTH = 4

GRID_W = 64
CTX_LEN = 256
F32 = jnp.float32
EPS = 1e-6
NEG = -1e30
N_MOD = 9
FFN_RESIDUAL = 0.5
D_FF = 256 * math.ceil(8 * D_MODEL / 3 / 256)
HEAD_DIM = 64
A_Q_HEADS = D_MODEL // (2 * HEAD_DIM)
A_KV_HEADS = max(1, A_Q_HEADS // 4)
WINDOW = 128
BLOCK = 128
ROPE_BASE = 10000.0
R_DIM = 64
R_HEADS = D_MODEL // (2 * R_DIM)
R_CHUNK = 128
R_MIN_EXP = 5.0
A_Q_W = A_Q_HEADS * HEAD_DIM
A_KV_W = A_KV_HEADS * HEAD_DIM
R_W = R_HEADS * R_DIM
Q_WIDTHS = (A_Q_W, R_W, R_W)
KV_WIDTHS = (A_KV_W, A_KV_W, R_W, R_W)
Q_SIDE = sum(Q_WIDTHS)
IN_W = Q_SIDE + sum(KV_WIDTHS)
MIX_W = A_Q_W + R_W
HY_EMB = 33
HY_ORDER = 64
HY_SHORT = 3
HY_MAX_DECAY = math.log(1e-2) / 0.3
HY_MIN_DECAY = math.log(1e-2) / 1.5
N_EVEN = (DEPTH + 1) // 2
N_ODD = DEPTH // 2

kernel_name = 'hybrid_swa_retention_hyena_prefix_dit'


def rmsnorm(x, g):
    xf = x.astype(F32)
    y = xf * lax.rsqrt(jnp.mean(xf * xf, -1, keepdims=True) + EPS)
    return (y * g.astype(F32)).astype(x.dtype)


def adaln_in(x, g, m, i):
    return rmsnorm(x, g) * (1.0 + m[:, :, 3 * i + 1]) + m[:, :, 3 * i]


def adaln_out(x, y, g, m, i, w):
    return x + w * m[:, :, 3 * i + 2] * rmsnorm(y, g)


def swiglu(h, w_gate, w_up, w_down):
    return (jax.nn.silu(h @ w_gate) * (h @ w_up)) @ w_down


def macaron_half(s, m, i, g_pre, g_post, w_gate, w_up, w_down):
    h = adaln_in(s, g_pre, m, i)
    return adaln_out(s, swiglu(h, w_gate, w_up, w_down), g_post, m, i, FFN_RESIDUAL)


def split_cols(z, widths):
    idx = np.cumsum(widths)[:-1].tolist()
    return jnp.split(z, idx, axis=-1)


def heads(a, d):
    return a.reshape(a.shape[0], a.shape[1], -1, d)


def axial_rope(L):
    n_rows = L // GRID_W
    row = jnp.repeat(jnp.arange(n_rows, dtype=F32), GRID_W)
    col = jnp.tile(jnp.arange(GRID_W, dtype=F32), n_rows)
    nf = HEAD_DIM // 4
    inv = ROPE_BASE ** (-jnp.arange(nf, dtype=F32) / nf)
    ang = jnp.concatenate([row[:, None] * inv, col[:, None] * inv], -1)
    return jnp.cos(ang), jnp.sin(ang)


def line_rope(L, d):
    inv = ROPE_BASE ** (-jnp.linspace(0.0, 1.0, d // 2, dtype=F32))
    ang = jnp.arange(L, dtype=F32)[:, None] * inv
    return jnp.cos(ang), jnp.sin(ang)


def apply_rope(x, cos, sin):
    half = x.shape[-1] // 2
    x1, x2 = x[..., :half], x[..., half:]
    c, s = cos[None, :, None], sin[None, :, None]
    return jnp.concatenate([x1 * c - x2 * s, x1 * s + x2 * c], -1).astype(x.dtype)


def window_attention(q, k, v, kc, vc, sink):
    Bn, L, Hq, dh = q.shape
    G = k.shape[2]
    R = Hq // G
    nb = L // BLOCK
    T = BLOCK
    qb = q.reshape(Bn, nb, T, G, R, dh)
    pad = ((0, 0), (T, T), (0, 0), (0, 0))
    kp = jnp.pad(k, pad).reshape(Bn, nb + 2, T, G, dh)
    vp = jnp.pad(v, pad).reshape(Bn, nb + 2, T, G, dh)
    kb = jnp.concatenate([kp[:, :-2], kp[:, 1:-1], kp[:, 2:]], axis=2)
    vb = jnp.concatenate([vp[:, :-2], vp[:, 1:-1], vp[:, 2:]], axis=2)
    qi = jnp.arange(T)[:, None]
    kj = jnp.arange(3 * T)[None, :]
    kpos = jnp.arange(nb)[:, None, None] * T - T + kj
    valid = (jnp.abs(kj - T - qi) <= WINDOW)[None] & (kpos >= 0) & (kpos < L)
    scale = HEAD_DIM ** -0.5
    s_loc = jnp.einsum('bnqgrd,bnkgd->bngrqk', qb, kb).astype(F32) * scale
    s_loc = jnp.where(valid[None, :, None, None], s_loc, NEG)
    s_ctx = jnp.einsum('bnqgrd,bcgd->bngrqc', qb, kc).astype(F32) * scale
    s_sink = jnp.broadcast_to(sink.astype(F32).reshape(G, R)[None, None, :, :, None, None],
                              s_loc.shape[:-1] + (1,))
    p = jax.nn.softmax(jnp.concatenate([s_loc, s_ctx, s_sink], -1), axis=-1).astype(v.dtype)
    o = (jnp.einsum('bngrqk,bnkgd->bnqgrd', p[..., :3 * T], vb)
         + jnp.einsum('bngrqc,bcgd->bnqgrd', p[..., 3 * T:-1], vc))
    return o.reshape(Bn, L, Hq * dh)


def context_attention(qc, kc, vc, sink):
    Bn, C, Hq, dh = qc.shape
    G = kc.shape[2]
    R = Hq // G
    qg = qc.reshape(Bn, C, G, R, dh)
    s = jnp.einsum('bqgrd,bkgd->bgrqk', qg, kc).astype(F32) * HEAD_DIM ** -0.5
    s_sink = jnp.broadcast_to(sink.astype(F32).reshape(G, R)[None, :, :, None, None], s.shape[:-1] + (1,))
    p = jax.nn.softmax(jnp.concatenate([s, s_sink], -1), axis=-1)[..., :-1].astype(vc.dtype)
    return jnp.einsum('bgrqk,bkgd->bqgrd', p, vc).reshape(Bn, C, Hq * dh)


def retention_chunks(q, k, v, log_g, s0):
    Bn, L, H, _ = q.shape
    dv = v.shape[-1]
    n = L // R_CHUNK
    T = R_CHUNK
    idx = jnp.arange(T, dtype=F32)
    diff = idx[:, None] - idx[None, :]
    d_in = jnp.where(diff[None] >= 0, jnp.exp(jnp.maximum(diff, 0.0)[None] * log_g[:, None, None]), 0.0)
    d_q = jnp.exp((idx + 1.0)[None, :] * log_g[:, None])
    d_k = jnp.exp((T - 1.0 - idx)[None, :] * log_g[:, None])
    d_c = jnp.exp(T * log_g)

    def to_chunks(a):
        return a.astype(F32).reshape(Bn, n, T, H, -1).transpose(1, 0, 3, 2, 4)

    def step(S, inp):
        qi, ki, vi = inp
        att = jnp.einsum('bhtd,bhsd->bhts', qi, ki) * d_in
        o = (jnp.einsum('bhts,bhse->bhte', att, vi)
             + jnp.einsum('bhtd,bhde->bhte', qi, S) * d_q[None, :, :, None])
        S = S * d_c[None, :, None, None] + jnp.einsum('bhsd,bhse->bhde', ki * d_k[None, :, :, None], vi)
        return S, o

    S, o = lax.scan(step, s0, (to_chunks(q), to_chunks(k), to_chunks(v)))
    return o.transpose(1, 0, 3, 2, 4).reshape(Bn, L, H, dv), S


def retention_final_state(k, v, log_g):
    L = k.shape[1]
    w = jnp.exp((L - 1.0 - jnp.arange(L, dtype=F32))[:, None] * log_g[None, :])
    return jnp.einsum('bthd,bthe,th->bhde', k.astype(F32), v.astype(F32), w)


def retention_out(o, g):
    o = o * lax.rsqrt(jnp.mean(o * o, -1, keepdims=True) + EPS)
    return o.reshape(o.shape[0], o.shape[1], -1).astype(g.dtype) * jax.nn.silu(g)


def even_mixer(h, hc, w_in, sink, decay_logit, w_out, rope_a, rope_r, ctx_full):
    lg_f = jax.nn.log_sigmoid(decay_logit[0].astype(F32))
    lg_b = jax.nn.log_sigmoid(decay_logit[1].astype(F32))
    z = h @ w_in
    aq, rq, rg = split_cols(z[..., :Q_SIDE], Q_WIDTHS)
    ak, av, rk, rv = split_cols(z[..., Q_SIDE:], KV_WIDTHS)
    aq = apply_rope(heads(aq, HEAD_DIM), *rope_a)
    ak = apply_rope(heads(ak, HEAD_DIM), *rope_a)
    av = heads(av, HEAD_DIM)
    rq = apply_rope(heads(rq, R_DIM), *rope_r)
    rk = apply_rope(heads(rk, R_DIM), *rope_r) * R_DIM ** -0.5
    rv = heads(rv, R_DIM)
    if ctx_full:
        zc = hc @ w_in
        aqc, rqc, rgc = split_cols(zc[..., :Q_SIDE], Q_WIDTHS)
        akc, avc, rkc, rvc = split_cols(zc[..., Q_SIDE:], KV_WIDTHS)
    else:
        akc, avc, rkc, rvc = split_cols(hc @ w_in[:, Q_SIDE:], KV_WIDTHS)
    akc, avc = heads(akc, HEAD_DIM), heads(avc, HEAD_DIM)
    rkc, rvc = heads(rkc, R_DIM) * R_DIM ** -0.5, heads(rvc, R_DIM)
    if ctx_full:
        aqc, rqc = heads(aqc, HEAD_DIM), heads(rqc, R_DIM)
        zero = jnp.zeros((hc.shape[0], R_HEADS, R_DIM, R_DIM), F32)
        oc_f, s_f = retention_chunks(rqc, rkc, rvc, lg_f, zero)
        oc_b, s_b = retention_chunks(rqc[:, ::-1], rkc[:, ::-1], rvc[:, ::-1], lg_b, zero)
        yc = jnp.concatenate([context_attention(aqc, akc, avc, sink),
                              retention_out(oc_f + oc_b[:, ::-1], rgc)], -1) @ w_out
    else:
        s_f = retention_final_state(rkc, rvc, lg_f)
        s_b = retention_final_state(rkc[:, ::-1], rvc[:, ::-1], lg_b)
        yc = None
    o_f, _ = retention_chunks(rq, rk, rv, lg_f, s_f)
    o_b, _ = retention_chunks(rq[:, ::-1], rk[:, ::-1], rv[:, ::-1], lg_b, s_b)
    a = window_attention(aq, ak, av, akc, avc, sink)
    y = jnp.concatenate([a, retention_out(o_f + o_b[:, ::-1], rg)], -1) @ w_out
    return y, yc


def hyena_filter(L, f0, fb0, f1, fb1, f2, fb2, f3, freq):
    t = jnp.linspace(0.0, 1.0, L, dtype=F32)[:, None]
    bands = (HY_EMB - 1) // 2
    w = 2.0 * math.pi * jnp.arange(L, dtype=F32)[:, None] / L
    f = jnp.linspace(1e-4, bands - 1, bands, dtype=F32)[None]
    z = jnp.concatenate([t, jnp.cos(f * w), -jnp.sin(f * w)], -1)
    fr = freq.astype(F32)
    a = jnp.sin(fr * (z @ f0.astype(F32) + fb0.astype(F32)))
    a = jnp.sin(fr * (a @ f1.astype(F32) + fb1.astype(F32)))
    a = jnp.sin(fr * (a @ f2.astype(F32) + fb2.astype(F32)))
    k = (a @ f3.astype(F32)).reshape(L, 2, D_MODEL)
    deltas = jnp.abs(jnp.linspace(HY_MIN_DECAY, HY_MAX_DECAY, D_MODEL, dtype=F32))
    k = k * jnp.exp(-t * deltas)[:, None, :]
    return k[:, 0], k[:, 1]


def bidir_fftconv(u, k_f, k_b, bias):
    L = u.shape[1]
    k = jnp.concatenate([k_f, jnp.zeros_like(k_f[:1]), k_b[:0:-1]], 0)
    uf = jnp.fft.rfft(u.astype(F32), n=2 * L, axis=1)
    kf = jnp.fft.rfft(k, n=2 * L, axis=0)
    y = jnp.fft.irfft(uf * kf[None], n=2 * L, axis=1)[:, :L]
    return (y + u.astype(F32) * bias.astype(F32)).astype(u.dtype)


def short_conv(z, w, b):
    L = z.shape[1]
    half = HY_SHORT // 2
    zp = jnp.pad(z, ((0, 0), (half, half), (0, 0)))
    return sum(zp[:, i:i + L] * w[i] for i in range(HY_SHORT)) + b


def hyena_mixer(h, w_in, b_in, w_sh, b_sh, f0, fb0, f1, fb1, f2, fb2, f3, freq, bias, w_out):
    L = h.shape[1]
    z = short_conv(h @ w_in + b_in, w_sh, b_sh)
    x0, x1, v = jnp.split(z, 3, axis=-1)
    k_f, k_b = hyena_filter(L, f0, fb0, f1, fb1, f2, fb2, f3, freq)
    y = x0 * bidir_fftconv(v * x1, k_f, k_b, bias)
    return y @ w_out


def setup_inputs(seed: int = 0) -> dict:
    key = jax.random.key(seed)
    ks = iter(jax.random.split(key, 32))
    D = D_MODEL

    def nrm(shape, s):
        return jax.random.normal(next(ks), shape, F32) * s

    expo = jnp.arange(R_HEADS, dtype=F32) + R_MIN_EXP
    decay0 = jnp.log(2.0 ** expo - 1.0)
    return {
        'x': nrm((BATCH, SEQ, D), 1.0),
        'c': nrm((BATCH, D), 1.0),
        'ctx': nrm((BATCH, CTX_LEN, D), 1.0),
        'c_ctx': nrm((D,), 1.0),
        'w_mod': nrm((DEPTH, D, N_MOD * D), 0.5 * D ** -0.5),
        'b_mod': nrm((DEPTH, N_MOD * D), 0.02),
        'norm_pre': 1.0 + nrm((DEPTH, 3, D), 0.02),
        'norm_post': 1.0 + nrm((DEPTH, 3, D), 0.02),
        'ffn_gate': nrm((DEPTH, 2, D, D_FF), D ** -0.5),
        'ffn_up': nrm((DEPTH, 2, D, D_FF), D ** -0.5),
        'ffn_down': nrm((DEPTH, 2, D_FF, D), D_FF ** -0.5),
        'mix_w_in': nrm((N_EVEN, D, IN_W), D ** -0.5),
        'attn_sink': nrm((N_EVEN, A_Q_HEADS), 0.5),
        'ret_decay': decay0 + nrm((N_EVEN, 2, R_HEADS), 0.1),
        'mix_w_out': nrm((N_EVEN, MIX_W, D), MIX_W ** -0.5),
        'hy_w_in': nrm((N_ODD, D, 3 * D), D ** -0.5),
        'hy_b_in': nrm((N_ODD, 3 * D), 0.02),
        'hy_short_w': nrm((N_ODD, HY_SHORT, 3 * D), HY_SHORT ** -0.5),
        'hy_short_b': nrm((N_ODD, 3 * D), 0.02),
        'hy_f0': nrm((N_ODD, HY_EMB, HY_ORDER), HY_EMB ** -0.5),
        'hy_fb0': nrm((N_ODD, HY_ORDER), 0.1),
        'hy_f1': nrm((N_ODD, HY_ORDER, HY_ORDER), HY_ORDER ** -0.5),
        'hy_fb1': nrm((N_ODD, HY_ORDER), 0.1),
        'hy_f2': nrm((N_ODD, HY_ORDER, HY_ORDER), HY_ORDER ** -0.5),
        'hy_fb2': nrm((N_ODD, HY_ORDER), 0.1),
        'hy_f3': nrm((N_ODD, HY_ORDER, 2 * D), 0.05 * HY_ORDER ** -0.5),
        'hy_freq': 1.0 + nrm((N_ODD, HY_ORDER), 0.02),
        'hy_bias': nrm((N_ODD, D), 1.0),
        'hy_w_out': nrm((N_ODD, D, D), D ** -0.5),
    }


def reference(x, c, ctx, c_ctx, w_mod, b_mod, norm_pre, norm_post, ffn_gate, ffn_up, ffn_down,
              mix_w_in, attn_sink, ret_decay, mix_w_out, hy_w_in, hy_b_in, hy_short_w, hy_short_b,
              hy_f0, hy_fb0, hy_f1, hy_fb1, hy_f2, hy_fb2, hy_f3, hy_freq, hy_bias, hy_w_out):
    Bn, L, D = x.shape
    rope_a = axial_rope(L)
    rope_r = line_rope(L, R_DIM)
    sc = jax.nn.silu(c)
    scc = jax.nn.silu(c_ctx)[None]
    last_reader = DEPTH - 1 if (DEPTH - 1) % 2 == 0 else DEPTH - 2
    for l in range(DEPTH):
        ctx_live = l <= last_reader
        ctx_full = l < last_reader
        m = (sc @ w_mod[l] + b_mod[l]).reshape(Bn, 1, N_MOD, D)
        mc = (scc @ w_mod[l] + b_mod[l]).reshape(1, 1, N_MOD, D)
        x = macaron_half(x, m, 0, norm_pre[l, 0], norm_post[l, 0], ffn_gate[l, 0], ffn_up[l, 0], ffn_down[l, 0])
        if ctx_live:
            ctx = macaron_half(ctx, mc, 0, norm_pre[l, 0], norm_post[l, 0],
                               ffn_gate[l, 0], ffn_up[l, 0], ffn_down[l, 0])
        h = adaln_in(x, norm_pre[l, 1], m, 1)
        hc = adaln_in(ctx, norm_pre[l, 1], mc, 1) if ctx_live else None
        if l % 2 == 0:
            e = l // 2
            y, yc = even_mixer(h, hc, mix_w_in[e], attn_sink[e], ret_decay[e], mix_w_out[e],
                               rope_a, rope_r, ctx_full)
        else:
            o = l // 2
            hp = (hy_w_in[o], hy_b_in[o], hy_short_w[o], hy_short_b[o], hy_f0[o], hy_fb0[o], hy_f1[o],
                  hy_fb1[o], hy_f2[o], hy_fb2[o], hy_f3[o], hy_freq[o], hy_bias[o], hy_w_out[o])
            y = hyena_mixer(h, *hp)
            yc = hyena_mixer(hc, *hp) if ctx_full else None
        x = adaln_out(x, y, norm_post[l, 1], m, 1, 1.0)
        x = macaron_half(x, m, 2, norm_pre[l, 2], norm_post[l, 2], ffn_gate[l, 1], ffn_up[l, 1], ffn_down[l, 1])
        if ctx_full:
            ctx = adaln_out(ctx, yc, norm_post[l, 1], mc, 1, 1.0)
            ctx = macaron_half(ctx, mc, 2, norm_pre[l, 2], norm_post[l, 2],
                               ffn_gate[l, 1], ffn_up[l, 1], ffn_down[l, 1])
    return x
```

```python
import math
from functools import partial

import jax
import jax.numpy as jnp
import numpy as np
from jax import lax
from jax.experimental import pallas as pl
from jax.experimental.pallas import tpu as pltpu

F32 = jnp.float32
BF16 = jnp.bfloat16

D_MODEL = 1024
DEPTH = 4
GRID_W = 64
EPS = 1e-6
NEG = -1e30
N_MOD = 9
FFN_RESIDUAL = 0.5
D_FF = 2816
HEAD_DIM = 64
A_Q_HEADS = 8
A_KV_HEADS = 2
WINDOW = 128
BLOCK = 128
ROPE_BASE = 10000.0
R_DIM = 64
R_HEADS = 8
R_CHUNK = 128
A_Q_W = A_Q_HEADS * HEAD_DIM
A_KV_W = A_KV_HEADS * HEAD_DIM
R_W = R_HEADS * R_DIM
Q_WIDTHS = (A_Q_W, R_W, R_W)
KV_WIDTHS = (A_KV_W, A_KV_W, R_W, R_W)
Q_SIDE = sum(Q_WIDTHS)
HY_EMB = 33
HY_SHORT = 3
HY_MAX_DECAY = math.log(1e-2) / 0.3
HY_MIN_DECAY = math.log(1e-2) / 1.5

V7X_VMEM_LIMIT_BYTES = 56 * 1024 * 1024


def _ffn_kernel(x_ref, mod_ref, gpre_ref, gpost_ref, wgu_ref, wd_ref, o_ref, *, tf, n_chunks):
    x = x_ref[...]
    shift = mod_ref[0, 0:1, :]
    scale = mod_ref[0, 1:2, :]
    gate = mod_ref[0, 2:3, :]
    inv = lax.rsqrt(jnp.mean(x * x, axis=-1, keepdims=True) + EPS)
    h = (x * inv * gpre_ref[...]) * (1.0 + scale) + shift
    hb = h.astype(BF16)
    acc = jnp.zeros(x.shape, F32)
    for j in range(n_chunks):
        gu = jnp.dot(hb, wgu_ref[:, j * 2 * tf:(j + 1) * 2 * tf], preferred_element_type=F32)
        g = gu[:, :tf]
        u = gu[:, tf:]
        a = (g * jax.nn.sigmoid(g) * u).astype(BF16)
        acc = acc + jnp.dot(a, wd_ref[j * tf:(j + 1) * tf, :], preferred_element_type=F32)
    yinv = lax.rsqrt(jnp.mean(acc * acc, axis=-1, keepdims=True) + EPS)
    o_ref[...] = x + (FFN_RESIDUAL * gate) * (acc * yinv * gpost_ref[...])


def _ffn(s, mod3, gpre, gpost, wgu, wd, *, n_rows, tm, rows_per_mod, n_lat_rows, tf):
    D = s.shape[1]
    n_tiles = n_rows // tm
    lat_tiles = n_lat_rows // tm
    tiles_per_mod = rows_per_mod // tm
    n_ctx_mod = mod3.shape[0] - 1

    def mod_map(t):
        return (jnp.where(t < lat_tiles, t // tiles_per_mod, n_ctx_mod), 0, 0)

    return pl.pallas_call(
        partial(_ffn_kernel, tf=tf, n_chunks=D_FF // tf),
        out_shape=jax.ShapeDtypeStruct((n_rows, D), F32),
        grid=(n_tiles,),
        in_specs=[
            pl.BlockSpec((tm, D), lambda t: (t, 0)),
            pl.BlockSpec((1, 3, D), mod_map),
            pl.BlockSpec((1, D), lambda t: (0, 0)),
            pl.BlockSpec((1, D), lambda t: (0, 0)),
            pl.BlockSpec((D, 2 * D_FF), lambda t: (0, 0), pipeline_mode=pl.Buffered(1)),
            pl.BlockSpec((D_FF, D), lambda t: (0, 0), pipeline_mode=pl.Buffered(1)),
        ],
        out_specs=pl.BlockSpec((tm, D), lambda t: (t, 0)),
        compiler_params=pltpu.CompilerParams(
            dimension_semantics=("arbitrary",),
            vmem_limit_bytes=V7X_VMEM_LIMIT_BYTES),
        name="ffn",
    )(s, mod3, gpre, gpost, wgu, wd)


def _pack_gate_up(w_gate, w_up, tf):
    D, F = w_gate.shape
    g = w_gate.reshape(D, F // tf, 1, tf)
    u = w_up.reshape(D, F // tf, 1, tf)
    return jnp.concatenate([g, u], axis=2).reshape(D, 2 * F).astype(BF16)


def rmsnorm(x, g):
    xf = x.astype(F32)
    y = xf * lax.rsqrt(jnp.mean(xf * xf, -1, keepdims=True) + EPS)
    return (y * g.astype(F32)).astype(x.dtype)


def adaln_in(x, g, m, i):
    return rmsnorm(x, g) * (1.0 + m[:, :, 3 * i + 1]) + m[:, :, 3 * i]


def adaln_out(x, y, g, m, i, w):
    return x + w * m[:, :, 3 * i + 2] * rmsnorm(y, g)


def split_cols(z, widths):
    idx = np.cumsum(widths)[:-1].tolist()
    return jnp.split(z, idx, axis=-1)


def heads(a, d):
    return a.reshape(a.shape[0], a.shape[1], -1, d)


def axial_rope(L):
    n_rows = L // GRID_W
    row = jnp.repeat(jnp.arange(n_rows, dtype=F32), GRID_W)
    col = jnp.tile(jnp.arange(GRID_W, dtype=F32), n_rows)
    nf = HEAD_DIM // 4
    inv = ROPE_BASE ** (-jnp.arange(nf, dtype=F32) / nf)
    ang = jnp.concatenate([row[:, None] * inv, col[:, None] * inv], -1)
    return jnp.cos(ang), jnp.sin(ang)


def line_rope(L, d):
    inv = ROPE_BASE ** (-jnp.linspace(0.0, 1.0, d // 2, dtype=F32))
    ang = jnp.arange(L, dtype=F32)[:, None] * inv
    return jnp.cos(ang), jnp.sin(ang)


def apply_rope(x, cos, sin):
    half = x.shape[-1] // 2
    x1, x2 = x[..., :half], x[..., half:]
    c, s = cos[None, :, None], sin[None, :, None]
    return jnp.concatenate([x1 * c - x2 * s, x1 * s + x2 * c], -1).astype(x.dtype)


def window_attention(q, k, v, kc, vc, sink):
    Bn, L, Hq, dh = q.shape
    G = k.shape[2]
    R = Hq // G
    nb = L // BLOCK
    T = BLOCK
    qb = q.reshape(Bn, nb, T, G, R, dh)
    pad = ((0, 0), (T, T), (0, 0), (0, 0))
    kp = jnp.pad(k, pad).reshape(Bn, nb + 2, T, G, dh)
    vp = jnp.pad(v, pad).reshape(Bn, nb + 2, T, G, dh)
    kb = jnp.concatenate([kp[:, :-2], kp[:, 1:-1], kp[:, 2:]], axis=2)
    vb = jnp.concatenate([vp[:, :-2], vp[:, 1:-1], vp[:, 2:]], axis=2)
    qi = jnp.arange(T)[:, None]
    kj = jnp.arange(3 * T)[None, :]
    kpos = jnp.arange(nb)[:, None, None] * T - T + kj
    valid = (jnp.abs(kj - T - qi) <= WINDOW)[None] & (kpos >= 0) & (kpos < L)
    scale = HEAD_DIM ** -0.5
    s_loc = jnp.einsum('bnqgrd,bnkgd->bngrqk', qb, kb).astype(F32) * scale
    s_loc = jnp.where(valid[None, :, None, None], s_loc, NEG)
    s_ctx = jnp.einsum('bnqgrd,bcgd->bngrqc', qb, kc).astype(F32) * scale
    s_sink = jnp.broadcast_to(sink.astype(F32).reshape(G, R)[None, None, :, :, None, None],
                              s_loc.shape[:-1] + (1,))
    p = jax.nn.softmax(jnp.concatenate([s_loc, s_ctx, s_sink], -1), axis=-1).astype(v.dtype)
    o = (jnp.einsum('bngrqk,bnkgd->bnqgrd', p[..., :3 * T], vb)
         + jnp.einsum('bngrqc,bcgd->bnqgrd', p[..., 3 * T:-1], vc))
    return o.reshape(Bn, L, Hq * dh)


def context_attention(qc, kc, vc, sink):
    Bn, C, Hq, dh = qc.shape
    G = kc.shape[2]
    R = Hq // G
    qg = qc.reshape(Bn, C, G, R, dh)
    s = jnp.einsum('bqgrd,bkgd->bgrqk', qg, kc).astype(F32) * HEAD_DIM ** -0.5
    s_sink = jnp.broadcast_to(sink.astype(F32).reshape(G, R)[None, :, :, None, None], s.shape[:-1] + (1,))
    p = jax.nn.softmax(jnp.concatenate([s, s_sink], -1), axis=-1)[..., :-1].astype(vc.dtype)
    return jnp.einsum('bgrqk,bkgd->bqgrd', p, vc).reshape(Bn, C, Hq * dh)


def retention_chunks(q, k, v, log_g, s0):
    Bn, L, H, _ = q.shape
    dv = v.shape[-1]
    n = L // R_CHUNK
    T = R_CHUNK
    idx = jnp.arange(T, dtype=F32)
    diff = idx[:, None] - idx[None, :]
    d_in = jnp.where(diff[None] >= 0, jnp.exp(jnp.maximum(diff, 0.0)[None] * log_g[:, None, None]), 0.0)
    d_q = jnp.exp((idx + 1.0)[None, :] * log_g[:, None])
    d_k = jnp.exp((T - 1.0 - idx)[None, :] * log_g[:, None])
    d_c = jnp.exp(T * log_g)

    def to_chunks(a):
        return a.astype(F32).reshape(Bn, n, T, H, -1).transpose(1, 0, 3, 2, 4)

    def step(S, inp):
        qi, ki, vi = inp
        att = jnp.einsum('bhtd,bhsd->bhts', qi, ki) * d_in
        o = (jnp.einsum('bhts,bhse->bhte', att, vi)
             + jnp.einsum('bhtd,bhde->bhte', qi, S) * d_q[None, :, :, None])
        S = S * d_c[None, :, None, None] + jnp.einsum('bhsd,bhse->bhde', ki * d_k[None, :, :, None], vi)
        return S, o

    S, o = lax.scan(step, s0, (to_chunks(q), to_chunks(k), to_chunks(v)))
    return o.transpose(1, 0, 3, 2, 4).reshape(Bn, L, H, dv), S


def retention_final_state(k, v, log_g):
    L = k.shape[1]
    w = jnp.exp((L - 1.0 - jnp.arange(L, dtype=F32))[:, None] * log_g[None, :])
    return jnp.einsum('bthd,bthe,th->bhde', k.astype(F32), v.astype(F32), w)


def retention_out(o, g):
    o = o * lax.rsqrt(jnp.mean(o * o, -1, keepdims=True) + EPS)
    return o.reshape(o.shape[0], o.shape[1], -1).astype(g.dtype) * jax.nn.silu(g)


def even_mixer(h, hc, w_in, sink, decay_logit, w_out, rope_a, rope_r, ctx_full):
    lg_f = jax.nn.log_sigmoid(decay_logit[0].astype(F32))
    lg_b = jax.nn.log_sigmoid(decay_logit[1].astype(F32))
    z = h @ w_in
    aq, rq, rg = split_cols(z[..., :Q_SIDE], Q_WIDTHS)
    ak, av, rk, rv = split_cols(z[..., Q_SIDE:], KV_WIDTHS)
    aq = apply_rope(heads(aq, HEAD_DIM), *rope_a)
    ak = apply_rope(heads(ak, HEAD_DIM), *rope_a)
    av = heads(av, HEAD_DIM)
    rq = apply_rope(heads(rq, R_DIM), *rope_r)
    rk = apply_rope(heads(rk, R_DIM), *rope_r) * R_DIM ** -0.5
    rv = heads(rv, R_DIM)
    if ctx_full:
        zc = hc @ w_in
        aqc, rqc, rgc = split_cols(zc[..., :Q_SIDE], Q_WIDTHS)
        akc, avc, rkc, rvc = split_cols(zc[..., Q_SIDE:], KV_WIDTHS)
    else:
        akc, avc, rkc, rvc = split_cols(hc @ w_in[:, Q_SIDE:], KV_WIDTHS)
    akc, avc = heads(akc, HEAD_DIM), heads(avc, HEAD_DIM)
    rkc, rvc = heads(rkc, R_DIM) * R_DIM ** -0.5, heads(rvc, R_DIM)
    if ctx_full:
        aqc, rqc = heads(aqc, HEAD_DIM), heads(rqc, R_DIM)
        zero = jnp.zeros((hc.shape[0], R_HEADS, R_DIM, R_DIM), F32)
        oc_f, s_f = retention_chunks(rqc, rkc, rvc, lg_f, zero)
        oc_b, s_b = retention_chunks(rqc[:, ::-1], rkc[:, ::-1], rvc[:, ::-1], lg_b, zero)
        yc = jnp.concatenate([context_attention(aqc, akc, avc, sink),
                              retention_out(oc_f + oc_b[:, ::-1], rgc)], -1) @ w_out
    else:
        s_f = retention_final_state(rkc, rvc, lg_f)
        s_b = retention_final_state(rkc[:, ::-1], rvc[:, ::-1], lg_b)
        yc = None
    o_f, _ = retention_chunks(rq, rk, rv, lg_f, s_f)
    o_b, _ = retention_chunks(rq[:, ::-1], rk[:, ::-1], rv[:, ::-1], lg_b, s_b)
    a = window_attention(aq, ak, av, akc, avc, sink)
    y = jnp.concatenate([a, retention_out(o_f + o_b[:, ::-1], rg)], -1) @ w_out
    return y, yc


def hyena_filter(L, f0, fb0, f1, fb1, f2, fb2, f3, freq):
    t = jnp.linspace(0.0, 1.0, L, dtype=F32)[:, None]
    bands = (HY_EMB - 1) // 2
    w = 2.0 * math.pi * jnp.arange(L, dtype=F32)[:, None] / L
    f = jnp.linspace(1e-4, bands - 1, bands, dtype=F32)[None]
    z = jnp.concatenate([t, jnp.cos(f * w), -jnp.sin(f * w)], -1)
    fr = freq.astype(F32)
    a = jnp.sin(fr * (z @ f0.astype(F32) + fb0.astype(F32)))
    a = jnp.sin(fr * (a @ f1.astype(F32) + fb1.astype(F32)))
    a = jnp.sin(fr * (a @ f2.astype(F32) + fb2.astype(F32)))
    k = (a @ f3.astype(F32)).reshape(L, 2, D_MODEL)
    deltas = jnp.abs(jnp.linspace(HY_MIN_DECAY, HY_MAX_DECAY, D_MODEL, dtype=F32))
    k = k * jnp.exp(-t * deltas)[:, None, :]
    return k[:, 0], k[:, 1]


def bidir_fftconv(u, k_f, k_b, bias):
    L = u.shape[1]
    k = jnp.concatenate([k_f, jnp.zeros_like(k_f[:1]), k_b[:0:-1]], 0)
    uf = jnp.fft.rfft(u.astype(F32), n=2 * L, axis=1)
    kf = jnp.fft.rfft(k, n=2 * L, axis=0)
    y = jnp.fft.irfft(uf * kf[None], n=2 * L, axis=1)[:, :L]
    return (y + u.astype(F32) * bias.astype(F32)).astype(u.dtype)


def short_conv(z, w, b):
    L = z.shape[1]
    half = HY_SHORT // 2
    zp = jnp.pad(z, ((0, 0), (half, half), (0, 0)))
    return sum(zp[:, i:i + L] * w[i] for i in range(HY_SHORT)) + b


def hyena_mixer(h, w_in, b_in, w_sh, b_sh, f0, fb0, f1, fb1, f2, fb2, f3, freq, bias, w_out):
    L = h.shape[1]
    z = short_conv(h @ w_in + b_in, w_sh, b_sh)
    x0, x1, v = jnp.split(z, 3, axis=-1)
    k_f, k_b = hyena_filter(L, f0, fb0, f1, fb1, f2, fb2, f3, freq)
    y = x0 * bidir_fftconv(v * x1, k_f, k_b, bias)
    return y @ w_out


FFN_TM = 512
FFN_TF = 256


def kernel(x, c, ctx, c_ctx, w_mod, b_mod, norm_pre, norm_post, ffn_gate, ffn_up, ffn_down,
           mix_w_in, attn_sink, ret_decay, mix_w_out, hy_w_in, hy_b_in, hy_short_w, hy_short_b,
           hy_f0, hy_fb0, hy_f1, hy_fb1, hy_f2, hy_fb2, hy_f3, hy_freq, hy_bias, hy_w_out):
    Bn, L, D = x.shape
    C = ctx.shape[1]
    n_lat = Bn * L
    n_all = n_lat + Bn * C
    rope_a = axial_rope(L)
    rope_r = line_rope(L, R_DIM)
    sc = jax.nn.silu(c)
    scc = jax.nn.silu(c_ctx)[None]
    last_reader = DEPTH - 1 if (DEPTH - 1) % 2 == 0 else DEPTH - 2

    def ffn(s, mod_all, l, i, j, n_rows):
        mod3 = mod_all[:, 3 * i:3 * i + 3]
        wgu = _pack_gate_up(ffn_gate[l, j], ffn_up[l, j], FFN_TF)
        wd = ffn_down[l, j].astype(BF16)
        return _ffn(s, mod3, norm_pre[l, i][None], norm_post[l, i][None], wgu, wd,
                    n_rows=n_rows, tm=FFN_TM, rows_per_mod=L, n_lat_rows=n_lat, tf=FFN_TF)

    s = jnp.concatenate([x.reshape(n_lat, D), ctx.reshape(Bn * C, D)], axis=0)
    for l in range(DEPTH):
        ctx_live = l <= last_reader
        ctx_full = l < last_reader
        m = (sc @ w_mod[l] + b_mod[l]).reshape(Bn, N_MOD, D)
        mc = (scc @ w_mod[l] + b_mod[l]).reshape(1, N_MOD, D)
        mod_all = jnp.concatenate([m, mc], axis=0)
        n_rows = n_all if ctx_live else n_lat
        s = ffn(s, mod_all, l, 0, 0, n_rows)
        xl = s[:n_lat].reshape(Bn, L, D)
        cl = s[n_lat:].reshape(Bn, C, D) if ctx_live else None
        m4 = m[:, None]
        mc4 = mc[:, None]
        h = adaln_in(xl, norm_pre[l, 1], m4, 1)
        hc = adaln_in(cl, norm_pre[l, 1], mc4, 1) if ctx_live else None
        if l % 2 == 0:
            e = l // 2
            y, yc = even_mixer(h, hc, mix_w_in[e], attn_sink[e], ret_decay[e], mix_w_out[e],
                               rope_a, rope_r, ctx_full)
        else:
            o = l // 2
            hp = (hy_w_in[o], hy_b_in[o], hy_short_w[o], hy_short_b[o], hy_f0[o], hy_fb0[o], hy_f1[o],
                  hy_fb1[o], hy_f2[o], hy_fb2[o], hy_f3[o], hy_freq[o], hy_bias[o], hy_w_out[o])
            y = hyena_mixer(h, *hp)
            yc = hyena_mixer(hc, *hp) if ctx_full else None
        xl = adaln_out(xl, y, norm_post[l, 1], m4, 1, 1.0)
        if ctx_full:
            cl = adaln_out(cl, yc, norm_post[l, 1], mc4, 1, 1.0)
            s = jnp.concatenate([xl.reshape(n_lat, D), cl.reshape(Bn * C, D)], axis=0)
            n_rows = n_all
        else:
            s = xl.reshape(n_lat, D)
            n_rows = n_lat
        s = ffn(s, mod_all, l, 2, 1, n_rows)
    return s[:n_lat].reshape(Bn, L, D)
```

```python
import math
from functools import partial

import jax
import jax.numpy as jnp
from jax import lax
from jax.experimental import pallas as pl
from jax.experimental.pallas import tpu as pltpu

F32 = jnp.float32
BF16 = jnp.bfloat16

DEPTH = 4
GRID_W = 64
EPS = 1e-6
NEG = -1e30
N_MOD = 9
FFN_RESIDUAL = 0.5
HEAD_DIM = 64
A_Q_HEADS = 8
A_KV_HEADS = 2
WINDOW = 128
BLOCK = 128
ROPE_BASE = 10000.0
R_DIM = 64
R_HEADS = 8
R_CHUNK = 128
HY_EMB = 33
HY_EMB_PAD = 64
HY_MAX_DECAY = math.log(1e-2) / 0.3
HY_MIN_DECAY = math.log(1e-2) / 1.5

LANES = 128
SUBLANES = 8
V7X_VMEM_LIMIT_BYTES = 56 * 1024 * 1024

FFN_TM = 512
FFN_TF = 256
PROJ_TM = 512
HY_TM = 256
MOD_TN = 2304
DFT_TF = 256


def _cparams(*sem):
    return pltpu.CompilerParams(dimension_semantics=sem, vmem_limit_bytes=V7X_VMEM_LIMIT_BYTES)


def _const_spec(shape):
    zeros = (0,) * len(shape)
    return pl.BlockSpec(shape, lambda *_: zeros, pipeline_mode=pl.Buffered(1))


def _mod_spec(layer, tm, n_lat, seq_len, n_batch):
    def index(t, *_):
        return (layer, jnp.where(t * tm < n_lat, (t * tm) // seq_len, n_batch), 0, 0)
    return index


def _adaln_in(x, mod_ref, gpre_ref, i):
    shift = mod_ref[0, 0, 3 * i:3 * i + 1, :]
    scale = mod_ref[0, 0, 3 * i + 1:3 * i + 2, :]
    inv = lax.rsqrt(jnp.mean(x * x, axis=-1, keepdims=True) + EPS)
    return (x * inv * gpre_ref[...]) * (1.0 + scale) + shift


def _adaln_out(x, y, mod_ref, gpost_ref, i, w):
    gate = mod_ref[0, 0, 3 * i + 2:3 * i + 3, :]
    inv = lax.rsqrt(jnp.mean(y * y, axis=-1, keepdims=True) + EPS)
    return x + (w * gate) * (y * inv * gpost_ref[...])


def _mod_kernel(c_ref, w_ref, b_ref, o_ref):
    c = c_ref[...]
    a = (c * jax.nn.sigmoid(c)).astype(BF16)
    o_ref[0] = jnp.dot(a, w_ref[0].astype(BF16), preferred_element_type=F32) + b_ref[0]


def _modulation(c_all, w_mod, b_mod):
    depth, D, W = w_mod.shape
    rows = c_all.shape[0]
    return pl.pallas_call(
        _mod_kernel,
        out_shape=jax.ShapeDtypeStruct((depth, rows, W), F32),
        grid=(depth, W // MOD_TN),
        in_specs=[
            pl.BlockSpec((rows, D), lambda l, j: (0, 0)),
            pl.BlockSpec((1, D, MOD_TN), lambda l, j: (l, 0, j)),
            pl.BlockSpec((1, 1, MOD_TN), lambda l, j: (l, 0, j)),
        ],
        out_specs=pl.BlockSpec((1, rows, MOD_TN), lambda l, j: (l, 0, j)),
        compiler_params=_cparams("arbitrary", "arbitrary"),
        name="modulation",
    )(c_all, w_mod, b_mod.reshape(depth, 1, W))


def _ffn_kernel(x_ref, mod_ref, gpre_ref, gpost_ref, wgu_ref, wd_ref, o_ref, *, i, tf, n_chunks):
    x = x_ref[...]
    hb = _adaln_in(x, mod_ref, gpre_ref, i).astype(BF16)
    acc = jnp.zeros(x.shape, F32)
    for j in range(n_chunks):
        gu = jnp.dot(hb, wgu_ref[:, j * 2 * tf:(j + 1) * 2 * tf], preferred_element_type=F32)
        g = gu[:, :tf]
        u = gu[:, tf:]
        a = (g * jax.nn.sigmoid(g) * u).astype(BF16)
        acc = acc + jnp.dot(a, wd_ref[j * tf:(j + 1) * tf, :], preferred_element_type=F32)
    o_ref[...] = _adaln_out(x, acc, mod_ref, gpost_ref, i, FFN_RESIDUAL)


def _ffn(s, mod, gpre, gpost, wgu, wd, *, layer, i, n_rows, geom):
    D = s.shape[1]
    d_ff = wd.shape[0]
    tm = FFN_TM
    return pl.pallas_call(
        partial(_ffn_kernel, i=i, tf=FFN_TF, n_chunks=d_ff // FFN_TF),
        out_shape=jax.ShapeDtypeStruct((n_rows, D), F32),
        grid=(n_rows // tm,),
        in_specs=[
            pl.BlockSpec((tm, D), lambda t: (t, 0)),
            pl.BlockSpec((1, 1, N_MOD, D), _mod_spec(layer, tm, *geom)),
            _const_spec((1, D)),
            _const_spec((1, D)),
            _const_spec((D, 2 * d_ff)),
            _const_spec((d_ff, D)),
        ],
        out_specs=pl.BlockSpec((tm, D), lambda t: (t, 0)),
        compiler_params=_cparams("arbitrary"),
        name="ffn",
    )(s, mod, gpre, gpost, wgu, wd)


def _pack_gate_up(w_gate, w_up, tf):
    D, F = w_gate.shape
    g = w_gate.reshape(D, F // tf, 1, tf)
    u = w_up.reshape(D, F // tf, 1, tf)
    return jnp.concatenate([g, u], axis=2).reshape(D, 2 * F).astype(BF16)


_AQ = (0, 512)
_RQ = (512, 1024)
_RG = (1024, 1536)
_AK = (1536, 1792)
_AV = (1792, 2048)
_RK = (2048, 2560)
_RV = (2560, 3072)
EVEN_W = 3072


def _pack_even_w_in(w_in):
    aq, rq, rg = w_in[:, 0:512], w_in[:, 512:1024], w_in[:, 1024:1536]
    ak, av = w_in[:, 1536:1664], w_in[:, 1664:1792]
    rk, rv = w_in[:, 1792:2304], w_in[:, 2304:2816]

    def dup(a):
        g0, g1 = a[:, :HEAD_DIM], a[:, HEAD_DIM:]
        return jnp.concatenate([g0, g0, g1, g1], axis=1)

    return jnp.concatenate([aq, rq, rg, dup(ak), dup(av), rk, rv], axis=1).astype(BF16)


def _rope_tables(ang, ident_rows):
    L = ang.shape[0]
    cos, sin = jnp.cos(ang), jnp.sin(ang)
    zero = jnp.zeros_like(sin)
    cos_t = jnp.tile(cos, (1, 4))
    sin_a = jnp.tile(jnp.concatenate([-sin, zero], axis=1), (1, 2))
    sin_b = jnp.tile(jnp.concatenate([zero, sin], axis=1), (1, 2))
    pad_one = jnp.ones((ident_rows, LANES), F32)
    pad_zero = jnp.zeros((ident_rows, LANES), F32)
    return (jnp.concatenate([cos_t, pad_one], 0), jnp.concatenate([sin_a, pad_zero], 0),
            jnp.concatenate([sin_b, pad_zero], 0))


def _axial_angles(L):
    n_rows = L // GRID_W
    row = jnp.repeat(jnp.arange(n_rows, dtype=F32), GRID_W)
    col = jnp.tile(jnp.arange(GRID_W, dtype=F32), n_rows)
    nf = HEAD_DIM // 4
    inv = ROPE_BASE ** (-jnp.arange(nf, dtype=F32) / nf)
    return jnp.concatenate([row[:, None] * inv, col[:, None] * inv], -1)


def _line_angles(L):
    inv = ROPE_BASE ** (-jnp.linspace(0.0, 1.0, R_DIM // 2, dtype=F32))
    return jnp.arange(L, dtype=F32)[:, None] * inv


def _rope(z, cos, sin_a, sin_b):
    outs = []
    for c in range(z.shape[1] // LANES):
        zc = z[:, c * LANES:(c + 1) * LANES]
        outs.append(zc * cos + pltpu.roll(zc, 96, 1) * sin_a + pltpu.roll(zc, 32, 1) * sin_b)
    return outs[0] if len(outs) == 1 else jnp.concatenate(outs, axis=1)


def _even_in_kernel(x_ref, mod_ref, gpre_ref, w_ref, ca_ref, saa_ref, sab_ref, cr_ref, sra_ref, srb_ref,
                    aq_ref, rq_ref, rg_ref, ak_ref, av_ref, rk_ref, rv_ref):
    hb = _adaln_in(x_ref[...], mod_ref, gpre_ref, 1).astype(BF16)

    def proj(cols):
        return jnp.dot(hb, w_ref[:, cols[0]:cols[1]], preferred_element_type=F32)

    rope_a = (ca_ref[...], saa_ref[...], sab_ref[...])
    rope_r = (cr_ref[...], sra_ref[...], srb_ref[...])
    aq_ref[...] = _rope(proj(_AQ), *rope_a).astype(BF16)
    rq_ref[...] = _rope(proj(_RQ), *rope_r).astype(BF16)
    rg_ref[...] = proj(_RG)
    ak_ref[...] = _rope(proj(_AK), *rope_a).astype(BF16)
    av_ref[...] = proj(_AV).astype(BF16)
    rk_ref[...] = (_rope(proj(_RK), *rope_r) * R_DIM ** -0.5).astype(BF16)
    rv_ref[...] = proj(_RV).astype(BF16)


def _even_in(s, mod, gpre, w_pack, rope_a, rope_r, *, layer, n_rows, geom):
    n_lat, seq_len, _ = geom
    D = s.shape[1]
    tm = PROJ_TM
    tiles_per_seq = seq_len // tm

    def rope_index(t):
        return (jnp.where(t * tm < n_lat, t % tiles_per_seq, tiles_per_seq), 0)

    rope_spec = pl.BlockSpec((tm, LANES), rope_index)

    def out(width, dtype):
        return jax.ShapeDtypeStruct((n_rows, width), dtype), pl.BlockSpec((tm, width), lambda t: (t, 0))

    outs = [out(512, BF16), out(512, BF16), out(512, F32), out(256, BF16), out(256, BF16),
            out(512, BF16), out(512, BF16)]
    return pl.pallas_call(
        _even_in_kernel,
        out_shape=[o[0] for o in outs],
        grid=(n_rows // tm,),
        in_specs=[
            pl.BlockSpec((tm, D), lambda t: (t, 0)),
            pl.BlockSpec((1, 1, N_MOD, D), _mod_spec(layer, tm, *geom)),
            _const_spec((1, D)),
            _const_spec((D, EVEN_W)),
        ] + [rope_spec] * 6,
        out_specs=[o[1] for o in outs],
        compiler_params=_cparams("arbitrary"),
        name="even_in",
    )(s, mod, gpre, w_pack, *rope_a, *rope_r)


def _attn_kernel(q_ref, kp_ref, kc_ref, kn_ref, kx_ref, vp_ref, vc_ref, vn_ref, vx_ref, sink_ref, o_ref,
                 *, nb):
    T = BLOCK
    j = pl.program_id(1)
    k_lo = jnp.where(j > 0, 0, T)
    k_hi = jnp.where(j < nb - 1, 3 * T, jnp.where(j < nb, 2 * T, 0))
    qi = lax.broadcasted_iota(jnp.int32, (4 * T, 1), 0) & (T - 1)
    kj = lax.broadcasted_iota(jnp.int32, (1, 3 * T), 1)
    rel = kj - T - qi
    valid = (jnp.abs(rel) <= WINDOW) & (kj >= k_lo) & (kj < k_hi)
    lo = lax.broadcasted_iota(jnp.int32, (1, LANES), 1) < HEAD_DIM
    q = q_ref[...].astype(F32)
    scale = HEAD_DIM ** -0.5
    for g in range(A_KV_HEADS):
        gl = slice(g * LANES, (g + 1) * LANES)
        kd = jnp.concatenate([kp_ref[:, gl], kc_ref[:, gl], kn_ref[:, gl], kx_ref[:, gl]], axis=0)
        vd = jnp.concatenate([vp_ref[:, gl], vc_ref[:, gl], vn_ref[:, gl], vx_ref[:, gl]], axis=0)
        qs = []
        for c in range(2):
            qc = q[:, (2 * g + c) * LANES:(2 * g + c + 1) * LANES]
            qs.append(jnp.where(lo, qc, 0.0))
            qs.append(jnp.where(lo, 0.0, qc))
        q_stack = jnp.concatenate(qs, axis=0).astype(BF16)
        s = lax.dot_general(q_stack, kd, (((1,), (1,)), ((), ())), preferred_element_type=F32) * scale
        s_loc = jnp.where(valid, s[:, :3 * T], NEG)
        s_ctx = s[:, 3 * T:]
        sink = jnp.concatenate(
            [jnp.broadcast_to(sink_ref[4 * g + k:4 * g + k + 1, 0:1], (T, 1)) for k in range(4)], axis=0)
        m = jnp.maximum(jnp.maximum(jnp.max(s_loc, axis=-1, keepdims=True),
                                    jnp.max(s_ctx, axis=-1, keepdims=True)), sink)
        p_loc = jnp.exp(s_loc - m)
        p_ctx = jnp.exp(s_ctx - m)
        denom = (jnp.sum(p_loc, axis=-1, keepdims=True) + jnp.sum(p_ctx, axis=-1, keepdims=True)
                 + jnp.exp(sink - m))
        p = jnp.concatenate([p_loc, p_ctx], axis=1).astype(BF16)
        o = jnp.dot(p, vd, preferred_element_type=F32) * (1.0 / denom)
        for c in range(2):
            oc = jnp.where(lo, o[(2 * c) * T:(2 * c + 1) * T], o[(2 * c + 1) * T:(2 * c + 2) * T])
            o_ref[:, (2 * g + c) * LANES:(2 * g + c + 1) * LANES] = oc.astype(BF16)


def _attention(aq, akd, avd, sink_lanes, *, n_batch, seq_len, ctx_len, with_ctx_queries):
    T = BLOCK
    nb = seq_len // T
    ncb = ctx_len // T
    n_lat = n_batch * seq_len
    n_q_blocks = nb + (ncb if with_ctx_queries else 0)
    n_rows = n_lat + (n_batch * ctx_len if with_ctx_queries else 0)

    def q_index(b, j):
        return (jnp.where(j < nb, b * nb + j, n_lat // T + b * ncb + (j - nb)), 0)

    def kv_index(off):
        def index(b, j):
            return (b * nb + jnp.clip(j + off, 0, nb - 1), 0)
        return index

    def ctx_index(b, j):
        return (n_lat // ctx_len + b, 0)

    kv_specs = [pl.BlockSpec((T, 2 * LANES), kv_index(-1)), pl.BlockSpec((T, 2 * LANES), kv_index(0)),
                pl.BlockSpec((T, 2 * LANES), kv_index(1)), pl.BlockSpec((ctx_len, 2 * LANES), ctx_index)]
    return pl.pallas_call(
        partial(_attn_kernel, nb=nb),
        out_shape=jax.ShapeDtypeStruct((n_rows, A_Q_HEADS * HEAD_DIM), BF16),
        grid=(n_batch, n_q_blocks),
        in_specs=[pl.BlockSpec((T, A_Q_HEADS * HEAD_DIM), q_index)] + kv_specs + kv_specs
        + [_const_spec((A_Q_HEADS, LANES))],
        out_specs=pl.BlockSpec((T, A_Q_HEADS * HEAD_DIM), q_index),
        compiler_params=_cparams("arbitrary", "arbitrary"),
        name="attention",
    )(aq, akd, akd, akd, akd, avd, avd, avd, avd, sink_lanes)


def _log_sigmoid(x):
    return jnp.minimum(x, 0.0) - jnp.log(1.0 + jnp.exp(-jnp.abs(x)))


def _head_block_diag(n):
    r = lax.broadcasted_iota(jnp.int32, (n, 1), 0) >> 6
    c = lax.broadcasted_iota(jnp.int32, (1, n), 1) >> 6
    return r == c


def _ret_state_kernel(kf_ref, vf_ref, kb_ref, vb_ref, dec_ref, sf_ref, sb_ref, sf_acc, sb_acc):
    T = R_CHUNK
    W = R_HEADS * R_DIM

    @pl.when(pl.program_id(1) == 0)
    def _():
        sf_acc[...] = jnp.zeros_like(sf_acc)
        sb_acc[...] = jnp.zeros_like(sb_acc)

    lg_f = _log_sigmoid(dec_ref[0:1, :])
    lg_b = _log_sigmoid(dec_ref[1:2, :])
    t = lax.broadcasted_iota(jnp.int32, (T, 1), 0).astype(F32)
    bd = _head_block_diag(W)
    fold = ((lax.broadcasted_iota(jnp.int32, (W, 1), 0) & (R_DIM - 1))
            == lax.broadcasted_iota(jnp.int32, (1, R_DIM), 1)).astype(BF16)

    def step(acc, s_ref, k_ref, v_ref, d_k, lg):
        s_ref[0] = jnp.dot(acc[...].astype(BF16), fold, preferred_element_type=F32).astype(BF16)
        kd = (k_ref[...].astype(F32) * d_k).T.astype(BF16)
        kv = jnp.dot(kd, v_ref[...], preferred_element_type=F32)
        acc[...] = acc[...] * jnp.exp(T * lg) + jnp.where(bd, kv, 0.0)

    step(sf_acc, sf_ref, kf_ref, vf_ref, jnp.exp((T - 1.0 - t) * lg_f), lg_f)
    step(sb_acc, sb_ref, kb_ref, vb_ref, jnp.exp(t * lg_b), lg_b)


def _ret_chunk_maps(n_batch, seq_len, ctx_len):
    T = R_CHUNK
    ncc, nlc = ctx_len // T, seq_len // T
    ctx_base = n_batch * nlc

    def fwd(b, j):
        return jnp.where(j < ncc, ctx_base + b * ncc + j, b * nlc + (j - ncc))

    def bwd(b, j):
        return jnp.where(j < ncc, ctx_base + b * ncc + (ncc - 1 - j), b * nlc + (nlc - 1 - (j - ncc)))

    return ncc + nlc, fwd, bwd


def _retention_states(rk, rv, dec_lanes, *, n_batch, seq_len, ctx_len):
    T = R_CHUNK
    W = R_HEADS * R_DIM
    n_steps, fwd, bwd = _ret_chunk_maps(n_batch, seq_len, ctx_len)
    n_chunks = n_batch * n_steps
    rows = pl.BlockSpec((T, W), lambda b, j: (fwd(b, j), 0))
    rows_b = pl.BlockSpec((T, W), lambda b, j: (bwd(b, j), 0))
    state = jax.ShapeDtypeStruct((n_chunks, W, R_DIM), BF16)
    return pl.pallas_call(
        _ret_state_kernel,
        out_shape=[state, state],
        grid=(n_batch, n_steps),
        in_specs=[rows, rows, rows_b, rows_b, _const_spec((2, W))],
        out_specs=[pl.BlockSpec((1, W, R_DIM), lambda b, j: (fwd(b, j), 0, 0)),
                   pl.BlockSpec((1, W, R_DIM), lambda b, j: (bwd(b, j), 0, 0))],
        scratch_shapes=[pltpu.VMEM((W, W), F32), pltpu.VMEM((W, W), F32)],
        compiler_params=_cparams("arbitrary", "arbitrary"),
        name="retention_states",
    )(rk, rv, rk, rv, dec_lanes)


def _ret_out_kernel(q_ref, k_ref, v_ref, g_ref, sf_ref, sb_ref, dec_ref, o_ref):
    T = R_CHUNK
    W = R_HEADS * R_DIM
    lg_f = _log_sigmoid(dec_ref[0:1, :])
    lg_b = _log_sigmoid(dec_ref[1:2, :])
    t = lax.broadcasted_iota(jnp.int32, (T, 1), 0).astype(F32)
    lane_h = lax.broadcasted_iota(jnp.int32, (1, W), 1) >> 6
    bd = _head_block_diag(W)
    q = q_ref[...]
    qf = q.astype(F32)

    q_stack = jnp.concatenate([jnp.where(lane_h == h, qf, 0.0) for h in range(R_HEADS)], axis=0).astype(BF16)
    sc = lax.dot_general(q_stack, k_ref[...], (((1,), (1,)), ((), ())), preferred_element_type=F32)
    tt = (lax.broadcasted_iota(jnp.int32, (R_HEADS * T, 1), 0) & (T - 1)).astype(F32)
    ss = lax.broadcasted_iota(jnp.int32, (1, T), 1).astype(F32)
    d = tt - ss
    a_rows = jnp.concatenate(
        [jnp.broadcast_to(lg_f[0:1, h * R_DIM:h * R_DIM + 1], (T, 1)) for h in range(R_HEADS)], axis=0)
    b_rows = jnp.concatenate(
        [jnp.broadcast_to(lg_b[0:1, h * R_DIM:h * R_DIM + 1], (T, 1)) for h in range(R_HEADS)], axis=0)
    decay = jnp.where(d > 0, jnp.exp(a_rows * d), jnp.where(d < 0, jnp.exp(b_rows * (-d)), 2.0))
    att = (sc * decay).astype(BF16)
    oh = jnp.dot(att, v_ref[...], preferred_element_type=F32)
    o = jnp.zeros((T, W), F32)
    for h in range(R_HEADS):
        o = o + jnp.where(lane_h == h, oh[h * T:(h + 1) * T], 0.0)

    fold_t = (lax.broadcasted_iota(jnp.int32, (R_DIM, 1), 0)
              == (lax.broadcasted_iota(jnp.int32, (1, W), 1) & (R_DIM - 1))).astype(BF16)

    def expand(s_ref):
        return jnp.where(bd, jnp.dot(s_ref[0], fold_t, preferred_element_type=F32), 0.0).astype(BF16)

    o = o + jnp.dot(q, expand(sf_ref), preferred_element_type=F32) * jnp.exp((t + 1.0) * lg_f)
    o = o + jnp.dot(q, expand(sb_ref), preferred_element_type=F32) * jnp.exp((T - t) * lg_b)

    o2 = o * o
    hi = o2.astype(BF16)
    lo = (o2 - hi.astype(F32)).astype(BF16)
    avg = jnp.where(bd, 1.0 / R_DIM, 0.0).astype(BF16)
    ms = jnp.dot(hi, avg, preferred_element_type=F32) + jnp.dot(lo, avg, preferred_element_type=F32)
    g = g_ref[...]
    o_ref[...] = (o * lax.rsqrt(ms + EPS) * (g * jax.nn.sigmoid(g))).astype(BF16)


def _retention_out(rq, rk, rv, rg, sf, sb, dec_lanes, *, n_rows):
    T = R_CHUNK
    W = R_HEADS * R_DIM
    rows = pl.BlockSpec((T, W), lambda c: (c, 0))
    st = pl.BlockSpec((1, W, R_DIM), lambda c: (c, 0, 0))
    return pl.pallas_call(
        _ret_out_kernel,
        out_shape=jax.ShapeDtypeStruct((n_rows, W), BF16),
        grid=(n_rows // T,),
        in_specs=[rows, rows, rows, rows, st, st, _const_spec((2, W))],
        out_specs=rows,
        compiler_params=_cparams("arbitrary"),
        name="retention_out",
    )(rq, rk, rv, rg, sf, sb, dec_lanes)


def _out_proj_kernel(*refs, n_in):
    ins = refs[:n_in]
    s_ref, mod_ref, gpost_ref, w_ref, o_ref = refs[n_in:]
    y = None
    row = 0
    for r in ins:
        width = r.shape[1]
        part = jnp.dot(r[...], w_ref[row:row + width, :], preferred_element_type=F32)
        y = part if y is None else y + part
        row += width
    o_ref[...] = _adaln_out(s_ref[...], y, mod_ref, gpost_ref, 1, 1.0)


def _out_proj(parts, s, mod, gpost, w_out, *, layer, n_rows, geom):
    D = s.shape[1]
    tm = PROJ_TM
    return pl.pallas_call(
        partial(_out_proj_kernel, n_in=len(parts)),
        out_shape=jax.ShapeDtypeStruct((n_rows, D), F32),
        grid=(n_rows // tm,),
        in_specs=[pl.BlockSpec((tm, p.shape[1]), lambda t: (t, 0)) for p in parts] + [
            pl.BlockSpec((tm, D), lambda t: (t, 0)),
            pl.BlockSpec((1, 1, N_MOD, D), _mod_spec(layer, tm, *geom)),
            _const_spec((1, D)),
            _const_spec(w_out.shape),
        ],
        out_specs=pl.BlockSpec((tm, D), lambda t: (t, 0)),
        compiler_params=_cparams("arbitrary"),
        name="out_proj",
    )(*parts, s, mod, gpost, w_out)


def _hy_in_kernel(x_ref, xp_ref, xn_ref, mod_ref, gpre_ref, w_ref, bin_ref, wsh_ref, bsh_ref,
                  x0_ref, u_ref, *, tm, n_lat, tiles_per_seq, tiles_per_ctx):
    D = x_ref.shape[1]
    H = SUBLANES
    t = pl.program_id(0)
    lat_tiles = n_lat // tm
    pos = jnp.where(t < lat_tiles, t % tiles_per_seq, (t - lat_tiles) % tiles_per_ctx)
    n_pos = jnp.where(t < lat_tiles, tiles_per_seq, tiles_per_ctx)
    xs = jnp.concatenate([xp_ref[...], x_ref[...], xn_ref[...]], axis=0)
    hb = _adaln_in(xs, mod_ref, gpre_ref, 1).astype(BF16)
    row = lax.broadcasted_iota(jnp.int32, (tm, 1), 0)
    keep_prev = row != jnp.where(pos == 0, 0, -1)
    keep_next = row != jnp.where(pos == n_pos - 1, tm - 1, -1)
    n_ext = tm + 2 * H

    def conv(c):
        cols = slice(c * D, (c + 1) * D)
        z = jnp.dot(hb, w_ref[:, cols], preferred_element_type=F32) + bin_ref[:, cols]
        zm = jnp.where(keep_prev, pltpu.roll(z, 1, 0)[H:H + tm], 0.0)
        zp = jnp.where(keep_next, pltpu.roll(z, n_ext - 1, 0)[H:H + tm], 0.0)
        return (zm * wsh_ref[0:1, cols] + z[H:H + tm] * wsh_ref[1:2, cols] + zp * wsh_ref[2:3, cols]
                + bsh_ref[:, cols])

    x0_ref[...] = conv(0).astype(BF16)
    u_ref[...] = (conv(2) * conv(1)).astype(BF16)


def _hy_in(s, mod, gpre, w_in, b_in, w_sh, b_sh, *, layer, n_rows, geom, ctx_len):
    n_lat, seq_len, _ = geom
    D = s.shape[1]
    tm = HY_TM
    H = SUBLANES
    blocks_per_tile = tm // H
    last_block = s.shape[0] // H - 1
    kern = partial(_hy_in_kernel, tm=tm, n_lat=n_lat, tiles_per_seq=seq_len // tm,
                   tiles_per_ctx=max(ctx_len // tm, 1))
    out = jax.ShapeDtypeStruct((n_rows, D), BF16)
    return pl.pallas_call(
        kern,
        out_shape=[out, out],
        grid=(n_rows // tm,),
        in_specs=[
            pl.BlockSpec((tm, D), lambda t: (t, 0)),
            pl.BlockSpec((H, D), lambda t: (jnp.maximum(t * blocks_per_tile - 1, 0), 0)),
            pl.BlockSpec((H, D), lambda t: (jnp.minimum((t + 1) * blocks_per_tile, last_block), 0)),
            pl.BlockSpec((1, 1, N_MOD, D), _mod_spec(layer, tm, *geom)),
            _const_spec((1, D)),
            _const_spec((D, 3 * D)),
            _const_spec((1, 3 * D)),
            _const_spec((3, 3 * D)),
            _const_spec((1, 3 * D)),
        ],
        out_specs=[pl.BlockSpec((tm, D), lambda t: (t, 0))] * 2,
        compiler_params=_cparams("arbitrary"),
        name="hyena_in",
    )(s, s, s, mod, gpre, w_in, b_in, w_sh, b_sh)


def _hy_filter_kernel(z_ref, f0_ref, fb0_ref, f1_ref, fb1_ref, f2_ref, fb2_ref, f3_ref, fr_ref, dl_ref,
                      ksum_ref, kdiff_ref):
    D = dl_ref.shape[1]
    hp = lax.Precision.HIGHEST
    z = z_ref[...]
    fr = fr_ref[...]
    a = jnp.sin(fr * (jnp.dot(z, f0_ref[...], precision=hp, preferred_element_type=F32) + fb0_ref[...]))
    a = jnp.sin(fr * (jnp.dot(a, f1_ref[...], precision=hp, preferred_element_type=F32) + fb1_ref[...]))
    a = jnp.sin(fr * (jnp.dot(a, f2_ref[...], precision=hp, preferred_element_type=F32) + fb2_ref[...]))
    k = jnp.dot(a, f3_ref[...], precision=hp, preferred_element_type=F32)
    decay = jnp.exp(-z[:, 0:1] * dl_ref[...])
    k_f = k[:, :D] * decay
    k_b = k[:, D:] * decay
    row = lax.broadcasted_iota(jnp.int32, (z.shape[0], 1), 0) + pl.program_id(0) * z.shape[0]
    k_b = jnp.where(row == 0, 0.0, k_b)
    ksum_ref[...] = (k_f + k_b).astype(BF16)
    kdiff_ref[...] = (k_f - k_b).astype(BF16)


def _hy_filter(L, f0, fb0, f1, fb1, f2, fb2, f3, freq, D):
    t = jnp.linspace(0.0, 1.0, L, dtype=F32)[:, None]
    bands = (HY_EMB - 1) // 2
    w = 2.0 * math.pi * jnp.arange(L, dtype=F32)[:, None] / L
    f = jnp.linspace(1e-4, bands - 1, bands, dtype=F32)[None]
    z = jnp.concatenate([t, jnp.cos(f * w), -jnp.sin(f * w),
                         jnp.zeros((L, HY_EMB_PAD - HY_EMB), F32)], -1)
    f0p = jnp.concatenate([f0, jnp.zeros((HY_EMB_PAD - HY_EMB, f0.shape[1]), F32)], 0)
    deltas = jnp.abs(jnp.linspace(HY_MIN_DECAY, HY_MAX_DECAY, D, dtype=F32))[None]
    tl = min(512, L)
    O = f0.shape[1]
    out = jax.ShapeDtypeStruct((L, D), BF16)
    return pl.pallas_call(
        _hy_filter_kernel,
        out_shape=[out, out],
        grid=(L // tl,),
        in_specs=[pl.BlockSpec((tl, HY_EMB_PAD), lambda i: (i, 0)),
                  _const_spec((HY_EMB_PAD, O)), _const_spec((1, O)),
                  _const_spec((O, O)), _const_spec((1, O)),
                  _const_spec((O, O)), _const_spec((1, O)),
                  _const_spec((O, 2 * D)), _const_spec((1, O)), _const_spec((1, D))],
        out_specs=[pl.BlockSpec((tl, D), lambda i: (i, 0))] * 2,
        compiler_params=_cparams("arbitrary"),
        name="hyena_filter",
    )(z, f0p, fb0[None], f1, fb1[None], f2, fb2[None], f3, freq[None], deltas)


def _dft_matrices(L, tf):
    n_hi = L // 64 if L >= 64 else 1
    n_lo = L // n_hi
    k2 = 2 * jnp.arange(L, dtype=jnp.int32)[:, None] + 1
    step = 2.0 * math.pi / (4 * L)

    def angles(n):
        return ((k2 * n[None, :]) % (4 * L)).astype(F32) * step

    a_hi = angles(jnp.arange(n_hi, dtype=jnp.int32) * n_lo)
    a_lo = angles(jnp.arange(n_lo, dtype=jnp.int32))
    c1, s1 = jnp.cos(a_hi)[:, :, None], jnp.sin(a_hi)[:, :, None]
    c0, s0 = jnp.cos(a_lo)[:, None, :], jnp.sin(a_lo)[:, None, :]
    cos = (c1 * c0 - s1 * s0).reshape(L, L)
    nsin = (-(s1 * c0 + c1 * s0)).reshape(L, L)
    fwd = jnp.concatenate([cos.reshape(L // tf, tf, L), nsin.reshape(L // tf, tf, L)], axis=1).reshape(2 * L, L)
    inv = fwd.T * (1.0 / L)
    return fwd.astype(BF16), inv.astype(BF16)


def _spectrum_kernel(w_ref, ksum_ref, kdiff_ref, o_ref):
    tf = w_ref.shape[0] // 2
    o_ref[0:tf, :] = jnp.dot(w_ref[0:tf, :], ksum_ref[...], preferred_element_type=F32)
    o_ref[tf:, :] = jnp.dot(w_ref[tf:, :], kdiff_ref[...], preferred_element_type=F32)


def _filter_spectrum(fwd, ksum, kdiff, tf):
    L, D = ksum.shape
    return pl.pallas_call(
        _spectrum_kernel,
        out_shape=jax.ShapeDtypeStruct((2 * L, D), F32),
        grid=(L // tf,),
        in_specs=[pl.BlockSpec((2 * tf, L), lambda j: (j, 0)), _const_spec((L, D)), _const_spec((L, D))],
        out_specs=pl.BlockSpec((2 * tf, D), lambda j: (j, 0)),
        compiler_params=_cparams("arbitrary"),
        name="filter_spectrum",
    )(fwd, ksum, kdiff)


def _dft_fwd_kernel(w_ref, u_ref, kh_ref, o_ref):
    tf = w_ref.shape[0] // 2
    uh = jnp.dot(w_ref[...], u_ref[...], preferred_element_type=F32)
    ur, ui = uh[:tf], uh[tf:]
    kr, ki = kh_ref[0:tf, :], kh_ref[tf:, :]
    o_ref[0:tf, :] = (ur * kr - ui * ki).astype(BF16)
    o_ref[tf:, :] = (ur * ki + ui * kr).astype(BF16)


def _dft_forward(fwd, u, khat, *, n_batch, L, row_off, tf):
    D = u.shape[1]
    seq0 = row_off // L
    return pl.pallas_call(
        _dft_fwd_kernel,
        out_shape=jax.ShapeDtypeStruct((n_batch * 2 * L, D), BF16),
        grid=(n_batch, L // tf),
        in_specs=[pl.BlockSpec((2 * tf, L), lambda b, j: (j, 0)),
                  pl.BlockSpec((L, D), lambda b, j: (seq0 + b, 0)),
                  pl.BlockSpec((2 * tf, D), lambda b, j: (j, 0))],
        out_specs=pl.BlockSpec((2 * tf, D), lambda b, j: (b * (L // tf) + j, 0)),
        compiler_params=_cparams("arbitrary", "arbitrary"),
        name="dft_forward",
    )(fwd, u, khat)


def _dft_inv_kernel(w_ref, y_ref, u_ref, x0_ref, bias_ref, *rest):
    o_ref, acc = rest[-2], rest[-1]
    kk = pl.program_id(2)

    @pl.when(kk == 0)
    def _():
        acc[...] = jnp.zeros_like(acc)

    acc[...] += jnp.dot(w_ref[...], y_ref[...], preferred_element_type=F32)

    @pl.when(kk == pl.num_programs(2) - 1)
    def _():
        y = acc[...] + u_ref[...].astype(F32) * bias_ref[...]
        o_ref[...] = (x0_ref[...].astype(F32) * y).astype(BF16)


def _dft_inverse(inv, yhat, u, x0, bias, prev, *, n_batch, L, row_off, n_rows_out):
    D = u.shape[1]
    tm = min(1024, L)
    tk = min(2048, 2 * L)
    m_tiles = L // tm
    k_tiles = 2 * L // tk
    blk0 = row_off // tm

    def rows(b, i, kk):
        return (blk0 + b * m_tiles + i, 0)

    in_specs = [pl.BlockSpec((tm, tk), lambda b, i, kk: (i, kk)),
                pl.BlockSpec((tk, D), lambda b, i, kk: (b * k_tiles + kk, 0)),
                pl.BlockSpec((tm, D), rows), pl.BlockSpec((tm, D), rows), _const_spec((1, D))]
    args = [inv, yhat, u, x0, bias]
    aliases = {}
    if prev is not None:
        in_specs.append(pl.BlockSpec(memory_space=pl.ANY))
        args.append(prev)
        aliases = {5: 0}
    return pl.pallas_call(
        _dft_inv_kernel,
        out_shape=jax.ShapeDtypeStruct((n_rows_out, D), BF16),
        grid=(n_batch, m_tiles, k_tiles),
        in_specs=in_specs,
        out_specs=pl.BlockSpec((tm, D), rows),
        scratch_shapes=[pltpu.VMEM((tm, D), F32)],
        input_output_aliases=aliases,
        compiler_params=_cparams("arbitrary", "arbitrary", "arbitrary"),
        name="dft_inverse",
    )(*args)


def kernel(x, c, ctx, c_ctx, w_mod, b_mod, norm_pre, norm_post, ffn_gate, ffn_up, ffn_down,
           mix_w_in, attn_sink, ret_decay, mix_w_out, hy_w_in, hy_b_in, hy_short_w, hy_short_b,
           hy_f0, hy_fb0, hy_f1, hy_fb1, hy_f2, hy_fb2, hy_f3, hy_freq, hy_bias, hy_w_out):
    Bn, L, D = x.shape
    C = ctx.shape[1]
    n_lat = Bn * L
    n_all = n_lat + Bn * C
    geom = (n_lat, L, Bn)
    assert L % PROJ_TM == 0 and L % HY_TM == 0 and n_lat % C == 0 and (Bn * C) % PROJ_TM == 0
    assert C % R_CHUNK == 0 and L % R_CHUNK == 0 and (C % HY_TM == 0 or HY_TM % C == 0)
    last_reader = DEPTH - 1 if (DEPTH - 1) % 2 == 0 else DEPTH - 2

    c_all = jnp.concatenate([c, c_ctx[None], jnp.zeros((SUBLANES - (Bn + 1) % SUBLANES, D), F32)], axis=0)
    mod = _modulation(c_all, w_mod, b_mod).reshape(DEPTH, c_all.shape[0], N_MOD, D)

    rope_a = _rope_tables(_axial_angles(L), PROJ_TM)
    rope_r = _rope_tables(_line_angles(L), PROJ_TM)
    dft = {}

    def dft_mats(length):
        if length not in dft:
            dft[length] = _dft_matrices(length, min(DFT_TF, length))
        return dft[length]

    def ffn(s, l, i, j, n_rows):
        wgu = _pack_gate_up(ffn_gate[l, j], ffn_up[l, j], FFN_TF)
        wd = ffn_down[l, j].astype(BF16)
        return _ffn(s, mod, norm_pre[l, i][None], norm_post[l, i][None], wgu, wd,
                    layer=l, i=i, n_rows=n_rows, geom=geom)

    s = jnp.concatenate([x.reshape(n_lat, D), ctx.reshape(Bn * C, D)], axis=0)
    for l in range(DEPTH):
        ctx_live = l <= last_reader
        ctx_full = l < last_reader
        n_in = n_all if ctx_live else n_lat
        n_out = n_all if ctx_full else n_lat
        s = ffn(s, l, 0, 0, n_in)
        gpre, gpost = norm_pre[l, 1][None], norm_post[l, 1][None]
        if l % 2 == 0:
            e = l // 2
            aq, rq, rg, akd, avd, rk, rv = _even_in(s, mod, gpre, _pack_even_w_in(mix_w_in[e]), rope_a, rope_r,
                                                    layer=l, n_rows=n_in, geom=geom)
            sink_lanes = jnp.broadcast_to(attn_sink[e][:, None], (A_Q_HEADS, LANES))
            dec_lanes = jnp.repeat(ret_decay[e], R_DIM, axis=1)
            a = _attention(aq, akd, avd, sink_lanes, n_batch=Bn, seq_len=L, ctx_len=C,
                           with_ctx_queries=ctx_full)
            sf, sb = _retention_states(rk, rv, dec_lanes, n_batch=Bn, seq_len=L, ctx_len=C)
            r = _retention_out(rq, rk, rv, rg, sf, sb, dec_lanes, n_rows=n_out)
            s = _out_proj([a, r], s, mod, gpost, mix_w_out[e].astype(BF16), layer=l, n_rows=n_out, geom=geom)
        else:
            o = l // 2
            x0, u = _hy_in(s, mod, gpre, hy_w_in[o].astype(BF16), hy_b_in[o][None], hy_short_w[o],
                           hy_short_b[o][None], layer=l, n_rows=n_in, geom=geom, ctx_len=C)
            filt = (hy_f0[o], hy_fb0[o], hy_f1[o], hy_fb1[o], hy_f2[o], hy_fb2[o], hy_f3[o], hy_freq[o])
            bias = hy_bias[o][None]
            yg = None
            for length, row_off in ((L, 0), (C, n_lat)) if ctx_full else ((L, 0),):
                tf = min(DFT_TF, length)
                fwd, inv = dft_mats(length)
                ksum, kdiff = _hy_filter(length, *filt, D)
                khat = _filter_spectrum(fwd, ksum, kdiff, tf)
                yhat = _dft_forward(fwd, u, khat, n_batch=Bn, L=length, row_off=row_off, tf=tf)
                yg = _dft_inverse(inv, yhat, u, x0, bias, yg, n_batch=Bn, L=length, row_off=row_off,
                                  n_rows_out=n_out)
            s = _out_proj([yg], s, mod, gpost, hy_w_out[o].astype(BF16), layer=l, n_rows=n_out, geom=geom)
        s = ffn(s, l, 2, 1, n_out)
    return s[:n_lat].reshape(Bn, L, D)
```

```python
import math
from functools import partial

import jax
import jax.numpy as jnp
from jax import lax
from jax.experimental import pallas as pl
from jax.experimental.pallas import tpu as pltpu

F32 = jnp.float32
BF16 = jnp.bfloat16

DEPTH = 4
GRID_W = 64
EPS = 1e-6
NEG = -1e30
N_MOD = 9
FFN_RESIDUAL = 0.5
HEAD_DIM = 64
A_Q_HEADS = 8
A_KV_HEADS = 2
WINDOW = 128
BLOCK = 128
ROPE_BASE = 10000.0
R_DIM = 64
R_HEADS = 8
R_CHUNK = 128
HY_EMB = 33
HY_EMB_PAD = 64
HY_MAX_DECAY = math.log(1e-2) / 0.3
HY_MIN_DECAY = math.log(1e-2) / 1.5

LANES = 128
SUBLANES = 8
V7X_VMEM_LIMIT_BYTES = 56 * 1024 * 1024

FFN_TM = 512
FFN_TF = 256
PROJ_TM = 512
HY_TM = 256
MOD_TN = 2304
DFT_TF = 256
RET_CH = 2
ATT_RC = 64


def _cparams(*sem):
    return pltpu.CompilerParams(dimension_semantics=sem, vmem_limit_bytes=V7X_VMEM_LIMIT_BYTES)


def _const_spec(shape):
    zeros = (0,) * len(shape)
    return pl.BlockSpec(shape, lambda *_: zeros, pipeline_mode=pl.Buffered(1))


def _mod_spec(layer, tm, n_lat, seq_len, n_batch):
    def index(t, *_):
        return (layer, jnp.where(t * tm < n_lat, (t * tm) // seq_len, n_batch), 0, 0)
    return index


def _adaln_in(x, mod_ref, gpre_ref, i):
    shift = mod_ref[0, 0, 3 * i:3 * i + 1, :]
    scale = mod_ref[0, 0, 3 * i + 1:3 * i + 2, :]
    inv = lax.rsqrt(jnp.mean(x * x, axis=-1, keepdims=True) + EPS)
    return (x * inv * gpre_ref[...]) * (1.0 + scale) + shift


def _adaln_out(x, y, mod_ref, gpost_ref, i, w):
    gate = mod_ref[0, 0, 3 * i + 2:3 * i + 3, :]
    inv = lax.rsqrt(jnp.mean(y * y, axis=-1, keepdims=True) + EPS)
    return x + (w * gate) * (y * inv * gpost_ref[...])


def _mod_kernel(c_ref, w_ref, b_ref, o_ref):
    c = c_ref[...]
    a = (c * jax.nn.sigmoid(c)).astype(BF16)
    o_ref[0] = jnp.dot(a, w_ref[0].astype(BF16), preferred_element_type=F32) + b_ref[0]


def _modulation(c_all, w_mod, b_mod):
    depth, D, W = w_mod.shape
    rows = c_all.shape[0]
    return pl.pallas_call(
        _mod_kernel,
        out_shape=jax.ShapeDtypeStruct((depth, rows, W), F32),
        grid=(depth, W // MOD_TN),
        in_specs=[
            pl.BlockSpec((rows, D), lambda l, j: (0, 0)),
            pl.BlockSpec((1, D, MOD_TN), lambda l, j: (l, 0, j)),
            pl.BlockSpec((1, 1, MOD_TN), lambda l, j: (l, 0, j)),
        ],
        out_specs=pl.BlockSpec((1, rows, MOD_TN), lambda l, j: (l, 0, j)),
        compiler_params=_cparams("arbitrary", "arbitrary"),
        name="modulation",
    )(c_all, w_mod, b_mod.reshape(depth, 1, W))


def _ffn_kernel(x_ref, mod_ref, gpre_ref, gpost_ref, wg_ref, wu_ref, wd_ref, o_ref, *, i, tf, n_chunks):
    x = x_ref[...]
    hb = _adaln_in(x, mod_ref, gpre_ref, i).astype(BF16)
    acc = jnp.zeros(x.shape, F32)
    for j in range(n_chunks):
        cols = slice(j * tf, (j + 1) * tf)
        g = jnp.dot(hb, wg_ref[:, cols], preferred_element_type=F32)
        u = jnp.dot(hb, wu_ref[:, cols], preferred_element_type=F32)
        a = (g * jax.nn.sigmoid(g) * u).astype(BF16)
        acc = acc + jnp.dot(a, wd_ref[cols, :], preferred_element_type=F32)
    o_ref[...] = _adaln_out(x, acc, mod_ref, gpost_ref, i, FFN_RESIDUAL)


def _ffn(s, mod, gpre, gpost, wg, wu, wd, *, layer, i, n_rows, geom):
    D = s.shape[1]
    d_ff = wd.shape[0]
    tm = FFN_TM
    return pl.pallas_call(
        partial(_ffn_kernel, i=i, tf=FFN_TF, n_chunks=d_ff // FFN_TF),
        out_shape=jax.ShapeDtypeStruct((n_rows, D), F32),
        grid=(n_rows // tm,),
        in_specs=[
            pl.BlockSpec((tm, D), lambda t: (t, 0)),
            pl.BlockSpec((1, 1, N_MOD, D), _mod_spec(layer, tm, *geom)),
            _const_spec((1, D)),
            _const_spec((1, D)),
            _const_spec((D, d_ff)),
            _const_spec((D, d_ff)),
            _const_spec((d_ff, D)),
        ],
        out_specs=pl.BlockSpec((tm, D), lambda t: (t, 0)),
        compiler_params=_cparams("arbitrary"),
        name="ffn",
    )(s, mod, gpre, gpost, wg, wu, wd)


_AQ = (0, 512)
_RQ = (512, 1024)
_RG = (1024, 1536)
_AK = (1536, 1792)
_AV = (1792, 2048)
_RK = (2048, 2560)
_RV = (2560, 3072)
EVEN_W = 3072


def _pack_even_w_in(w_in):
    aq, rq, rg = w_in[:, 0:512], w_in[:, 512:1024], w_in[:, 1024:1536]
    ak, av = w_in[:, 1536:1664], w_in[:, 1664:1792]
    rk, rv = w_in[:, 1792:2304], w_in[:, 2304:2816]

    def dup(a):
        g0, g1 = a[:, :HEAD_DIM], a[:, HEAD_DIM:]
        return jnp.concatenate([g0, g0, g1, g1], axis=1)

    return jnp.concatenate([aq, rq, rg, dup(ak), dup(av), rk, rv], axis=1).astype(BF16)


def _rope_tables(ang, ident_rows):
    L = ang.shape[0]
    cos, sin = jnp.cos(ang), jnp.sin(ang)
    zero = jnp.zeros_like(sin)
    cos_t = jnp.tile(cos, (1, 4))
    sin_a = jnp.tile(jnp.concatenate([-sin, zero], axis=1), (1, 2))
    sin_b = jnp.tile(jnp.concatenate([zero, sin], axis=1), (1, 2))
    pad_one = jnp.ones((ident_rows, LANES), F32)
    pad_zero = jnp.zeros((ident_rows, LANES), F32)
    return (jnp.concatenate([cos_t, pad_one], 0), jnp.concatenate([sin_a, pad_zero], 0),
            jnp.concatenate([sin_b, pad_zero], 0))


def _axial_angles(L):
    n_rows = L // GRID_W
    row = jnp.repeat(jnp.arange(n_rows, dtype=F32), GRID_W)
    col = jnp.tile(jnp.arange(GRID_W, dtype=F32), n_rows)
    nf = HEAD_DIM // 4
    inv = ROPE_BASE ** (-jnp.arange(nf, dtype=F32) / nf)
    return jnp.concatenate([row[:, None] * inv, col[:, None] * inv], -1)


def _line_angles(L):
    inv = ROPE_BASE ** (-jnp.linspace(0.0, 1.0, R_DIM // 2, dtype=F32))
    return jnp.arange(L, dtype=F32)[:, None] * inv


def _rope(z, cos, sin_a, sin_b):
    outs = []
    for c in range(z.shape[1] // LANES):
        zc = z[:, c * LANES:(c + 1) * LANES]
        outs.append(zc * cos + pltpu.roll(zc, 96, 1) * sin_a + pltpu.roll(zc, 32, 1) * sin_b)
    return outs[0] if len(outs) == 1 else jnp.concatenate(outs, axis=1)


def _even_in_kernel(x_ref, mod_ref, gpre_ref, w_ref, ca_ref, saa_ref, sab_ref, cr_ref, sra_ref, srb_ref,
                    aq_ref, rq_ref, rg_ref, ak_ref, av_ref, rk_ref, rv_ref):
    hb = _adaln_in(x_ref[...], mod_ref, gpre_ref, 1).astype(BF16)

    def proj(cols):
        return jnp.dot(hb, w_ref[:, cols[0]:cols[1]], preferred_element_type=F32)

    rope_a = (ca_ref[...], saa_ref[...], sab_ref[...])
    rope_r = (cr_ref[...], sra_ref[...], srb_ref[...])
    aq_ref[...] = (_rope(proj(_AQ), *rope_a) * HEAD_DIM ** -0.5).astype(BF16)
    rq_ref[...] = _rope(proj(_RQ), *rope_r).astype(BF16)
    rg_ref[...] = proj(_RG)
    ak_ref[...] = _rope(proj(_AK), *rope_a).astype(BF16)
    av_ref[...] = proj(_AV).astype(BF16)
    rk_ref[...] = (_rope(proj(_RK), *rope_r) * R_DIM ** -0.5).astype(BF16)
    rv_ref[...] = proj(_RV).astype(BF16)


def _even_in(s, mod, gpre, w_pack, rope_a, rope_r, *, layer, n_rows, geom):
    n_lat, seq_len, _ = geom
    D = s.shape[1]
    tm = PROJ_TM
    tiles_per_seq = seq_len // tm

    def rope_index(t):
        return (jnp.where(t * tm < n_lat, t % tiles_per_seq, tiles_per_seq), 0)

    rope_spec = pl.BlockSpec((tm, LANES), rope_index)

    def out(width, dtype):
        return jax.ShapeDtypeStruct((n_rows, width), dtype), pl.BlockSpec((tm, width), lambda t: (t, 0))

    outs = [out(512, BF16), out(512, BF16), out(512, F32), out(256, BF16), out(256, BF16),
            out(512, BF16), out(512, BF16)]
    return pl.pallas_call(
        _even_in_kernel,
        out_shape=[o[0] for o in outs],
        grid=(n_rows // tm,),
        in_specs=[
            pl.BlockSpec((tm, D), lambda t: (t, 0)),
            pl.BlockSpec((1, 1, N_MOD, D), _mod_spec(layer, tm, *geom)),
            _const_spec((1, D)),
            _const_spec((D, EVEN_W)),
        ] + [rope_spec] * 6,
        out_specs=[o[1] for o in outs],
        compiler_params=_cparams("arbitrary"),
        name="even_in",
    )(s, mod, gpre, w_pack, *rope_a, *rope_r)


def _attn_kernel(q_ref, kp_ref, kc_ref, kn_ref, kx_ref, vp_ref, vc_ref, vn_ref, vx_ref, sink_ref, band_ref,
                 o_ref, *, nb):
    T = BLOCK
    j = pl.program_id(1)
    k_lo = jnp.where(j > 0, 0, T)
    k_hi = jnp.where(j < nb - 1, 3 * T, jnp.where(j < nb, 2 * T, 0))
    kj = lax.broadcasted_iota(jnp.int32, (1, band_ref.shape[1]), 1)
    exists = ((kj >= k_lo) & (kj < k_hi)) | (kj >= 3 * T)
    bias = band_ref[...] + jnp.where(exists, 0.0, NEG)
    lo = lax.broadcasted_iota(jnp.int32, (1, LANES), 1) < HEAD_DIM
    q = q_ref[...].astype(F32)
    for g in range(A_KV_HEADS):
        gl = slice(g * LANES, (g + 1) * LANES)
        kd = jnp.concatenate([kp_ref[:, gl], kc_ref[:, gl], kn_ref[:, gl], kx_ref[:, gl]], axis=0)
        vd = jnp.concatenate([vp_ref[:, gl], vc_ref[:, gl], vn_ref[:, gl], vx_ref[:, gl]], axis=0)
        qs = []
        for c in range(2):
            qc = q[:, (2 * g + c) * LANES:(2 * g + c + 1) * LANES]
            qs.append(jnp.where(lo, qc, 0.0))
            qs.append(jnp.where(lo, 0.0, qc))
        q_stack = jnp.concatenate(qs, axis=0).astype(BF16)
        s_all = lax.dot_general(q_stack, kd, (((1,), (1,)), ((), ())), preferred_element_type=F32)
        ps, inv = [], []
        for rc in range(4 * T // ATT_RC):
            rows = slice(rc * ATT_RC, (rc + 1) * ATT_RC)
            h = 4 * g + (rc * ATT_RC) // T
            sink = sink_ref[h:h + 1, 0:1]
            s = s_all[rows] + bias[rows]
            m = jnp.maximum(jnp.max(s, axis=-1, keepdims=True), sink)
            p = jnp.exp(s - m)
            inv.append(1.0 / (jnp.sum(p, axis=-1, keepdims=True) + jnp.exp(sink - m)))
            ps.append(p.astype(BF16))
        p = jnp.concatenate(ps, axis=0)
        o = jnp.dot(p, vd, preferred_element_type=F32) * jnp.concatenate(inv, axis=0)
        for c in range(2):
            oc = jnp.where(lo, o[(2 * c) * T:(2 * c + 1) * T], o[(2 * c + 1) * T:(2 * c + 2) * T])
            o_ref[:, (2 * g + c) * LANES:(2 * g + c + 1) * LANES] = oc.astype(BF16)


def _attention(aq, akd, avd, sink_lanes, *, n_batch, seq_len, ctx_len, with_ctx_queries):
    T = BLOCK
    nb = seq_len // T
    ncb = ctx_len // T
    n_lat = n_batch * seq_len
    n_q_blocks = nb + (ncb if with_ctx_queries else 0)
    n_rows = n_lat + (n_batch * ctx_len if with_ctx_queries else 0)

    def q_index(b, j):
        return (jnp.where(j < nb, b * nb + j, n_lat // T + b * ncb + (j - nb)), 0)

    def kv_index(off):
        def index(b, j):
            return (b * nb + jnp.clip(j + off, 0, nb - 1), 0)
        return index

    def ctx_index(b, j):
        return (n_lat // ctx_len + b, 0)

    kv_specs = [pl.BlockSpec((T, 2 * LANES), kv_index(-1)), pl.BlockSpec((T, 2 * LANES), kv_index(0)),
                pl.BlockSpec((T, 2 * LANES), kv_index(1)), pl.BlockSpec((ctx_len, 2 * LANES), ctx_index)]
    qi = jnp.arange(4 * T, dtype=jnp.int32)[:, None] % T
    kj = jnp.arange(3 * T + ctx_len, dtype=jnp.int32)[None, :]
    band = jnp.where((jnp.abs(kj - T - qi) <= WINDOW) | (kj >= 3 * T), 0.0, NEG).astype(F32)
    return pl.pallas_call(
        partial(_attn_kernel, nb=nb),
        out_shape=jax.ShapeDtypeStruct((n_rows, A_Q_HEADS * HEAD_DIM), BF16),
        grid=(n_batch, n_q_blocks),
        in_specs=[pl.BlockSpec((T, A_Q_HEADS * HEAD_DIM), q_index)] + kv_specs + kv_specs
        + [_const_spec((A_Q_HEADS, LANES)), _const_spec(band.shape)],
        out_specs=pl.BlockSpec((T, A_Q_HEADS * HEAD_DIM), q_index),
        compiler_params=_cparams("arbitrary", "arbitrary"),
        name="attention",
    )(aq, akd, akd, akd, akd, avd, avd, avd, avd, sink_lanes, band)


def _log_sigmoid(x):
    return jnp.minimum(x, 0.0) - jnp.log(1.0 + jnp.exp(-jnp.abs(x)))


def _lo_head():
    return lax.broadcasted_iota(jnp.int32, (1, LANES), 1) < R_DIM


_ROW_DK_F, _ROW_DK_B, _ROW_DQ_F, _ROW_DQ_B, _ROW_DC = (i * R_CHUNK for i in range(5))
_RET_ROWS = 4 * R_CHUNK + SUBLANES


def _ret_tables_kernel(dec_ref, tt_ref, rows_ref):
    T = R_CHUNK
    lg_f = _log_sigmoid(dec_ref[0:1, :])
    lg_b = _log_sigmoid(dec_ref[1:2, :])
    t = lax.broadcasted_iota(jnp.int32, (T, 1), 0).astype(F32)
    rows_ref[_ROW_DK_F:_ROW_DK_F + T, :] = jnp.exp((T - 1.0 - t) * lg_f)
    rows_ref[_ROW_DK_B:_ROW_DK_B + T, :] = jnp.exp(t * lg_b)
    rows_ref[_ROW_DQ_F:_ROW_DQ_F + T, :] = jnp.exp((t + 1.0) * lg_f)
    rows_ref[_ROW_DQ_B:_ROW_DQ_B + T, :] = jnp.exp((T - t) * lg_b)
    rows_ref[_ROW_DC:_ROW_DC + SUBLANES, :] = jnp.concatenate(
        [jnp.exp(T * lg_f), jnp.exp(T * lg_b), jnp.zeros((SUBLANES - 2, lg_f.shape[1]), F32)], axis=0)
    d = t - lax.broadcasted_iota(jnp.int32, (1, T), 1).astype(F32)
    for h in range(R_HEADS):
        a = lg_f[0:1, h * R_DIM:h * R_DIM + 1]
        b = lg_b[0:1, h * R_DIM:h * R_DIM + 1]
        fwd = jnp.exp(jnp.maximum(d, 0.0) * a)
        bwd = jnp.exp(jnp.maximum(-d, 0.0) * b)
        tt_ref[h * T:(h + 1) * T, :] = jnp.where(d > 0, fwd, jnp.where(d < 0, bwd, 2.0))


def _retention_tables(dec_lanes):
    T = R_CHUNK
    W = dec_lanes.shape[1]
    return pl.pallas_call(
        _ret_tables_kernel,
        out_shape=[jax.ShapeDtypeStruct((R_HEADS * T, T), F32), jax.ShapeDtypeStruct((_RET_ROWS, W), F32)],
        grid=(1,),
        in_specs=[_const_spec((2, W))],
        out_specs=[pl.BlockSpec((R_HEADS * T, T), lambda i: (0, 0)), pl.BlockSpec((_RET_ROWS, W), lambda i: (0, 0))],
        compiler_params=_cparams("arbitrary"),
        name="retention_tables",
    )(dec_lanes)


def _ret_state_kernel(kf_ref, vf_ref, kb_ref, vb_ref, rows_ref, sf_ref, sb_ref, sf_acc, sb_acc):
    T = R_CHUNK

    @pl.when(pl.program_id(1) == 0)
    def _():
        sf_acc[...] = jnp.zeros_like(sf_acc)
        sb_acc[...] = jnp.zeros_like(sb_acc)

    lo = _lo_head()

    def chunk(acc, s_ref, k_ref, v_ref, ci, d_k, d_c):
        s_ref[ci] = acc[...].astype(BF16)
        r = slice(ci * T, (ci + 1) * T)
        kd = k_ref[r, :].astype(F32) * d_k
        for p in range(R_HEADS // 2):
            ls = slice(p * LANES, (p + 1) * LANES)
            kv = jnp.dot(kd[:, ls].T.astype(BF16), v_ref[r, ls], preferred_element_type=F32)
            rs = slice(p * R_DIM, (p + 1) * R_DIM)
            acc[rs, :] = acc[rs, :] * d_c[:, ls] + jnp.where(lo, kv[0:R_DIM], kv[R_DIM:])

    dkf = rows_ref[_ROW_DK_F:_ROW_DK_F + T, :]
    dkb = rows_ref[_ROW_DK_B:_ROW_DK_B + T, :]
    dcf = rows_ref[_ROW_DC:_ROW_DC + 1, :]
    dcb = rows_ref[_ROW_DC + 1:_ROW_DC + 2, :]
    for ci in range(RET_CH):
        chunk(sf_acc, sf_ref, kf_ref, vf_ref, ci, dkf, dcf)
    for ci in reversed(range(RET_CH)):
        chunk(sb_acc, sb_ref, kb_ref, vb_ref, ci, dkb, dcb)


def _ret_step_maps(n_batch, seq_len, ctx_len):
    rows = RET_CH * R_CHUNK
    ncs, nls = ctx_len // rows, seq_len // rows
    ctx_base = n_batch * nls

    def fwd(b, j):
        return jnp.where(j < ncs, ctx_base + b * ncs + j, b * nls + (j - ncs))

    def bwd(b, j):
        return jnp.where(j < ncs, ctx_base + b * ncs + (ncs - 1 - j), b * nls + (nls - 1 - (j - ncs)))

    return ncs + nls, fwd, bwd


def _retention_states(rk, rv, tab_rows, *, n_batch, seq_len, ctx_len):
    T = R_CHUNK
    W = R_HEADS * R_DIM
    n_steps, fwd, bwd = _ret_step_maps(n_batch, seq_len, ctx_len)
    n_chunks = n_batch * n_steps * RET_CH
    rows = pl.BlockSpec((RET_CH * T, W), lambda b, j: (fwd(b, j), 0))
    rows_b = pl.BlockSpec((RET_CH * T, W), lambda b, j: (bwd(b, j), 0))
    state = jax.ShapeDtypeStruct((n_chunks, W // 2, LANES), BF16)
    return pl.pallas_call(
        _ret_state_kernel,
        out_shape=[state, state],
        grid=(n_batch, n_steps),
        in_specs=[rows, rows, rows_b, rows_b, _const_spec(tab_rows.shape)],
        out_specs=[pl.BlockSpec((RET_CH, W // 2, LANES), lambda b, j: (fwd(b, j), 0, 0)),
                   pl.BlockSpec((RET_CH, W // 2, LANES), lambda b, j: (bwd(b, j), 0, 0))],
        scratch_shapes=[pltpu.VMEM((W // 2, LANES), F32), pltpu.VMEM((W // 2, LANES), F32)],
        compiler_params=_cparams("arbitrary", "arbitrary"),
        name="retention_states",
    )(rk, rv, rk, rv, tab_rows)


def _ret_out_kernel(q_ref, k_ref, v_ref, g_ref, sf_ref, sb_ref, tt_ref, rows_ref, o_ref):
    T = R_CHUNK
    lo = _lo_head()
    ri = lax.broadcasted_iota(jnp.int32, (LANES, 1), 0) >> 6
    ci = lax.broadcasted_iota(jnp.int32, (1, LANES), 1) >> 6
    avg = jnp.where(ri == ci, 1.0 / R_DIM, 0.0).astype(BF16)

    def state_block(s):
        s = s.astype(F32)
        return jnp.concatenate([jnp.where(lo, s, 0.0), jnp.where(lo, 0.0, s)], axis=0).astype(BF16)

    for c in range(RET_CH):
        r = slice(c * T, (c + 1) * T)
        for p in range(R_HEADS // 2):
            ls = slice(p * LANES, (p + 1) * LANES)
            q = q_ref[r, ls]
            qf = q.astype(F32)
            q_stack = jnp.concatenate([jnp.where(lo, qf, 0.0), jnp.where(lo, 0.0, qf)], axis=0).astype(BF16)
            sc = lax.dot_general(q_stack, k_ref[r, ls], (((1,), (1,)), ((), ())), preferred_element_type=F32)
            att = (sc * tt_ref[2 * p * T:(2 * p + 2) * T, :]).astype(BF16)
            oh = jnp.dot(att, v_ref[r, ls], preferred_element_type=F32)
            o = jnp.where(lo, oh[:T], oh[T:])
            rs = slice(p * R_DIM, (p + 1) * R_DIM)
            o = o + (jnp.dot(q, state_block(sf_ref[c, rs, :]), preferred_element_type=F32)
                     * rows_ref[_ROW_DQ_F:_ROW_DQ_F + T, ls])
            o = o + (jnp.dot(q, state_block(sb_ref[c, rs, :]), preferred_element_type=F32)
                     * rows_ref[_ROW_DQ_B:_ROW_DQ_B + T, ls])
            o2 = o * o
            hi = o2.astype(BF16)
            rest = (o2 - hi.astype(F32)).astype(BF16)
            ms = jnp.dot(hi, avg, preferred_element_type=F32) + jnp.dot(rest, avg, preferred_element_type=F32)
            g = g_ref[r, ls]
            o_ref[r, ls] = (o * lax.rsqrt(ms + EPS) * (g * jax.nn.sigmoid(g))).astype(BF16)


def _retention_out(rq, rk, rv, rg, sf, sb, tab_tt, tab_rows, *, n_rows):
    T = R_CHUNK
    W = R_HEADS * R_DIM
    rows = pl.BlockSpec((RET_CH * T, W), lambda c: (c, 0))
    st = pl.BlockSpec((RET_CH, W // 2, LANES), lambda c: (c, 0, 0))
    return pl.pallas_call(
        _ret_out_kernel,
        out_shape=jax.ShapeDtypeStruct((n_rows, W), BF16),
        grid=(n_rows // (RET_CH * T),),
        in_specs=[rows, rows, rows, rows, st, st, _const_spec(tab_tt.shape), _const_spec(tab_rows.shape)],
        out_specs=rows,
        compiler_params=_cparams("arbitrary"),
        name="retention_out",
    )(rq, rk, rv, rg, sf, sb, tab_tt, tab_rows)


def _out_proj_kernel(*refs, n_in):
    ins = refs[:n_in]
    s_ref, mod_ref, gpost_ref, w_ref, o_ref = refs[n_in:]
    y = None
    row = 0
    for r in ins:
        width = r.shape[1]
        part = jnp.dot(r[...], w_ref[row:row + width, :], preferred_element_type=F32)
        y = part if y is None else y + part
        row += width
    o_ref[...] = _adaln_out(s_ref[...], y, mod_ref, gpost_ref, 1, 1.0)


def _out_proj(parts, s, mod, gpost, w_out, *, layer, n_rows, geom):
    D = s.shape[1]
    tm = PROJ_TM
    return pl.pallas_call(
        partial(_out_proj_kernel, n_in=len(parts)),
        out_shape=jax.ShapeDtypeStruct((n_rows, D), F32),
        grid=(n_rows // tm,),
        in_specs=[pl.BlockSpec((tm, p.shape[1]), lambda t: (t, 0)) for p in parts] + [
            pl.BlockSpec((tm, D), lambda t: (t, 0)),
            pl.BlockSpec((1, 1, N_MOD, D), _mod_spec(layer, tm, *geom)),
            _const_spec((1, D)),
            _const_spec(w_out.shape),
        ],
        out_specs=pl.BlockSpec((tm, D), lambda t: (t, 0)),
        compiler_params=_cparams("arbitrary"),
        name="out_proj",
    )(*parts, s, mod, gpost, w_out)


def _hy_in_kernel(x_ref, xp_ref, xn_ref, mod_ref, gpre_ref, w_ref, bin_ref, wsh_ref, bsh_ref,
                  x0_ref, u_ref, *, tm, n_lat, tiles_per_seq, tiles_per_ctx):
    D = x_ref.shape[1]
    H = SUBLANES
    t = pl.program_id(0)
    lat_tiles = n_lat // tm
    pos = jnp.where(t < lat_tiles, t % tiles_per_seq, (t - lat_tiles) % tiles_per_ctx)
    n_pos = jnp.where(t < lat_tiles, tiles_per_seq, tiles_per_ctx)
    xs = jnp.concatenate([xp_ref[...], x_ref[...], xn_ref[...]], axis=0)
    hb = _adaln_in(xs, mod_ref, gpre_ref, 1).astype(BF16)
    keep_prev = jnp.where(pos == 0, 0.0, 1.0)
    keep_next = jnp.where(pos == n_pos - 1, 0.0, 1.0)
    n_ext = tm + 2 * H

    def conv(c):
        cols = slice(c * D, (c + 1) * D)
        z = jnp.dot(hb, w_ref[:, cols], preferred_element_type=F32) + bin_ref[:, cols]
        z = jnp.concatenate([z[0:H] * keep_prev, z[H:H + tm], z[H + tm:] * keep_next], axis=0)
        zm = pltpu.roll(z, 1, 0)[H:H + tm]
        zp = pltpu.roll(z, n_ext - 1, 0)[H:H + tm]
        return (zm * wsh_ref[0:1, cols] + z[H:H + tm] * wsh_ref[1:2, cols] + zp * wsh_ref[2:3, cols]
                + bsh_ref[:, cols])

    x0_ref[...] = conv(0).astype(BF16)
    u_ref[...] = (conv(2) * conv(1)).astype(BF16)


def _hy_in(s, mod, gpre, w_in, b_in, w_sh, b_sh, *, layer, n_rows, geom, ctx_len):
    n_lat, seq_len, _ = geom
    D = s.shape[1]
    tm = HY_TM
    H = SUBLANES
    blocks_per_tile = tm // H
    last_block = s.shape[0] // H - 1
    kern = partial(_hy_in_kernel, tm=tm, n_lat=n_lat, tiles_per_seq=seq_len // tm,
                   tiles_per_ctx=max(ctx_len // tm, 1))
    out = jax.ShapeDtypeStruct((n_rows, D), BF16)
    return pl.pallas_call(
        kern,
        out_shape=[out, out],
        grid=(n_rows // tm,),
        in_specs=[
            pl.BlockSpec((tm, D), lambda t: (t, 0)),
            pl.BlockSpec((H, D), lambda t: (jnp.maximum(t * blocks_per_tile - 1, 0), 0)),
            pl.BlockSpec((H, D), lambda t: (jnp.minimum((t + 1) * blocks_per_tile, last_block), 0)),
            pl.BlockSpec((1, 1, N_MOD, D), _mod_spec(layer, tm, *geom)),
            _const_spec((1, D)),
            _const_spec((D, 3 * D)),
            _const_spec((1, 3 * D)),
            _const_spec((3, 3 * D)),
            _const_spec((1, 3 * D)),
        ],
        out_specs=[pl.BlockSpec((tm, D), lambda t: (t, 0))] * 2,
        compiler_params=_cparams("arbitrary"),
        name="hyena_in",
    )(s, s, s, mod, gpre, w_in, b_in, w_sh, b_sh)


def _hy_filter_kernel(z_ref, f0_ref, fb0_ref, f1_ref, fb1_ref, f2_ref, fb2_ref, f3_ref, fr_ref, dl_ref,
                      kfb_ref):
    D = dl_ref.shape[1]
    hp = lax.Precision.HIGHEST
    z = z_ref[...]
    fr = fr_ref[...]
    a = jnp.sin(fr * (jnp.dot(z, f0_ref[...], precision=hp, preferred_element_type=F32) + fb0_ref[...]))
    a = jnp.sin(fr * (jnp.dot(a, f1_ref[...], precision=hp, preferred_element_type=F32) + fb1_ref[...]))
    a = jnp.sin(fr * (jnp.dot(a, f2_ref[...], precision=hp, preferred_element_type=F32) + fb2_ref[...]))
    k = jnp.dot(a, f3_ref[...], precision=hp, preferred_element_type=F32)
    decay = jnp.exp(-z[:, 0:1] * dl_ref[...])
    k_f = k[:, :D] * decay
    k_b = k[:, D:] * decay
    row = lax.broadcasted_iota(jnp.int32, (z.shape[0], 1), 0) + pl.program_id(0) * z.shape[0]
    k_b = jnp.where(row == 0, 0.0, k_b)
    kfb_ref[0] = k_f.astype(BF16)
    kfb_ref[1] = k_b.astype(BF16)


def _hy_filter(L, f0, fb0, f1, fb1, f2, fb2, f3, freq, D):
    t = jnp.linspace(0.0, 1.0, L, dtype=F32)[:, None]
    bands = (HY_EMB - 1) // 2
    w = 2.0 * math.pi * jnp.arange(L, dtype=F32)[:, None] / L
    f = jnp.linspace(1e-4, bands - 1, bands, dtype=F32)[None]
    z = jnp.concatenate([t, jnp.cos(f * w), -jnp.sin(f * w),
                         jnp.zeros((L, HY_EMB_PAD - HY_EMB), F32)], -1)
    f0p = jnp.concatenate([f0, jnp.zeros((HY_EMB_PAD - HY_EMB, f0.shape[1]), F32)], 0)
    deltas = jnp.abs(jnp.linspace(HY_MIN_DECAY, HY_MAX_DECAY, D, dtype=F32))[None]
    tl = min(512, L)
    O = f0.shape[1]
    return pl.pallas_call(
        _hy_filter_kernel,
        out_shape=jax.ShapeDtypeStruct((2, L, D), BF16),
        grid=(L // tl,),
        in_specs=[pl.BlockSpec((tl, HY_EMB_PAD), lambda i: (i, 0)),
                  _const_spec((HY_EMB_PAD, O)), _const_spec((1, O)),
                  _const_spec((O, O)), _const_spec((1, O)),
                  _const_spec((O, O)), _const_spec((1, O)),
                  _const_spec((O, 2 * D)), _const_spec((1, O)), _const_spec((1, D))],
        out_specs=pl.BlockSpec((2, tl, D), lambda i: (0, i, 0)),
        compiler_params=_cparams("arbitrary"),
        name="hyena_filter",
    )(z, f0p, fb0[None], f1, fb1[None], f2, fb2[None], f3, freq[None], deltas)


def _dft_matrices(L, tf):
    n_hi = L // 64 if L >= 64 else 1
    n_lo = L // n_hi
    k2 = 2 * jnp.arange(L, dtype=jnp.int32)[:, None] + 1
    step = 2.0 * math.pi / (4 * L)

    def angles(n):
        return ((k2 * n[None, :]) % (4 * L)).astype(F32) * step

    a_hi = angles(jnp.arange(n_hi, dtype=jnp.int32) * n_lo)
    a_lo = angles(jnp.arange(n_lo, dtype=jnp.int32))
    c1, s1 = jnp.cos(a_hi)[:, :, None], jnp.sin(a_hi)[:, :, None]
    c0, s0 = jnp.cos(a_lo)[:, None, :], jnp.sin(a_lo)[:, None, :]
    cos = (c1 * c0 - s1 * s0).reshape(L, L)
    nsin = (-(s1 * c0 + c1 * s0)).reshape(L, L)
    fwd = jnp.concatenate([cos.reshape(L // tf, tf, L), nsin.reshape(L // tf, tf, L)], axis=1).reshape(2 * L, L)
    inv = fwd.T * (1.0 / L)
    return fwd.astype(BF16), inv.astype(BF16)


def _spectrum_kernel(w_ref, kfb_ref, o_ref):
    tf = w_ref.shape[0] // 2
    k_f = kfb_ref[0].astype(F32)
    k_b = kfb_ref[1].astype(F32)
    o_ref[0:tf, :] = jnp.dot(w_ref[0:tf, :], (k_f + k_b).astype(BF16), preferred_element_type=F32)
    o_ref[tf:, :] = jnp.dot(w_ref[tf:, :], (k_f - k_b).astype(BF16), preferred_element_type=F32)


def _filter_spectrum(fwd, kfb, tf):
    _, L, D = kfb.shape
    return pl.pallas_call(
        _spectrum_kernel,
        out_shape=jax.ShapeDtypeStruct((2 * L, D), F32),
        grid=(L // tf,),
        in_specs=[pl.BlockSpec((2 * tf, L), lambda j: (j, 0)), _const_spec((2, L, D))],
        out_specs=pl.BlockSpec((2 * tf, D), lambda j: (j, 0)),
        compiler_params=_cparams("arbitrary"),
        name="filter_spectrum",
    )(fwd, kfb)


def _dft_fwd_kernel(w_ref, u_ref, kh_ref, o_ref):
    tf = w_ref.shape[0] // 2
    uh = jnp.dot(w_ref[...], u_ref[...], preferred_element_type=F32)
    ur, ui = uh[:tf], uh[tf:]
    kr, ki = kh_ref[0:tf, :], kh_ref[tf:, :]
    o_ref[0:tf, :] = (ur * kr - ui * ki).astype(BF16)
    o_ref[tf:, :] = (ur * ki + ui * kr).astype(BF16)


def _dft_forward(fwd, u, khat, *, n_batch, L, row_off, tf):
    D = u.shape[1]
    seq0 = row_off // L
    return pl.pallas_call(
        _dft_fwd_kernel,
        out_shape=jax.ShapeDtypeStruct((n_batch * 2 * L, D), BF16),
        grid=(n_batch, L // tf),
        in_specs=[pl.BlockSpec((2 * tf, L), lambda b, j: (j, 0)),
                  pl.BlockSpec((L, D), lambda b, j: (seq0 + b, 0)),
                  pl.BlockSpec((2 * tf, D), lambda b, j: (j, 0))],
        out_specs=pl.BlockSpec((2 * tf, D), lambda b, j: (b * (L // tf) + j, 0)),
        compiler_params=_cparams("arbitrary", "arbitrary"),
        name="dft_forward",
    )(fwd, u, khat)


def _dft_inv_kernel(w_ref, y_ref, u_ref, x0_ref, bias_ref, *rest):
    o_ref, acc = rest[-2], rest[-1]
    kk = pl.program_id(2)

    @pl.when(kk == 0)
    def _():
        acc[...] = jnp.zeros_like(acc)

    acc[...] += jnp.dot(w_ref[...], y_ref[...], preferred_element_type=F32)

    @pl.when(kk == pl.num_programs(2) - 1)
    def _():
        y = acc[...] + u_ref[...].astype(F32) * bias_ref[...]
        o_ref[...] = (x0_ref[...].astype(F32) * y).astype(BF16)


def _dft_inverse(inv, yhat, u, x0, bias, prev, *, n_batch, L, row_off, n_rows_out):
    D = u.shape[1]
    tm = min(1024, L)
    tk = min(2048, 2 * L)
    m_tiles = L // tm
    k_tiles = 2 * L // tk
    blk0 = row_off // tm

    def rows(b, i, kk):
        return (blk0 + b * m_tiles + i, 0)

    in_specs = [pl.BlockSpec((tm, tk), lambda b, i, kk: (i, kk)),
                pl.BlockSpec((tk, D), lambda b, i, kk: (b * k_tiles + kk, 0)),
                pl.BlockSpec((tm, D), rows), pl.BlockSpec((tm, D), rows), _const_spec((1, D))]
    args = [inv, yhat, u, x0, bias]
    aliases = {}
    if prev is not None:
        in_specs.append(pl.BlockSpec(memory_space=pl.ANY))
        args.append(prev)
        aliases = {5: 0}
    return pl.pallas_call(
        _dft_inv_kernel,
        out_shape=jax.ShapeDtypeStruct((n_rows_out, D), BF16),
        grid=(n_batch, m_tiles, k_tiles),
        in_specs=in_specs,
        out_specs=pl.BlockSpec((tm, D), rows),
        scratch_shapes=[pltpu.VMEM((tm, D), F32)],
        input_output_aliases=aliases,
        compiler_params=_cparams("arbitrary", "arbitrary", "arbitrary"),
        name="dft_inverse",
    )(*args)


FFT_N2 = 256
FFT_J = 16
FFT_JB = 64


def _fft_matrices(L):
    N = 2 * L
    N1 = N // FFT_N2
    H1, K1, J, A = N1 // 2, N1 // 2 + 1, FFT_J, FFT_N2 // FFT_J
    a = jnp.arange(A, dtype=jnp.int32)[:, None, None, None]
    k1 = jnp.arange(K1, dtype=jnp.int32)[None, :, None, None]
    j = jnp.arange(J, dtype=jnp.int32)[None, None, :, None]
    n1 = jnp.arange(H1, dtype=jnp.int32)[None, None, None, :]
    ang = ((k1 * (FFT_N2 * n1 + J * a + j)) % N).astype(F32) * (2.0 * math.pi / N)
    base = jnp.stack([jnp.cos(ang), -jnp.sin(ang)], axis=2)
    eye = jnp.eye(J, dtype=F32)
    ma = (base[..., None] * eye[None, None, None, :, None, :]).reshape(A, K1 * 2 * J, H1 * J)
    kk = jnp.arange(K1)
    w = jnp.where((kk == 0) | (kk == N1 // 2), 1.0, 2.0) / N
    basew = jnp.transpose(base * w[None, :, None, None, None], (0, 4, 3, 1, 2))
    mi = (basew[..., None] * eye[None, None, :, None, None, :]).reshape(A, H1 * J, K1 * 2 * J)
    mi = jnp.pad(mi, ((0, 0), (0, 0), (0, (-mi.shape[2]) % LANES)))
    n2 = jnp.arange(FFT_N2, dtype=jnp.int32)
    angf = ((n2[:, None] * n2[None, :]) % FFT_N2).astype(F32) * (2.0 * math.pi / FFT_N2)
    fr, fi = jnp.cos(angf), -jnp.sin(angf)
    cat = lambda top, bot: jnp.concatenate([top, bot], axis=0).astype(BF16)
    return dict(ma=ma.astype(BF16), mi=mi.astype(BF16), k1=K1, h1=H1,
                mre=cat(fr, fi), mim=cat(-fi, fr), gre=cat(fr, -fi), gim=cat(fi, fr))


def _fft_a_kernel(u_ref, ma_ref, z_ref):
    H1, JB, D = u_ref.shape
    K1 = z_ref.shape[1]
    J = FFT_J
    a = pl.program_id(1)
    for q in range(JB // J):
        rows = slice(q * J, (q + 1) * J)
        data = u_ref[:, rows, :].reshape(H1 * J, D)
        res = jnp.dot(ma_ref[a * (JB // J) + q], data, preferred_element_type=F32)
        z_ref[0, :, :, rows, :] = res.astype(BF16).reshape(K1, 2, J, D)


def _fft_a(u3, mats, *, n_batch):
    D = u3.shape[2]
    K1, H1 = mats["k1"], mats["h1"]
    return pl.pallas_call(
        _fft_a_kernel,
        out_shape=jax.ShapeDtypeStruct((n_batch, K1, 2, FFT_N2, D), BF16),
        grid=(n_batch, FFT_N2 // FFT_JB),
        in_specs=[pl.BlockSpec((H1, FFT_JB, D), lambda b, a: (b, a, 0)), _const_spec(mats["ma"].shape)],
        out_specs=pl.BlockSpec((1, K1, 2, FFT_JB, D), lambda b, a: (b, 0, 0, a, 0)),
        compiler_params=_cparams("arbitrary", "arbitrary"),
        name="fft_stage_a",
    )(u3, mats["ma"])


def _dft256(mre_ref, mim_ref, z_ref, idx):
    n2 = FFT_N2
    x = (jnp.dot(mre_ref[...], z_ref[idx + (0,)], preferred_element_type=F32)
         + jnp.dot(mim_ref[...], z_ref[idx + (1,)], preferred_element_type=F32))
    return x[:n2], x[n2:]


def _fft_spec_kernel(zf_ref, zb_ref, mre_ref, mim_ref, o_ref):
    fr, fi = _dft256(mre_ref, mim_ref, zf_ref, (0, 0))
    br, bi = _dft256(mre_ref, mim_ref, zb_ref, (0, 0))
    o_ref[0, 0] = fr + br
    o_ref[0, 1] = fi - bi


def _fft_spectrum(zfilt, mats):
    _, K1, _, n2, D = zfilt.shape
    blk = (1, 1, 2, n2, D)
    return pl.pallas_call(
        _fft_spec_kernel,
        out_shape=jax.ShapeDtypeStruct((K1, 2, n2, D), F32),
        grid=(K1,),
        in_specs=[pl.BlockSpec(blk, lambda k: (0, k, 0, 0, 0)), pl.BlockSpec(blk, lambda k: (1, k, 0, 0, 0)),
                  _const_spec(mats["mre"].shape), _const_spec(mats["mim"].shape)],
        out_specs=pl.BlockSpec((1, 2, n2, D), lambda k: (k, 0, 0, 0)),
        compiler_params=_cparams("arbitrary"),
        name="fft_filter_spectrum",
    )(zfilt, zfilt, mats["mre"], mats["mim"])


def _fft_mid_kernel(z_ref, kh_ref, mre_ref, mim_ref, gre_ref, gim_ref, o_ref):
    n2 = FFT_N2
    xr, xi = _dft256(mre_ref, mim_ref, z_ref, (0, 0))
    kr, ki = kh_ref[0, 0], kh_ref[0, 1]
    yr = (xr * kr - xi * ki).astype(BF16)
    yi = (xr * ki + xi * kr).astype(BF16)
    zp = (jnp.dot(gre_ref[...], yr, preferred_element_type=F32)
          + jnp.dot(gim_ref[...], yi, preferred_element_type=F32))
    o_ref[0, 0, 0] = zp[:n2].astype(BF16)
    o_ref[0, 0, 1] = zp[n2:].astype(BF16)


def _fft_mid(z, khat, mats):
    n_batch, K1, _, n2, D = z.shape
    blk = pl.BlockSpec((1, 1, 2, n2, D), lambda k, b: (b, k, 0, 0, 0))
    mat = [_const_spec(mats[m].shape) for m in ("mre", "mim", "gre", "gim")]
    return pl.pallas_call(
        _fft_mid_kernel,
        out_shape=jax.ShapeDtypeStruct(z.shape, BF16),
        grid=(K1, n_batch),
        in_specs=[blk, pl.BlockSpec((1, 2, n2, D), lambda k, b: (k, 0, 0, 0))] + mat,
        out_specs=blk,
        compiler_params=_cparams("arbitrary", "arbitrary"),
        name="fft_stage_b",
    )(z, khat, mats["mre"], mats["mim"], mats["gre"], mats["gim"])


def _fft_a_inv_kernel(zp_ref, mi_ref, u_ref, x0_ref, bias_ref, o_ref):
    H1, JB, D = u_ref.shape
    K1 = zp_ref.shape[1]
    J = FFT_J
    a = pl.program_id(1)
    k_pad = mi_ref.shape[2] - K1 * 2 * J
    for q in range(JB // J):
        rows = slice(q * J, (q + 1) * J)
        zz = zp_ref[0, :, :, rows, :].reshape(K1 * 2 * J, D)
        if k_pad:
            zz = jnp.concatenate([zz, jnp.zeros((k_pad, D), BF16)], axis=0)
        y = jnp.dot(mi_ref[a * (JB // J) + q], zz, preferred_element_type=F32).reshape(H1, J, D)
        y = y + u_ref[:, rows, :].astype(F32) * bias_ref[...]
        o_ref[:, rows, :] = (x0_ref[:, rows, :].astype(F32) * y).astype(BF16)


def _fft_a_inv(zp, mats, u3, x03, bias, *, n_blocks_out):
    n_batch, K1, _, n2, D = zp.shape
    H1 = mats["h1"]
    rows = pl.BlockSpec((H1, FFT_JB, D), lambda b, a: (b, a, 0))
    return pl.pallas_call(
        _fft_a_inv_kernel,
        out_shape=jax.ShapeDtypeStruct((n_blocks_out, n2, D), BF16),
        grid=(n_batch, n2 // FFT_JB),
        in_specs=[pl.BlockSpec((1, K1, 2, FFT_JB, D), lambda b, a: (b, 0, 0, a, 0)),
                  _const_spec(mats["mi"].shape), rows, rows, _const_spec((1, D))],
        out_specs=rows,
        compiler_params=_cparams("arbitrary", "arbitrary"),
        name="fft_stage_a_inv",
    )(zp, mats["mi"], u3, x03, bias)


def kernel(x, c, ctx, c_ctx, w_mod, b_mod, norm_pre, norm_post, ffn_gate, ffn_up, ffn_down,
           mix_w_in, attn_sink, ret_decay, mix_w_out, hy_w_in, hy_b_in, hy_short_w, hy_short_b,
           hy_f0, hy_fb0, hy_f1, hy_fb1, hy_f2, hy_fb2, hy_f3, hy_freq, hy_bias, hy_w_out):
    Bn, L, D = x.shape
    C = ctx.shape[1]
    n_lat = Bn * L
    n_all = n_lat + Bn * C
    geom = (n_lat, L, Bn)
    assert L % PROJ_TM == 0 and L % HY_TM == 0 and n_lat % C == 0 and (Bn * C) % PROJ_TM == 0
    assert C % (RET_CH * R_CHUNK) == 0 and L % (RET_CH * R_CHUNK) == 0 and C % HY_TM == 0
    assert L % FFT_N2 == 0 and n_all % FFT_N2 == 0
    last_reader = DEPTH - 1 if (DEPTH - 1) % 2 == 0 else DEPTH - 2

    c_all = jnp.concatenate([c, c_ctx[None], jnp.zeros((SUBLANES - (Bn + 1) % SUBLANES, D), F32)], axis=0)
    mod = _modulation(c_all, w_mod, b_mod).reshape(DEPTH, c_all.shape[0], N_MOD, D)

    rope_a = _rope_tables(_axial_angles(L), PROJ_TM)
    rope_r = _rope_tables(_line_angles(L), PROJ_TM)
    fft = _fft_matrices(L)

    def ffn(s, l, i, j, n_rows):
        return _ffn(s, mod, norm_pre[l, i][None], norm_post[l, i][None], ffn_gate[l, j].astype(BF16),
                    ffn_up[l, j].astype(BF16), ffn_down[l, j].astype(BF16),
                    layer=l, i=i, n_rows=n_rows, geom=geom)

    s = jnp.concatenate([x.reshape(n_lat, D), ctx.reshape(Bn * C, D)], axis=0)
    for l in range(DEPTH):
        ctx_live = l <= last_reader
        ctx_full = l < last_reader
        n_in = n_all if ctx_live else n_lat
        n_out = n_all if ctx_full else n_lat
        s = ffn(s, l, 0, 0, n_in)
        gpre, gpost = norm_pre[l, 1][None], norm_post[l, 1][None]
        if l % 2 == 0:
            e = l // 2
            aq, rq, rg, akd, avd, rk, rv = _even_in(s, mod, gpre, _pack_even_w_in(mix_w_in[e]), rope_a, rope_r,
                                                    layer=l, n_rows=n_in, geom=geom)
            sink_lanes = jnp.broadcast_to(attn_sink[e][:, None], (A_Q_HEADS, LANES))
            dec_lanes = jnp.repeat(ret_decay[e], R_DIM, axis=1)
            a = _attention(aq, akd, avd, sink_lanes, n_batch=Bn, seq_len=L, ctx_len=C,
                           with_ctx_queries=ctx_full)
            tab_tt, tab_rows = _retention_tables(dec_lanes)
            sf, sb = _retention_states(rk, rv, tab_rows, n_batch=Bn, seq_len=L, ctx_len=C)
            r = _retention_out(rq, rk, rv, rg, sf, sb, tab_tt, tab_rows, n_rows=n_out)
            s = _out_proj([a, r], s, mod, gpost, mix_w_out[e].astype(BF16), layer=l, n_rows=n_out, geom=geom)
        else:
            o = l // 2
            x0, u = _hy_in(s, mod, gpre, hy_w_in[o].astype(BF16), hy_b_in[o][None], hy_short_w[o],
                           hy_short_b[o][None], layer=l, n_rows=n_in, geom=geom, ctx_len=C)
            filt = (hy_f0[o], hy_fb0[o], hy_f1[o], hy_fb1[o], hy_f2[o], hy_fb2[o], hy_f3[o], hy_freq[o])
            bias = hy_bias[o][None]
            blocks = lambda arr: arr.reshape(arr.shape[0] // FFT_N2, FFT_N2, D)
            kfb = _hy_filter(L, *filt, D)
            khat = _fft_spectrum(_fft_a(blocks(kfb.reshape(2 * L, D)), fft, n_batch=2), fft)
            zp = _fft_mid(_fft_a(blocks(u), fft, n_batch=Bn), khat, fft)
            yg = _fft_a_inv(zp, fft, blocks(u), blocks(x0), bias, n_blocks_out=n_out // FFT_N2).reshape(n_out, D)
            if ctx_full:
                tf = min(DFT_TF, C)
                fwd, inv = _dft_matrices(C, tf)
                khat_c = _filter_spectrum(fwd, _hy_filter(C, *filt, D), tf)
                yhat = _dft_forward(fwd, u, khat_c, n_batch=Bn, L=C, row_off=n_lat, tf=tf)
                yg = _dft_inverse(inv, yhat, u, x0, bias, yg, n_batch=Bn, L=C, row_off=n_lat, n_rows_out=n_out)
            s = _out_proj([yg], s, mod, gpost, hy_w_out[o].astype(BF16), layer=l, n_rows=n_out, geom=geom)
        s = ffn(s, l, 2, 1, n_out)
    return s[:n_lat].reshape(Bn, L, D)
```

```python
import math
from functools import partial

import jax
import jax.numpy as jnp
from jax import lax
from jax.experimental import pallas as pl
from jax.experimental.pallas import tpu as pltpu

F32 = jnp.float32
BF16 = jnp.bfloat16

DEPTH = 4
GRID_W = 64
EPS = 1e-6
NEG = -1e30
N_MOD = 9
FFN_RESIDUAL = 0.5
HEAD_DIM = 64
A_Q_HEADS = 8
A_KV_HEADS = 2
WINDOW = 128
BLOCK = 128
ROPE_BASE = 10000.0
R_DIM = 64
R_HEADS = 8
R_CHUNK = 128
HY_EMB = 33
HY_EMB_PAD = 64
HY_MAX_DECAY = math.log(1e-2) / 0.3
HY_MIN_DECAY = math.log(1e-2) / 1.5

LANES = 128
SUBLANES = 8
V7X_VMEM_LIMIT_BYTES = 56 * 1024 * 1024

FFN_TM = 1024
FFN_SUB = 2
FFN_TF = 256
PROJ_TM = 512
HY_TM = 256
MOD_TN = 2304
DFT_TF = 256
RET_CH = 2
ATT_RC = 64


def _cparams(*sem):
    return pltpu.CompilerParams(dimension_semantics=sem, vmem_limit_bytes=V7X_VMEM_LIMIT_BYTES)


def _const_spec(shape):
    zeros = (0,) * len(shape)
    return pl.BlockSpec(shape, lambda *_: zeros, pipeline_mode=pl.Buffered(1))


def _mod_spec(layer, tm, n_lat, seq_len, n_batch):
    def index(t, *_):
        return (layer, jnp.where(t * tm < n_lat, (t * tm) // seq_len, n_batch), 0, 0)
    return index


def _adaln_in(x, mod_ref, gpre_ref, i):
    shift = mod_ref[0, 0, 3 * i:3 * i + 1, :]
    scale = mod_ref[0, 0, 3 * i + 1:3 * i + 2, :]
    gain = gpre_ref[0] * (1.0 + scale)
    inv = lax.rsqrt(jnp.mean(x * x, axis=-1, keepdims=True) + EPS)
    return (x * inv) * gain + shift


def _adaln_out(x, y, mod_ref, gpost_ref, i, w):
    gate = mod_ref[0, 0, 3 * i + 2:3 * i + 3, :]
    gain = (w * gate) * gpost_ref[0]
    inv = lax.rsqrt(jnp.mean(y * y, axis=-1, keepdims=True) + EPS)
    return x + (y * inv) * gain


def _mod_kernel(c_ref, w_ref, b_ref, o_ref):
    c = c_ref[...]
    a = (c * jax.nn.sigmoid(c)).astype(BF16)
    o_ref[0] = jnp.dot(a, w_ref[0].astype(BF16), preferred_element_type=F32) + b_ref[0]


def _modulation(c_all, w_mod, b_mod):
    depth, D, W = w_mod.shape
    rows = c_all.shape[0]
    return pl.pallas_call(
        _mod_kernel,
        out_shape=jax.ShapeDtypeStruct((depth, rows, W), F32),
        grid=(depth, W // MOD_TN),
        in_specs=[
            pl.BlockSpec((rows, D), lambda l, j: (0, 0)),
            pl.BlockSpec((1, D, MOD_TN), lambda l, j: (l, 0, j)),
            pl.BlockSpec((1, 1, MOD_TN), lambda l, j: (l, 0, j)),
        ],
        out_specs=pl.BlockSpec((1, rows, MOD_TN), lambda l, j: (l, 0, j)),
        compiler_params=_cparams("arbitrary", "arbitrary"),
        name="modulation",
    )(c_all, w_mod, b_mod.reshape(depth, 1, W))


def _ffn_kernel(*refs, i, tf, n_chunks, n_parts, n_sub):
    parts = refs[:n_parts]
    x_ref, mod_ref, npre_ref, npost_ref = refs[n_parts:n_parts + 4]
    rest = refs[n_parts + 4:]
    if n_parts:
        nmix_ref, wo_ref = rest[:2]
        rest = rest[2:]
    wg_ref, wu_ref, wd_ref, o_ref = rest
    sub = x_ref.shape[0] // n_sub
    for b in range(n_sub):
        rows = slice(b * sub, (b + 1) * sub)
        x = x_ref[rows, :]
        if n_parts:
            y = None
            row = 0
            for r in parts:
                width = r.shape[1]
                part = jnp.dot(r[rows, :], wo_ref[0, row:row + width, :], preferred_element_type=F32)
                y = part if y is None else y + part
                row += width
            x = _adaln_out(x, y, mod_ref, nmix_ref, 1, 1.0)
        hb = _adaln_in(x, mod_ref, npre_ref, i).astype(BF16)
        acc = jnp.zeros(x.shape, F32)
        for j in range(n_chunks):
            cols = slice(j * tf, (j + 1) * tf)
            g = jnp.dot(hb, wg_ref[0, 0, :, cols], preferred_element_type=F32)
            u = jnp.dot(hb, wu_ref[0, 0, :, cols], preferred_element_type=F32)
            a = (g * jax.nn.sigmoid(g) * u).astype(BF16)
            acc = acc + jnp.dot(a, wd_ref[0, 0, cols, :], preferred_element_type=F32)
        o_ref[rows, :] = _adaln_out(x, acc, mod_ref, npost_ref, i, FFN_RESIDUAL)


def _ffn(s, mod, npre, npost, wg, wu, wd, *, layer, i, j, n_rows, geom, mix=None):
    D = s.shape[1]
    d_ff = wd.shape[2]
    tm = min(FFN_TM, geom[1])
    rows = lambda t: (t, 0)
    norm = lambda sub: pl.BlockSpec((1, 1, D), lambda t: (3 * layer + sub, 0, 0), pipeline_mode=pl.Buffered(1))
    weight = lambda shape: pl.BlockSpec((1, 1) + shape, lambda t: (layer, j, 0, 0), pipeline_mode=pl.Buffered(1))
    parts, extra_specs, extra_args = [], [], []
    if mix is not None:
        parts, w_out, e = mix
        extra_specs = [norm(1), pl.BlockSpec((1,) + w_out.shape[1:], lambda t: (e, 0, 0),
                                             pipeline_mode=pl.Buffered(1))]
        extra_args = [npost, w_out]
    return pl.pallas_call(
        partial(_ffn_kernel, i=i, tf=FFN_TF, n_chunks=d_ff // FFN_TF, n_parts=len(parts), n_sub=FFN_SUB),
        out_shape=jax.ShapeDtypeStruct((n_rows, D), F32),
        grid=(n_rows // tm,),
        in_specs=[pl.BlockSpec((tm, p.shape[1]), rows) for p in parts] + [
            pl.BlockSpec((tm, D), rows),
            pl.BlockSpec((1, 1, N_MOD, D), _mod_spec(layer, tm, *geom)),
            norm(i), norm(i),
        ] + extra_specs + [weight((D, d_ff)), weight((D, d_ff)), weight((d_ff, D))],
        out_specs=pl.BlockSpec((tm, D), rows),
        compiler_params=_cparams("arbitrary"),
        name="ffn",
    )(*parts, s, mod, npre, npost, *extra_args, wg, wu, wd)


_AQ = (0, 512)
_RQ = (512, 1024)
_RG = (1024, 1536)
_AK = (1536, 1792)
_AV = (1792, 2048)
_RK = (2048, 2560)
_RV = (2560, 3072)
EVEN_W = 3072


def _pack_even_w_in(w_in):
    aq, rq, rg = w_in[:, 0:512], w_in[:, 512:1024], w_in[:, 1024:1536]
    ak, av = w_in[:, 1536:1664], w_in[:, 1664:1792]
    rk, rv = w_in[:, 1792:2304], w_in[:, 2304:2816]

    def dup(a):
        g0, g1 = a[:, :HEAD_DIM], a[:, HEAD_DIM:]
        return jnp.concatenate([g0, g0, g1, g1], axis=1)

    return jnp.concatenate([aq, rq, rg, dup(ak), dup(av), rk, rv], axis=1).astype(BF16)


def _rope_tables(ang, ident_rows):
    L = ang.shape[0]
    cos, sin = jnp.cos(ang), jnp.sin(ang)
    zero = jnp.zeros_like(sin)
    cos_t = jnp.tile(cos, (1, 4))
    sin_a = jnp.tile(jnp.concatenate([-sin, zero], axis=1), (1, 2))
    sin_b = jnp.tile(jnp.concatenate([zero, sin], axis=1), (1, 2))
    pad_one = jnp.ones((ident_rows, LANES), F32)
    pad_zero = jnp.zeros((ident_rows, LANES), F32)
    return (jnp.concatenate([cos_t, pad_one], 0), jnp.concatenate([sin_a, pad_zero], 0),
            jnp.concatenate([sin_b, pad_zero], 0))


def _axial_angles(L):
    n_rows = L // GRID_W
    row = jnp.repeat(jnp.arange(n_rows, dtype=F32), GRID_W)
    col = jnp.tile(jnp.arange(GRID_W, dtype=F32), n_rows)
    nf = HEAD_DIM // 4
    inv = ROPE_BASE ** (-jnp.arange(nf, dtype=F32) / nf)
    return jnp.concatenate([row[:, None] * inv, col[:, None] * inv], -1)


def _line_angles(L):
    inv = ROPE_BASE ** (-jnp.linspace(0.0, 1.0, R_DIM // 2, dtype=F32))
    return jnp.arange(L, dtype=F32)[:, None] * inv


def _rope(z, cos, sin_a, sin_b):
    outs = []
    for c in range(z.shape[1] // LANES):
        zc = z[:, c * LANES:(c + 1) * LANES]
        outs.append(zc * cos + pltpu.roll(zc, 96, 1) * sin_a + pltpu.roll(zc, 32, 1) * sin_b)
    return outs[0] if len(outs) == 1 else jnp.concatenate(outs, axis=1)


def _even_in_kernel(x_ref, mod_ref, gpre_ref, w_ref, ca_ref, saa_ref, sab_ref, cr_ref, sra_ref, srb_ref,
                    aq_ref, rq_ref, rg_ref, ak_ref, av_ref, rk_ref, rv_ref):
    hb = _adaln_in(x_ref[...], mod_ref, gpre_ref, 1).astype(BF16)

    def proj(cols):
        return jnp.dot(hb, w_ref[:, cols[0]:cols[1]], preferred_element_type=F32)

    rope_a = (ca_ref[...], saa_ref[...], sab_ref[...])
    rope_r = (cr_ref[...], sra_ref[...], srb_ref[...])
    aq_ref[...] = (_rope(proj(_AQ), *rope_a) * HEAD_DIM ** -0.5).astype(BF16)
    rq_ref[...] = _rope(proj(_RQ), *rope_r).astype(BF16)
    rg_ref[...] = proj(_RG)
    ak_ref[...] = _rope(proj(_AK), *rope_a).astype(BF16)
    av_ref[...] = proj(_AV).astype(BF16)
    rk_ref[...] = (_rope(proj(_RK), *rope_r) * R_DIM ** -0.5).astype(BF16)
    rv_ref[...] = proj(_RV).astype(BF16)


def _even_in(s, mod, gpre, w_pack, rope_a, rope_r, *, layer, n_rows, geom):
    n_lat, seq_len, _ = geom
    D = s.shape[1]
    tm = PROJ_TM
    tiles_per_seq = seq_len // tm

    def rope_index(t):
        return (jnp.where(t * tm < n_lat, t % tiles_per_seq, tiles_per_seq), 0)

    rope_spec = pl.BlockSpec((tm, LANES), rope_index)

    def out(width, dtype):
        return jax.ShapeDtypeStruct((n_rows, width), dtype), pl.BlockSpec((tm, width), lambda t: (t, 0))

    outs = [out(512, BF16), out(512, BF16), out(512, F32), out(256, BF16), out(256, BF16),
            out(512, BF16), out(512, BF16)]
    return pl.pallas_call(
        _even_in_kernel,
        out_shape=[o[0] for o in outs],
        grid=(n_rows // tm,),
        in_specs=[
            pl.BlockSpec((tm, D), lambda t: (t, 0)),
            pl.BlockSpec((1, 1, N_MOD, D), _mod_spec(layer, tm, *geom)),
            pl.BlockSpec((1, 1, D), lambda t: (3 * layer + 1, 0, 0), pipeline_mode=pl.Buffered(1)),
            _const_spec((D, EVEN_W)),
        ] + [rope_spec] * 6,
        out_specs=[o[1] for o in outs],
        compiler_params=_cparams("arbitrary"),
        name="even_in",
    )(s, mod, gpre, w_pack, *rope_a, *rope_r)


def _attn_kernel(q_ref, kp_ref, kc_ref, kn_ref, kx_ref, vp_ref, vc_ref, vn_ref, vx_ref, sink_ref, band_ref,
                 o_ref, *, nb):
    T = BLOCK
    j = pl.program_id(1)
    k_lo = jnp.where(j > 0, 0, T)
    k_hi = jnp.where(j < nb - 1, 3 * T, jnp.where(j < nb, 2 * T, 0))
    kj = lax.broadcasted_iota(jnp.int32, (1, band_ref.shape[1]), 1)
    exists = ((kj >= k_lo) & (kj < k_hi)) | (kj >= 3 * T)
    bias = band_ref[...] + jnp.where(exists, 0.0, NEG)
    lo = lax.broadcasted_iota(jnp.int32, (1, LANES), 1) < HEAD_DIM
    q = q_ref[...].astype(F32)
    for g in range(A_KV_HEADS):
        gl = slice(g * LANES, (g + 1) * LANES)
        kd = jnp.concatenate([kp_ref[:, gl], kc_ref[:, gl], kn_ref[:, gl], kx_ref[:, gl]], axis=0)
        vd = jnp.concatenate([vp_ref[:, gl], vc_ref[:, gl], vn_ref[:, gl], vx_ref[:, gl]], axis=0)
        qs = []
        for c in range(2):
            qc = q[:, (2 * g + c) * LANES:(2 * g + c + 1) * LANES]
            qs.append(jnp.where(lo, qc, 0.0))
            qs.append(jnp.where(lo, 0.0, qc))
        q_stack = jnp.concatenate(qs, axis=0).astype(BF16)
        s_all = lax.dot_general(q_stack, kd, (((1,), (1,)), ((), ())), preferred_element_type=F32)
        ps, inv = [], []
        for rc in range(4 * T // ATT_RC):
            rows = slice(rc * ATT_RC, (rc + 1) * ATT_RC)
            h = 4 * g + (rc * ATT_RC) // T
            sink = sink_ref[h:h + 1, 0:1]
            s = s_all[rows] + bias[rows]
            m = jnp.maximum(jnp.max(s, axis=-1, keepdims=True), sink)
            p = jnp.exp(s - m)
            inv.append(1.0 / (jnp.sum(p, axis=-1, keepdims=True) + jnp.exp(sink - m)))
            ps.append(p.astype(BF16))
        p = jnp.concatenate(ps, axis=0)
        o = jnp.dot(p, vd, preferred_element_type=F32) * jnp.concatenate(inv, axis=0)
        for c in range(2):
            oc = jnp.where(lo, o[(2 * c) * T:(2 * c + 1) * T], o[(2 * c + 1) * T:(2 * c + 2) * T])
            o_ref[:, (2 * g + c) * LANES:(2 * g + c + 1) * LANES] = oc.astype(BF16)


def _attention(aq, akd, avd, sink_lanes, *, n_batch, seq_len, ctx_len, with_ctx_queries):
    T = BLOCK
    nb = seq_len // T
    ncb = ctx_len // T
    n_lat = n_batch * seq_len
    n_q_blocks = nb + (ncb if with_ctx_queries else 0)
    n_rows = n_lat + (n_batch * ctx_len if with_ctx_queries else 0)

    def q_index(b, j):
        return (jnp.where(j < nb, b * nb + j, n_lat // T + b * ncb + (j - nb)), 0)

    def kv_index(off):
        def index(b, j):
            return (b * nb + jnp.clip(j + off, 0, nb - 1), 0)
        return index

    def ctx_index(b, j):
        return (n_lat // ctx_len + b, 0)

    kv_specs = [pl.BlockSpec((T, 2 * LANES), kv_index(-1)), pl.BlockSpec((T, 2 * LANES), kv_index(0)),
                pl.BlockSpec((T, 2 * LANES), kv_index(1)), pl.BlockSpec((ctx_len, 2 * LANES), ctx_index)]
    qi = jnp.arange(4 * T, dtype=jnp.int32)[:, None] % T
    kj = jnp.arange(3 * T + ctx_len, dtype=jnp.int32)[None, :]
    band = jnp.where((jnp.abs(kj - T - qi) <= WINDOW) | (kj >= 3 * T), 0.0, NEG).astype(F32)
    return pl.pallas_call(
        partial(_attn_kernel, nb=nb),
        out_shape=jax.ShapeDtypeStruct((n_rows, A_Q_HEADS * HEAD_DIM), BF16),
        grid=(n_batch, n_q_blocks),
        in_specs=[pl.BlockSpec((T, A_Q_HEADS * HEAD_DIM), q_index)] + kv_specs + kv_specs
        + [_const_spec((A_Q_HEADS, LANES)), _const_spec(band.shape)],
        out_specs=pl.BlockSpec((T, A_Q_HEADS * HEAD_DIM), q_index),
        compiler_params=_cparams("arbitrary", "arbitrary"),
        name="attention",
    )(aq, akd, akd, akd, akd, avd, avd, avd, avd, sink_lanes, band)


def _log_sigmoid(x):
    return jnp.minimum(x, 0.0) - jnp.log(1.0 + jnp.exp(-jnp.abs(x)))


def _lo_head():
    return lax.broadcasted_iota(jnp.int32, (1, LANES), 1) < R_DIM


_ROW_DK_F, _ROW_DK_B, _ROW_DQ_F, _ROW_DQ_B, _ROW_DC = (i * R_CHUNK for i in range(5))
_RET_ROWS = 4 * R_CHUNK + SUBLANES


def _ret_tables_kernel(dec_ref, tt_ref, rows_ref):
    T = R_CHUNK
    lg_f = _log_sigmoid(dec_ref[0:1, :])
    lg_b = _log_sigmoid(dec_ref[1:2, :])
    t = lax.broadcasted_iota(jnp.int32, (T, 1), 0).astype(F32)
    rows_ref[_ROW_DK_F:_ROW_DK_F + T, :] = jnp.exp((T - 1.0 - t) * lg_f)
    rows_ref[_ROW_DK_B:_ROW_DK_B + T, :] = jnp.exp(t * lg_b)
    rows_ref[_ROW_DQ_F:_ROW_DQ_F + T, :] = jnp.exp((t + 1.0) * lg_f)
    rows_ref[_ROW_DQ_B:_ROW_DQ_B + T, :] = jnp.exp((T - t) * lg_b)
    rows_ref[_ROW_DC:_ROW_DC + SUBLANES, :] = jnp.concatenate(
        [jnp.exp(T * lg_f), jnp.exp(T * lg_b), jnp.zeros((SUBLANES - 2, lg_f.shape[1]), F32)], axis=0)
    d = t - lax.broadcasted_iota(jnp.int32, (1, T), 1).astype(F32)
    for h in range(R_HEADS):
        a = lg_f[0:1, h * R_DIM:h * R_DIM + 1]
        b = lg_b[0:1, h * R_DIM:h * R_DIM + 1]
        fwd = jnp.exp(jnp.maximum(d, 0.0) * a)
        bwd = jnp.exp(jnp.maximum(-d, 0.0) * b)
        tt_ref[h * T:(h + 1) * T, :] = jnp.where(d > 0, fwd, jnp.where(d < 0, bwd, 2.0))


def _retention_tables(dec_lanes):
    T = R_CHUNK
    W = dec_lanes.shape[1]
    return pl.pallas_call(
        _ret_tables_kernel,
        out_shape=[jax.ShapeDtypeStruct((R_HEADS * T, T), F32), jax.ShapeDtypeStruct((_RET_ROWS, W), F32)],
        grid=(1,),
        in_specs=[_const_spec((2, W))],
        out_specs=[pl.BlockSpec((R_HEADS * T, T), lambda i: (0, 0)), pl.BlockSpec((_RET_ROWS, W), lambda i: (0, 0))],
        compiler_params=_cparams("arbitrary"),
        name="retention_tables",
    )(dec_lanes)


def _ret_state_kernel(kf_ref, vf_ref, kb_ref, vb_ref, rows_ref, sf_ref, sb_ref, sf_acc, sb_acc):
    T = R_CHUNK

    @pl.when(pl.program_id(1) == 0)
    def _():
        sf_acc[...] = jnp.zeros_like(sf_acc)
        sb_acc[...] = jnp.zeros_like(sb_acc)

    lo = _lo_head()

    def chunk(acc, s_ref, k_ref, v_ref, ci, d_k, d_c):
        s_ref[ci] = acc[...].astype(BF16)
        r = slice(ci * T, (ci + 1) * T)
        kd = k_ref[r, :].astype(F32) * d_k
        for p in range(R_HEADS // 2):
            ls = slice(p * LANES, (p + 1) * LANES)
            kv = jnp.dot(kd[:, ls].T.astype(BF16), v_ref[r, ls], preferred_element_type=F32)
            rs = slice(p * R_DIM, (p + 1) * R_DIM)
            acc[rs, :] = acc[rs, :] * d_c[:, ls] + jnp.where(lo, kv[0:R_DIM], kv[R_DIM:])

    dkf = rows_ref[_ROW_DK_F:_ROW_DK_F + T, :]
    dkb = rows_ref[_ROW_DK_B:_ROW_DK_B + T, :]
    dcf = rows_ref[_ROW_DC:_ROW_DC + 1, :]
    dcb = rows_ref[_ROW_DC + 1:_ROW_DC + 2, :]
    for ci in range(RET_CH):
        chunk(sf_acc, sf_ref, kf_ref, vf_ref, ci, dkf, dcf)
    for ci in reversed(range(RET_CH)):
        chunk(sb_acc, sb_ref, kb_ref, vb_ref, ci, dkb, dcb)


def _ret_step_maps(n_batch, seq_len, ctx_len):
    rows = RET_CH * R_CHUNK
    ncs, nls = ctx_len // rows, seq_len // rows
    ctx_base = n_batch * nls

    def fwd(b, j):
        return jnp.where(j < ncs, ctx_base + b * ncs + j, b * nls + (j - ncs))

    def bwd(b, j):
        return jnp.where(j < ncs, ctx_base + b * ncs + (ncs - 1 - j), b * nls + (nls - 1 - (j - ncs)))

    return ncs + nls, fwd, bwd


def _retention_states(rk, rv, tab_rows, *, n_batch, seq_len, ctx_len):
    T = R_CHUNK
    W = R_HEADS * R_DIM
    n_steps, fwd, bwd = _ret_step_maps(n_batch, seq_len, ctx_len)
    n_chunks = n_batch * n_steps * RET_CH
    rows = pl.BlockSpec((RET_CH * T, W), lambda b, j: (fwd(b, j), 0))
    rows_b = pl.BlockSpec((RET_CH * T, W), lambda b, j: (bwd(b, j), 0))
    state = jax.ShapeDtypeStruct((n_chunks, W // 2, LANES), BF16)
    return pl.pallas_call(
        _ret_state_kernel,
        out_shape=[state, state],
        grid=(n_batch, n_steps),
        in_specs=[rows, rows, rows_b, rows_b, _const_spec(tab_rows.shape)],
        out_specs=[pl.BlockSpec((RET_CH, W // 2, LANES), lambda b, j: (fwd(b, j), 0, 0)),
                   pl.BlockSpec((RET_CH, W // 2, LANES), lambda b, j: (bwd(b, j), 0, 0))],
        scratch_shapes=[pltpu.VMEM((W // 2, LANES), F32), pltpu.VMEM((W // 2, LANES), F32)],
        compiler_params=_cparams("arbitrary", "arbitrary"),
        name="retention_states",
    )(rk, rv, rk, rv, tab_rows)


def _ret_out_kernel(q_ref, k_ref, v_ref, g_ref, sf_ref, sb_ref, tt_ref, rows_ref, o_ref):
    T = R_CHUNK
    lo = _lo_head()
    ri = lax.broadcasted_iota(jnp.int32, (LANES, 1), 0) >> 6
    ci = lax.broadcasted_iota(jnp.int32, (1, LANES), 1) >> 6
    avg = jnp.where(ri == ci, 1.0 / R_DIM, 0.0).astype(BF16)

    def state_block(s):
        s = s.astype(F32)
        return jnp.concatenate([jnp.where(lo, s, 0.0), jnp.where(lo, 0.0, s)], axis=0).astype(BF16)

    for c in range(RET_CH):
        r = slice(c * T, (c + 1) * T)
        for p in range(R_HEADS // 2):
            ls = slice(p * LANES, (p + 1) * LANES)
            q = q_ref[r, ls]
            qf = q.astype(F32)
            q_stack = jnp.concatenate([jnp.where(lo, qf, 0.0), jnp.where(lo, 0.0, qf)], axis=0).astype(BF16)
            sc = lax.dot_general(q_stack, k_ref[r, ls], (((1,), (1,)), ((), ())), preferred_element_type=F32)
            att = (sc * tt_ref[2 * p * T:(2 * p + 2) * T, :]).astype(BF16)
            oh = jnp.dot(att, v_ref[r, ls], preferred_element_type=F32)
            o = jnp.where(lo, oh[:T], oh[T:])
            rs = slice(p * R_DIM, (p + 1) * R_DIM)
            o = o + (jnp.dot(q, state_block(sf_ref[c, rs, :]), preferred_element_type=F32)
                     * rows_ref[_ROW_DQ_F:_ROW_DQ_F + T, ls])
            o = o + (jnp.dot(q, state_block(sb_ref[c, rs, :]), preferred_element_type=F32)
                     * rows_ref[_ROW_DQ_B:_ROW_DQ_B + T, ls])
            o2 = o * o
            hi = o2.astype(BF16)
            rest = (o2 - hi.astype(F32)).astype(BF16)
            ms = jnp.dot(hi, avg, preferred_element_type=F32) + jnp.dot(rest, avg, preferred_element_type=F32)
            g = g_ref[r, ls]
            o_ref[r, ls] = (o * lax.rsqrt(ms + EPS) * (g * jax.nn.sigmoid(g))).astype(BF16)


def _retention_out(rq, rk, rv, rg, sf, sb, tab_tt, tab_rows, *, n_rows):
    T = R_CHUNK
    W = R_HEADS * R_DIM
    rows = pl.BlockSpec((RET_CH * T, W), lambda c: (c, 0))
    st = pl.BlockSpec((RET_CH, W // 2, LANES), lambda c: (c, 0, 0))
    return pl.pallas_call(
        _ret_out_kernel,
        out_shape=jax.ShapeDtypeStruct((n_rows, W), BF16),
        grid=(n_rows // (RET_CH * T),),
        in_specs=[rows, rows, rows, rows, st, st, _const_spec(tab_tt.shape), _const_spec(tab_rows.shape)],
        out_specs=rows,
        compiler_params=_cparams("arbitrary"),
        name="retention_out",
    )(rq, rk, rv, rg, sf, sb, tab_tt, tab_rows)


def _hy_in_kernel(x_ref, xp_ref, xn_ref, mod_ref, gpre_ref, w_ref, bin_ref, wsh_ref, bsh_ref,
                  x0_ref, u_ref, *, tm, n_lat, tiles_per_seq, tiles_per_ctx):
    D = x_ref.shape[1]
    H = SUBLANES
    t = pl.program_id(0)
    lat_tiles = n_lat // tm
    pos = jnp.where(t < lat_tiles, t % tiles_per_seq, (t - lat_tiles) % tiles_per_ctx)
    n_pos = jnp.where(t < lat_tiles, tiles_per_seq, tiles_per_ctx)
    xs = jnp.concatenate([xp_ref[...], x_ref[...], xn_ref[...]], axis=0)
    hb = _adaln_in(xs, mod_ref, gpre_ref, 1).astype(BF16)
    keep_prev = jnp.where(pos == 0, 0.0, 1.0)
    keep_next = jnp.where(pos == n_pos - 1, 0.0, 1.0)
    n_ext = tm + 2 * H

    def conv(c):
        cols = slice(c * D, (c + 1) * D)
        z = jnp.dot(hb, w_ref[:, cols], preferred_element_type=F32) + bin_ref[:, cols]
        z = jnp.concatenate([z[0:H] * keep_prev, z[H:H + tm], z[H + tm:] * keep_next], axis=0)
        zm = pltpu.roll(z, 1, 0)[H:H + tm]
        zp = pltpu.roll(z, n_ext - 1, 0)[H:H + tm]
        return (zm * wsh_ref[0:1, cols] + z[H:H + tm] * wsh_ref[1:2, cols] + zp * wsh_ref[2:3, cols]
                + bsh_ref[:, cols])

    x0_ref[...] = conv(0).astype(BF16)
    u_ref[...] = (conv(2) * conv(1)).astype(BF16)


def _hy_in(s, mod, gpre, w_in, b_in, w_sh, b_sh, *, layer, n_rows, geom, ctx_len):
    n_lat, seq_len, _ = geom
    D = s.shape[1]
    tm = HY_TM
    H = SUBLANES
    blocks_per_tile = tm // H
    last_block = s.shape[0] // H - 1
    kern = partial(_hy_in_kernel, tm=tm, n_lat=n_lat, tiles_per_seq=seq_len // tm,
                   tiles_per_ctx=max(ctx_len // tm, 1))
    out = jax.ShapeDtypeStruct((n_rows, D), BF16)
    return pl.pallas_call(
        kern,
        out_shape=[out, out],
        grid=(n_rows // tm,),
        in_specs=[
            pl.BlockSpec((tm, D), lambda t: (t, 0)),
            pl.BlockSpec((H, D), lambda t: (jnp.maximum(t * blocks_per_tile - 1, 0), 0)),
            pl.BlockSpec((H, D), lambda t: (jnp.minimum((t + 1) * blocks_per_tile, last_block), 0)),
            pl.BlockSpec((1, 1, N_MOD, D), _mod_spec(layer, tm, *geom)),
            pl.BlockSpec((1, 1, D), lambda t: (3 * layer + 1, 0, 0), pipeline_mode=pl.Buffered(1)),
            _const_spec((D, 3 * D)),
            _const_spec((1, 3 * D)),
            _const_spec((3, 3 * D)),
            _const_spec((1, 3 * D)),
        ],
        out_specs=[pl.BlockSpec((tm, D), lambda t: (t, 0))] * 2,
        compiler_params=_cparams("arbitrary"),
        name="hyena_in",
    )(s, s, s, mod, gpre, w_in, b_in, w_sh, b_sh)


def _hy_filter_kernel(z_ref, f0_ref, fb0_ref, f1_ref, fb1_ref, f2_ref, fb2_ref, f3_ref, fr_ref, dl_ref,
                      kfb_ref):
    D = dl_ref.shape[1]
    hp = lax.Precision.HIGHEST
    z = z_ref[...]
    fr = fr_ref[...]
    a = jnp.sin(fr * (jnp.dot(z, f0_ref[...], precision=hp, preferred_element_type=F32) + fb0_ref[...]))
    a = jnp.sin(fr * (jnp.dot(a, f1_ref[...], precision=hp, preferred_element_type=F32) + fb1_ref[...]))
    a = jnp.sin(fr * (jnp.dot(a, f2_ref[...], precision=hp, preferred_element_type=F32) + fb2_ref[...]))
    k = jnp.dot(a.astype(BF16), f3_ref[...].astype(BF16), preferred_element_type=F32)
    decay = jnp.exp(-z[:, 0:1] * dl_ref[...])
    k_f = k[:, :D] * decay
    k_b = k[:, D:] * decay
    row = lax.broadcasted_iota(jnp.int32, (z.shape[0], 1), 0) + pl.program_id(0) * z.shape[0]
    k_b = jnp.where(row == 0, 0.0, k_b)
    kfb_ref[0] = k_f.astype(BF16)
    kfb_ref[1] = k_b.astype(BF16)


def _hy_filter(L, f0, fb0, f1, fb1, f2, fb2, f3, freq, D):
    t = jnp.linspace(0.0, 1.0, L, dtype=F32)[:, None]
    bands = (HY_EMB - 1) // 2
    w = 2.0 * math.pi * jnp.arange(L, dtype=F32)[:, None] / L
    f = jnp.linspace(1e-4, bands - 1, bands, dtype=F32)[None]
    z = jnp.concatenate([t, jnp.cos(f * w), -jnp.sin(f * w),
                         jnp.zeros((L, HY_EMB_PAD - HY_EMB), F32)], -1)
    f0p = jnp.concatenate([f0, jnp.zeros((HY_EMB_PAD - HY_EMB, f0.shape[1]), F32)], 0)
    deltas = jnp.abs(jnp.linspace(HY_MIN_DECAY, HY_MAX_DECAY, D, dtype=F32))[None]
    tl = min(512, L)
    O = f0.shape[1]
    return pl.pallas_call(
        _hy_filter_kernel,
        out_shape=jax.ShapeDtypeStruct((2, L, D), BF16),
        grid=(L // tl,),
        in_specs=[pl.BlockSpec((tl, HY_EMB_PAD), lambda i: (i, 0)),
                  _const_spec((HY_EMB_PAD, O)), _const_spec((1, O)),
                  _const_spec((O, O)), _const_spec((1, O)),
                  _const_spec((O, O)), _const_spec((1, O)),
                  _const_spec((O, 2 * D)), _const_spec((1, O)), _const_spec((1, D))],
        out_specs=pl.BlockSpec((2, tl, D), lambda i: (0, i, 0)),
        compiler_params=_cparams("arbitrary"),
        name="hyena_filter",
    )(z, f0p, fb0[None], f1, fb1[None], f2, fb2[None], f3, freq[None], deltas)


def _dft_matrices(L, tf):
    n_hi = L // 64 if L >= 64 else 1
    n_lo = L // n_hi
    k2 = 2 * jnp.arange(L, dtype=jnp.int32)[:, None] + 1
    step = 2.0 * math.pi / (4 * L)

    def angles(n):
        return ((k2 * n[None, :]) % (4 * L)).astype(F32) * step

    a_hi = angles(jnp.arange(n_hi, dtype=jnp.int32) * n_lo)
    a_lo = angles(jnp.arange(n_lo, dtype=jnp.int32))
    c1, s1 = jnp.cos(a_hi)[:, :, None], jnp.sin(a_hi)[:, :, None]
    c0, s0 = jnp.cos(a_lo)[:, None, :], jnp.sin(a_lo)[:, None, :]
    cos = (c1 * c0 - s1 * s0).reshape(L, L)
    nsin = (-(s1 * c0 + c1 * s0)).reshape(L, L)
    fwd = jnp.concatenate([cos.reshape(L // tf, tf, L), nsin.reshape(L // tf, tf, L)], axis=1).reshape(2 * L, L)
    inv = fwd.T * (1.0 / L)
    return fwd.astype(BF16), inv.astype(BF16)


def _spectrum_kernel(w_ref, kfb_ref, o_ref):
    tf = w_ref.shape[0] // 2
    k_f = kfb_ref[0].astype(F32)
    k_b = kfb_ref[1].astype(F32)
    o_ref[0:tf, :] = jnp.dot(w_ref[0:tf, :], (k_f + k_b).astype(BF16), preferred_element_type=F32)
    o_ref[tf:, :] = jnp.dot(w_ref[tf:, :], (k_f - k_b).astype(BF16), preferred_element_type=F32)


def _filter_spectrum(fwd, kfb, tf):
    _, L, D = kfb.shape
    return pl.pallas_call(
        _spectrum_kernel,
        out_shape=jax.ShapeDtypeStruct((2 * L, D), F32),
        grid=(L // tf,),
        in_specs=[pl.BlockSpec((2 * tf, L), lambda j: (j, 0)), _const_spec((2, L, D))],
        out_specs=pl.BlockSpec((2 * tf, D), lambda j: (j, 0)),
        compiler_params=_cparams("arbitrary"),
        name="filter_spectrum",
    )(fwd, kfb)


def _dft_fwd_kernel(w_ref, u_ref, kh_ref, o_ref):
    tf = w_ref.shape[0] // 2
    uh = jnp.dot(w_ref[...], u_ref[...], preferred_element_type=F32)
    ur, ui = uh[:tf], uh[tf:]
    kr, ki = kh_ref[0:tf, :], kh_ref[tf:, :]
    o_ref[0:tf, :] = (ur * kr - ui * ki).astype(BF16)
    o_ref[tf:, :] = (ur * ki + ui * kr).astype(BF16)


def _dft_forward(fwd, u, khat, *, n_batch, L, row_off, tf):
    D = u.shape[1]
    seq0 = row_off // L
    return pl.pallas_call(
        _dft_fwd_kernel,
        out_shape=jax.ShapeDtypeStruct((n_batch * 2 * L, D), BF16),
        grid=(n_batch, L // tf),
        in_specs=[pl.BlockSpec((2 * tf, L), lambda b, j: (j, 0)),
                  pl.BlockSpec((L, D), lambda b, j: (seq0 + b, 0)),
                  pl.BlockSpec((2 * tf, D), lambda b, j: (j, 0))],
        out_specs=pl.BlockSpec((2 * tf, D), lambda b, j: (b * (L // tf) + j, 0)),
        compiler_params=_cparams("arbitrary", "arbitrary"),
        name="dft_forward",
    )(fwd, u, khat)


def _dft_inv_kernel(w_ref, y_ref, u_ref, x0_ref, bias_ref, *rest):
    o_ref, acc = rest[-2], rest[-1]
    kk = pl.program_id(2)

    @pl.when(kk == 0)
    def _():
        acc[...] = jnp.zeros_like(acc)

    acc[...] += jnp.dot(w_ref[...], y_ref[...], preferred_element_type=F32)

    @pl.when(kk == pl.num_programs(2) - 1)
    def _():
        y = acc[...] + u_ref[...].astype(F32) * bias_ref[...]
        o_ref[...] = (x0_ref[...].astype(F32) * y).astype(BF16)


def _dft_inverse(inv, yhat, u, x0, bias, prev, *, n_batch, L, row_off, n_rows_out):
    D = u.shape[1]
    tm = min(1024, L)
    tk = min(2048, 2 * L)
    m_tiles = L // tm
    k_tiles = 2 * L // tk
    blk0 = row_off // tm

    def rows(b, i, kk):
        return (blk0 + b * m_tiles + i, 0)

    in_specs = [pl.BlockSpec((tm, tk), lambda b, i, kk: (i, kk)),
                pl.BlockSpec((tk, D), lambda b, i, kk: (b * k_tiles + kk, 0)),
                pl.BlockSpec((tm, D), rows), pl.BlockSpec((tm, D), rows), _const_spec((1, D))]
    args = [inv, yhat, u, x0, bias]
    aliases = {}
    if prev is not None:
        in_specs.append(pl.BlockSpec(memory_space=pl.ANY))
        args.append(prev)
        aliases = {5: 0}
    return pl.pallas_call(
        _dft_inv_kernel,
        out_shape=jax.ShapeDtypeStruct((n_rows_out, D), BF16),
        grid=(n_batch, m_tiles, k_tiles),
        in_specs=in_specs,
        out_specs=pl.BlockSpec((tm, D), rows),
        scratch_shapes=[pltpu.VMEM((tm, D), F32)],
        input_output_aliases=aliases,
        compiler_params=_cparams("arbitrary", "arbitrary", "arbitrary"),
        name="dft_inverse",
    )(*args)


FFT_N2 = 256
FFT_J = 16
FFT_JB = 64


def _fft_expand_kernel(t_ref, tt_ref, ma_ref, mi_ref):
    J = FFT_J
    R, H1 = t_ref.shape[1], t_ref.shape[2]
    W = H1 * J
    Rp = tt_ref.shape[2]
    hp = lax.Precision.HIGHEST
    col = lax.broadcasted_iota(jnp.int32, (1, W), 1)
    spread = (lax.broadcasted_iota(jnp.int32, (H1, 1), 0) == col // J).astype(F32)
    row_j = lax.broadcasted_iota(jnp.int32, (R, 1), 0) % J
    ma = jnp.dot(t_ref[0], spread, precision=hp, preferred_element_type=F32)
    ma_ref[0] = jnp.where(row_j == col % J, ma, 0.0).astype(BF16)
    rowi = lax.broadcasted_iota(jnp.int32, (W, 1), 0)
    spread_t = (rowi // J == lax.broadcasted_iota(jnp.int32, (1, H1), 1)).astype(F32)
    col_j = lax.broadcasted_iota(jnp.int32, (1, Rp), 1) % J
    mi = jnp.dot(spread_t, tt_ref[0], precision=hp, preferred_element_type=F32)
    mi_ref[0] = jnp.where(rowi % J == col_j, mi, 0.0).astype(BF16)


def _fft_matrices(L):
    N = 2 * L
    N1 = N // FFT_N2
    H1, K1, J, A = N1 // 2, N1 // 2 + 1, FFT_J, FFT_N2 // FFT_J
    R = K1 * 2 * J
    Rp = R + (-R) % LANES
    a = jnp.arange(A, dtype=jnp.int32)[:, None, None, None]
    k1 = jnp.arange(K1, dtype=jnp.int32)[None, :, None, None]
    j = jnp.arange(J, dtype=jnp.int32)[None, None, :, None]
    n1 = jnp.arange(H1, dtype=jnp.int32)[None, None, None, :]
    ang = ((k1 * (FFT_N2 * n1 + J * a + j)) % N).astype(F32) * (2.0 * math.pi / N)
    base = jnp.stack([jnp.cos(ang), -jnp.sin(ang)], axis=2)
    kk = jnp.arange(K1)
    w = jnp.where((kk == 0) | (kk == N1 // 2), 1.0, 2.0) / N
    t = base.reshape(A, R, H1)
    tt = jnp.transpose((base * w[None, :, None, None, None]).reshape(A, R, H1), (0, 2, 1))
    tt = jnp.pad(tt, ((0, 0), (0, 0), (0, Rp - R)))
    ma, mi = pl.pallas_call(
        _fft_expand_kernel,
        out_shape=[jax.ShapeDtypeStruct((A, R, H1 * J), BF16), jax.ShapeDtypeStruct((A, H1 * J, Rp), BF16)],
        grid=(A,),
        in_specs=[pl.BlockSpec((1, R, H1), lambda i: (i, 0, 0)), pl.BlockSpec((1, H1, Rp), lambda i: (i, 0, 0))],
        out_specs=[pl.BlockSpec((1, R, H1 * J), lambda i: (i, 0, 0)),
                   pl.BlockSpec((1, H1 * J, Rp), lambda i: (i, 0, 0))],
        compiler_params=_cparams("arbitrary"),
        name="fft_matrices",
    )(t, tt)
    n2 = jnp.arange(FFT_N2, dtype=jnp.int32)
    angf = ((n2[:, None] * n2[None, :]) % FFT_N2).astype(F32) * (2.0 * math.pi / FFT_N2)
    fr, fi = jnp.cos(angf), -jnp.sin(angf)
    cat = lambda top, bot: jnp.concatenate([top, bot], axis=0).astype(BF16)
    return dict(ma=ma, mi=mi, k1=K1, h1=H1,
                mre=cat(fr, fi), mim=cat(-fi, fr), gre=cat(fr, -fi), gim=cat(fi, fr))


def _fft_a_kernel(u_ref, ma_ref, z_ref):
    H1, JB, D = u_ref.shape
    K1 = z_ref.shape[1]
    J = FFT_J
    a = pl.program_id(1)
    for q in range(JB // J):
        rows = slice(q * J, (q + 1) * J)
        data = u_ref[:, rows, :].reshape(H1 * J, D)
        res = jnp.dot(ma_ref[a * (JB // J) + q], data, preferred_element_type=F32)
        z_ref[0, :, :, rows, :] = res.astype(BF16).reshape(K1, 2, J, D)


def _fft_a(u3, mats, *, n_batch):
    D = u3.shape[2]
    K1, H1 = mats["k1"], mats["h1"]
    return pl.pallas_call(
        _fft_a_kernel,
        out_shape=jax.ShapeDtypeStruct((n_batch, K1, 2, FFT_N2, D), BF16),
        grid=(n_batch, FFT_N2 // FFT_JB),
        in_specs=[pl.BlockSpec((H1, FFT_JB, D), lambda b, a: (b, a, 0)), _const_spec(mats["ma"].shape)],
        out_specs=pl.BlockSpec((1, K1, 2, FFT_JB, D), lambda b, a: (b, 0, 0, a, 0)),
        compiler_params=_cparams("arbitrary", "arbitrary"),
        name="fft_stage_a",
    )(u3, mats["ma"])


def _dft256(mre_ref, mim_ref, z_ref, idx):
    n2 = FFT_N2
    x = (jnp.dot(mre_ref[...], z_ref[idx + (0,)], preferred_element_type=F32)
         + jnp.dot(mim_ref[...], z_ref[idx + (1,)], preferred_element_type=F32))
    return x[:n2], x[n2:]


def _fft_spec_kernel(zf_ref, zb_ref, mre_ref, mim_ref, o_ref):
    fr, fi = _dft256(mre_ref, mim_ref, zf_ref, (0, 0))
    br, bi = _dft256(mre_ref, mim_ref, zb_ref, (0, 0))
    o_ref[0, 0] = fr + br
    o_ref[0, 1] = fi - bi


def _fft_spectrum(zfilt, mats):
    _, K1, _, n2, D = zfilt.shape
    blk = (1, 1, 2, n2, D)
    return pl.pallas_call(
        _fft_spec_kernel,
        out_shape=jax.ShapeDtypeStruct((K1, 2, n2, D), F32),
        grid=(K1,),
        in_specs=[pl.BlockSpec(blk, lambda k: (0, k, 0, 0, 0)), pl.BlockSpec(blk, lambda k: (1, k, 0, 0, 0)),
                  _const_spec(mats["mre"].shape), _const_spec(mats["mim"].shape)],
        out_specs=pl.BlockSpec((1, 2, n2, D), lambda k: (k, 0, 0, 0)),
        compiler_params=_cparams("arbitrary"),
        name="fft_filter_spectrum",
    )(zfilt, zfilt, mats["mre"], mats["mim"])


def _fft_mid_kernel(z_ref, kh_ref, mre_ref, mim_ref, gre_ref, gim_ref, o_ref):
    n2 = FFT_N2
    xr, xi = _dft256(mre_ref, mim_ref, z_ref, (0, 0))
    kr, ki = kh_ref[0, 0], kh_ref[0, 1]
    yr = (xr * kr - xi * ki).astype(BF16)
    yi = (xr * ki + xi * kr).astype(BF16)
    zp = (jnp.dot(gre_ref[...], yr, preferred_element_type=F32)
          + jnp.dot(gim_ref[...], yi, preferred_element_type=F32))
    o_ref[0, 0, 0] = zp[:n2].astype(BF16)
    o_ref[0, 0, 1] = zp[n2:].astype(BF16)


def _fft_mid(z, khat, mats):
    n_batch, K1, _, n2, D = z.shape
    blk = pl.BlockSpec((1, 1, 2, n2, D), lambda k, b: (b, k, 0, 0, 0))
    mat = [_const_spec(mats[m].shape) for m in ("mre", "mim", "gre", "gim")]
    return pl.pallas_call(
        _fft_mid_kernel,
        out_shape=jax.ShapeDtypeStruct(z.shape, BF16),
        grid=(K1, n_batch),
        in_specs=[blk, pl.BlockSpec((1, 2, n2, D), lambda k, b: (k, 0, 0, 0))] + mat,
        out_specs=blk,
        compiler_params=_cparams("arbitrary", "arbitrary"),
        name="fft_stage_b",
    )(z, khat, mats["mre"], mats["mim"], mats["gre"], mats["gim"])


def _fft_a_inv_kernel(zp_ref, mi_ref, u_ref, x0_ref, bias_ref, o_ref):
    H1, JB, D = u_ref.shape
    K1 = zp_ref.shape[1]
    J = FFT_J
    a = pl.program_id(1)
    k_pad = mi_ref.shape[2] - K1 * 2 * J
    for q in range(JB // J):
        rows = slice(q * J, (q + 1) * J)
        zz = zp_ref[0, :, :, rows, :].reshape(K1 * 2 * J, D)
        if k_pad:
            zz = jnp.concatenate([zz, jnp.zeros((k_pad, D), BF16)], axis=0)
        y = jnp.dot(mi_ref[a * (JB // J) + q], zz, preferred_element_type=F32).reshape(H1, J, D)
        y = y + u_ref[:, rows, :].astype(F32) * bias_ref[...]
        o_ref[:, rows, :] = (x0_ref[:, rows, :].astype(F32) * y).astype(BF16)


def _fft_a_inv(zp, mats, u3, x03, bias, *, n_blocks_out):
    n_batch, K1, _, n2, D = zp.shape
    H1 = mats["h1"]
    rows = pl.BlockSpec((H1, FFT_JB, D), lambda b, a: (b, a, 0))
    return pl.pallas_call(
        _fft_a_inv_kernel,
        out_shape=jax.ShapeDtypeStruct((n_blocks_out, n2, D), BF16),
        grid=(n_batch, n2 // FFT_JB),
        in_specs=[pl.BlockSpec((1, K1, 2, FFT_JB, D), lambda b, a: (b, 0, 0, a, 0)),
                  _const_spec(mats["mi"].shape), rows, rows, _const_spec((1, D))],
        out_specs=rows,
        compiler_params=_cparams("arbitrary", "arbitrary"),
        name="fft_stage_a_inv",
    )(zp, mats["mi"], u3, x03, bias)


def kernel(x, c, ctx, c_ctx, w_mod, b_mod, norm_pre, norm_post, ffn_gate, ffn_up, ffn_down,
           mix_w_in, attn_sink, ret_decay, mix_w_out, hy_w_in, hy_b_in, hy_short_w, hy_short_b,
           hy_f0, hy_fb0, hy_f1, hy_fb1, hy_f2, hy_fb2, hy_f3, hy_freq, hy_bias, hy_w_out):
    Bn, L, D = x.shape
    C = ctx.shape[1]
    n_lat = Bn * L
    n_all = n_lat + Bn * C
    geom = (n_lat, L, Bn)
    assert L % PROJ_TM == 0 and L % HY_TM == 0 and n_lat % C == 0 and (Bn * C) % PROJ_TM == 0
    assert L % min(FFN_TM, L) == 0 and (Bn * C) % min(FFN_TM, L) == 0
    assert C % (RET_CH * R_CHUNK) == 0 and L % (RET_CH * R_CHUNK) == 0 and C % HY_TM == 0
    assert L % FFT_N2 == 0 and n_all % FFT_N2 == 0
    last_reader = DEPTH - 1 if (DEPTH - 1) % 2 == 0 else DEPTH - 2

    c_all = jnp.concatenate([c, c_ctx[None], jnp.zeros((SUBLANES - (Bn + 1) % SUBLANES, D), F32)], axis=0)
    mod = _modulation(c_all, w_mod, b_mod).reshape(DEPTH, c_all.shape[0], N_MOD, D)

    rope_a = _rope_tables(_axial_angles(L), PROJ_TM)
    rope_r = _rope_tables(_line_angles(L), PROJ_TM)
    fft = _fft_matrices(L)

    npre = norm_pre.reshape(DEPTH * 3, 1, D)
    npost = norm_post.reshape(DEPTH * 3, 1, D)
    wg, wu, wd = ffn_gate.astype(BF16), ffn_up.astype(BF16), ffn_down.astype(BF16)
    mix_wo, hy_wo = mix_w_out.astype(BF16), hy_w_out.astype(BF16)

    def ffn(s, l, i, j, n_rows, mix=None):
        return _ffn(s, mod, npre, npost, wg, wu, wd, layer=l, i=i, j=j, n_rows=n_rows, geom=geom, mix=mix)

    s = jnp.concatenate([x.reshape(n_lat, D), ctx.reshape(Bn * C, D)], axis=0)
    for l in range(DEPTH):
        ctx_live = l <= last_reader
        ctx_full = l < last_reader
        n_in = n_all if ctx_live else n_lat
        n_out = n_all if ctx_full else n_lat
        s = ffn(s, l, 0, 0, n_in)
        if l % 2 == 0:
            e = l // 2
            aq, rq, rg, akd, avd, rk, rv = _even_in(s, mod, npre, _pack_even_w_in(mix_w_in[e]), rope_a, rope_r,
                                                    layer=l, n_rows=n_in, geom=geom)
            sink_lanes = jnp.broadcast_to(attn_sink[e][:, None], (A_Q_HEADS, LANES))
            dec_lanes = jnp.repeat(ret_decay[e], R_DIM, axis=1)
            a = _attention(aq, akd, avd, sink_lanes, n_batch=Bn, seq_len=L, ctx_len=C,
                           with_ctx_queries=ctx_full)
            tab_tt, tab_rows = _retention_tables(dec_lanes)
            sf, sb = _retention_states(rk, rv, tab_rows, n_batch=Bn, seq_len=L, ctx_len=C)
            r = _retention_out(rq, rk, rv, rg, sf, sb, tab_tt, tab_rows, n_rows=n_out)
            mix = ([a, r], mix_wo, e)
        else:
            o = l // 2
            x0, u = _hy_in(s, mod, npre, hy_w_in[o].astype(BF16), hy_b_in[o][None], hy_short_w[o],
                           hy_short_b[o][None], layer=l, n_rows=n_in, geom=geom, ctx_len=C)
            filt = (hy_f0[o], hy_fb0[o], hy_f1[o], hy_fb1[o], hy_f2[o], hy_fb2[o], hy_f3[o], hy_freq[o])
            bias = hy_bias[o][None]
            blocks = lambda arr: arr.reshape(arr.shape[0] // FFT_N2, FFT_N2, D)
            kfb = _hy_filter(L, *filt, D)
            khat = _fft_spectrum(_fft_a(blocks(kfb.reshape(2 * L, D)), fft, n_batch=2), fft)
            zp = _fft_mid(_fft_a(blocks(u), fft, n_batch=Bn), khat, fft)
            yg = _fft_a_inv(zp, fft, blocks(u), blocks(x0), bias, n_blocks_out=n_out // FFT_N2).reshape(n_out, D)
            if ctx_full:
                tf = min(DFT_TF, C)
                fwd, inv = _dft_matrices(C, tf)
                khat_c = _filter_spectrum(fwd, _hy_filter(C, *filt, D), tf)
                yhat = _dft_forward(fwd, u, khat_c, n_batch=Bn, L=C, row_off=n_lat, tf=tf)
                yg = _dft_inverse(inv, yhat, u, x0, bias, yg, n_batch=Bn, L=C, row_off=n_lat, n_rows_out=n_out)
            mix = ([yg], hy_wo, o)
        s = ffn(s, l, 2, 1, n_out, mix)
    return s[:n_lat].reshape(Bn, L, D)
```

```python
import math
from functools import partial

import jax
import jax.numpy as jnp
import numpy as np
from jax import lax
from jax.experimental import pallas as pl
from jax.experimental.pallas import tpu as pltpu

F32 = jnp.float32
BF16 = jnp.bfloat16

DEPTH = 4
GRID_W = 64
EPS = 1e-6
NEG = -1e30
N_MOD = 9
FFN_RESIDUAL = 0.5
HEAD_DIM = 64
A_Q_HEADS = 8
A_KV_HEADS = 2
WINDOW = 128
BLOCK = 128
ROPE_BASE = 10000.0
R_DIM = 64
R_HEADS = 8
R_CHUNK = 128
HY_EMB = 33
HY_EMB_PAD = 64
HY_MAX_DECAY = math.log(1e-2) / 0.3
HY_MIN_DECAY = math.log(1e-2) / 1.5

LANES = 128
SUBLANES = 8
V7X_VMEM_LIMIT_BYTES = 56 * 1024 * 1024

FFN_TM = 1024
FFN_SUB = 2
FFN_TF = 256
PROJ_TM = 512
HY_TM = 256
MOD_TN = 2304
DFT_TF = 256
RET_CH = 2
RET_OUT_CH = 4
ATT_RC = 128


def _cparams(*sem):
    return pltpu.CompilerParams(dimension_semantics=sem, vmem_limit_bytes=V7X_VMEM_LIMIT_BYTES)


def _const_spec(shape):
    zeros = (0,) * len(shape)
    return pl.BlockSpec(shape, lambda *_: zeros, pipeline_mode=pl.Buffered(1))


def _mod_spec(layer, tm, n_lat, seq_len, n_batch):
    def index(t, *_):
        return (layer, jnp.where(t * tm < n_lat, (t * tm) // seq_len, n_batch), 0, 0)
    return index


def _adaln_in(x, mod_ref, gpre_ref, i):
    shift = mod_ref[0, 0, 3 * i:3 * i + 1, :]
    scale = mod_ref[0, 0, 3 * i + 1:3 * i + 2, :]
    gain = gpre_ref[0] * (1.0 + scale)
    inv = lax.rsqrt(jnp.mean(x * x, axis=-1, keepdims=True) + EPS)
    return (x * inv) * gain + shift


def _adaln_out(x, y, mod_ref, gpost_ref, i, w):
    gate = mod_ref[0, 0, 3 * i + 2:3 * i + 3, :]
    gain = (w * gate) * gpost_ref[0]
    inv = lax.rsqrt(jnp.mean(y * y, axis=-1, keepdims=True) + EPS)
    return x + (y * inv) * gain


def _mod_kernel(c_ref, w_ref, b_ref, o_ref):
    c = c_ref[...]
    a = (c * jax.nn.sigmoid(c)).astype(BF16)
    o_ref[0] = jnp.dot(a, w_ref[0].astype(BF16), preferred_element_type=F32) + b_ref[0]


def _modulation(c_all, w_mod, b_mod):
    depth, D, W = w_mod.shape
    rows = c_all.shape[0]
    return pl.pallas_call(
        _mod_kernel,
        out_shape=jax.ShapeDtypeStruct((depth, rows, W), F32),
        grid=(depth, W // MOD_TN),
        in_specs=[
            pl.BlockSpec((rows, D), lambda l, j: (0, 0)),
            pl.BlockSpec((1, D, MOD_TN), lambda l, j: (l, 0, j)),
            pl.BlockSpec((1, 1, MOD_TN), lambda l, j: (l, 0, j)),
        ],
        out_specs=pl.BlockSpec((1, rows, MOD_TN), lambda l, j: (l, 0, j)),
        compiler_params=_cparams("arbitrary", "arbitrary"),
        name="modulation",
    )(c_all, w_mod, b_mod.reshape(depth, 1, W))


def _ffn_kernel(*refs, i, tf, n_chunks, n_parts, n_sub, split, lat_tiles):
    row_refs, k = [], 0
    for is_split in split:
        row_refs.append(refs[k:k + 2] if is_split else refs[k:k + 1])
        k += 2 if is_split else 1
    mod_ref, npre_ref, npost_ref = refs[k:k + 3]
    rest = refs[k + 3:]
    if n_parts:
        nmix_ref, wo_ref = rest[:2]
        rest = rest[2:]
    wg_ref, wu_ref, wd_ref, o_ref = rest
    is_ctx = (lax.broadcasted_iota(jnp.int32, (1, 1), 0) + pl.program_id(0)) >= lat_tiles

    def read(src, rows):
        if len(src) == 1:
            return src[0][rows, :]
        return jnp.where(is_ctx, src[1][rows, :], src[0][rows, :])

    sub = o_ref.shape[0] // n_sub
    for b in range(n_sub):
        rows = slice(b * sub, (b + 1) * sub)
        x = read(row_refs[-1], rows)
        if n_parts:
            y = None
            row = 0
            for src in row_refs[:-1]:
                width = src[0].shape[1]
                part = jnp.dot(read(src, rows), wo_ref[0, row:row + width, :], preferred_element_type=F32)
                y = part if y is None else y + part
                row += width
            x = _adaln_out(x, y, mod_ref, nmix_ref, 1, 1.0)
        hb = _adaln_in(x, mod_ref, npre_ref, i).astype(BF16)
        acc = jnp.zeros(x.shape, F32)
        for j in range(n_chunks):
            cols = slice(j * tf, (j + 1) * tf)
            g = jnp.dot(hb, wg_ref[0, 0, :, cols], preferred_element_type=F32)
            u = jnp.dot(hb, wu_ref[0, 0, :, cols], preferred_element_type=F32)
            a = (g * jax.nn.sigmoid(g) * u).astype(BF16)
            acc = acc + jnp.dot(a, wd_ref[0, 0, cols, :], preferred_element_type=F32)
        o_ref[rows, :] = _adaln_out(x, acc, mod_ref, npost_ref, i, FFN_RESIDUAL)


def _ffn(s, mod, npre, npost, wg, wu, wd, *, layer, i, j, n_rows, geom, mix=None):
    n_lat = geom[0]
    D = wd.shape[3]
    d_ff = wd.shape[2]
    tm = min(FFN_TM, geom[1])
    lat_tiles = n_lat // tm
    rows = lambda t: (t, 0)
    norm = lambda sub: pl.BlockSpec((1, 1, D), lambda t: (3 * layer + sub, 0, 0), pipeline_mode=pl.Buffered(1))
    weight = lambda shape: pl.BlockSpec((1, 1) + shape, lambda t: (layer, j, 0, 0), pipeline_mode=pl.Buffered(1))
    parts, extra_specs, extra_args = [], [], []
    if mix is not None:
        parts, w_out, e = mix
        extra_specs = [norm(1), pl.BlockSpec((1,) + w_out.shape[1:], lambda t: (e, 0, 0),
                                             pipeline_mode=pl.Buffered(1))]
        extra_args = [npost, w_out]
    row_specs, row_args, split = [], [], []
    for src in list(parts) + [s]:
        if isinstance(src, tuple):
            lat, ctx = src
            row_specs += [pl.BlockSpec((tm, lat.shape[1]), lambda t: (jnp.minimum(t, lat_tiles - 1), 0)),
                          pl.BlockSpec((tm, ctx.shape[1]), lambda t: (jnp.maximum(t - lat_tiles, 0), 0))]
            row_args += [lat, ctx]
        else:
            row_specs.append(pl.BlockSpec((tm, src.shape[1]), rows))
            row_args.append(src)
        split.append(isinstance(src, tuple))
    return pl.pallas_call(
        partial(_ffn_kernel, i=i, tf=FFN_TF, n_chunks=d_ff // FFN_TF, n_parts=len(parts), n_sub=FFN_SUB,
                split=tuple(split), lat_tiles=lat_tiles),
        out_shape=jax.ShapeDtypeStruct((n_rows, D), F32),
        grid=(n_rows // tm,),
        in_specs=row_specs + [pl.BlockSpec((1, 1, N_MOD, D), _mod_spec(layer, tm, *geom)), norm(i), norm(i)]
        + extra_specs + [weight((D, d_ff)), weight((D, d_ff)), weight((d_ff, D))],
        out_specs=pl.BlockSpec((tm, D), rows),
        compiler_params=_cparams("arbitrary"),
        name="ffn",
    )(*row_args, mod, npre, npost, *extra_args, wg, wu, wd)


_AQ = (0, 512)
_RQ = (512, 1024)
_RG = (1024, 1536)
_AK = (1536, 1792)
_AV = (1792, 2048)
_RK = (2048, 2560)
_RV = (2560, 3072)
EVEN_W = 3072


def _pack_even_w_in(w_in):
    aq, rq, rg = w_in[:, 0:512], w_in[:, 512:1024], w_in[:, 1024:1536]
    ak, av = w_in[:, 1536:1664], w_in[:, 1664:1792]
    rk, rv = w_in[:, 1792:2304], w_in[:, 2304:2816]

    def dup(a):
        g0, g1 = a[:, :HEAD_DIM], a[:, HEAD_DIM:]
        return jnp.concatenate([g0, g0, g1, g1], axis=1)

    return jnp.concatenate([aq, rq, rg, dup(ak), dup(av), rk, rv], axis=1).astype(BF16)


def _rope_tables(ang, ident_rows):
    L = ang.shape[0]
    cos, sin = jnp.cos(ang), jnp.sin(ang)
    zero = jnp.zeros_like(sin)
    cos_t = jnp.tile(cos, (1, 4))
    sin_a = jnp.tile(jnp.concatenate([-sin, zero], axis=1), (1, 2))
    sin_b = jnp.tile(jnp.concatenate([zero, sin], axis=1), (1, 2))
    pad_one = jnp.ones((ident_rows, LANES), F32)
    pad_zero = jnp.zeros((ident_rows, LANES), F32)
    return (jnp.concatenate([cos_t, pad_one], 0), jnp.concatenate([sin_a, pad_zero], 0),
            jnp.concatenate([sin_b, pad_zero], 0))


def _axial_angles(L):
    n_rows = L // GRID_W
    row = jnp.repeat(jnp.arange(n_rows, dtype=F32), GRID_W)
    col = jnp.tile(jnp.arange(GRID_W, dtype=F32), n_rows)
    nf = HEAD_DIM // 4
    inv = ROPE_BASE ** (-jnp.arange(nf, dtype=F32) / nf)
    return jnp.concatenate([row[:, None] * inv, col[:, None] * inv], -1)


def _line_angles(L):
    inv = ROPE_BASE ** (-jnp.linspace(0.0, 1.0, R_DIM // 2, dtype=F32))
    return jnp.arange(L, dtype=F32)[:, None] * inv


def _rope(z, cos, sin_a, sin_b):
    outs = []
    for c in range(z.shape[1] // LANES):
        zc = z[:, c * LANES:(c + 1) * LANES]
        outs.append(zc * cos + pltpu.roll(zc, 96, 1) * sin_a + pltpu.roll(zc, 32, 1) * sin_b)
    return outs[0] if len(outs) == 1 else jnp.concatenate(outs, axis=1)


def _even_in_kernel(x_ref, mod_ref, gpre_ref, w_ref, ca_ref, saa_ref, sab_ref, cr_ref, sra_ref, srb_ref,
                    aq_ref, rq_ref, rg_ref, ak_ref, av_ref, rk_ref, rv_ref):
    hb = _adaln_in(x_ref[...], mod_ref, gpre_ref, 1).astype(BF16)

    def proj(cols):
        return jnp.dot(hb, w_ref[:, cols[0]:cols[1]], preferred_element_type=F32)

    rope_a = (ca_ref[...], saa_ref[...], sab_ref[...])
    rope_r = (cr_ref[...], sra_ref[...], srb_ref[...])
    aq_ref[...] = (_rope(proj(_AQ), *rope_a) * HEAD_DIM ** -0.5).astype(BF16)
    rq_ref[...] = _rope(proj(_RQ), *rope_r).astype(BF16)
    rg_ref[...] = proj(_RG)
    ak_ref[...] = _rope(proj(_AK), *rope_a).astype(BF16)
    av_ref[...] = proj(_AV).astype(BF16)
    rk_ref[...] = (_rope(proj(_RK), *rope_r) * R_DIM ** -0.5).astype(BF16)
    rv_ref[...] = proj(_RV).astype(BF16)


def _even_in(s, mod, gpre, w_pack, rope_a, rope_r, *, layer, n_rows, geom):
    n_lat, seq_len, _ = geom
    D = s.shape[1]
    tm = PROJ_TM
    tiles_per_seq = seq_len // tm

    def rope_index(t):
        return (jnp.where(t * tm < n_lat, t % tiles_per_seq, tiles_per_seq), 0)

    rope_spec = pl.BlockSpec((tm, LANES), rope_index)

    def out(width, dtype):
        return jax.ShapeDtypeStruct((n_rows, width), dtype), pl.BlockSpec((tm, width), lambda t: (t, 0))

    outs = [out(512, BF16), out(512, BF16), out(512, F32), out(256, BF16), out(256, BF16),
            out(512, BF16), out(512, BF16)]
    return pl.pallas_call(
        _even_in_kernel,
        out_shape=[o[0] for o in outs],
        grid=(n_rows // tm,),
        in_specs=[
            pl.BlockSpec((tm, D), lambda t: (t, 0)),
            pl.BlockSpec((1, 1, N_MOD, D), _mod_spec(layer, tm, *geom)),
            pl.BlockSpec((1, 1, D), lambda t: (3 * layer + 1, 0, 0), pipeline_mode=pl.Buffered(1)),
            _const_spec((D, EVEN_W)),
        ] + [rope_spec] * 6,
        out_specs=[o[1] for o in outs],
        compiler_params=_cparams("arbitrary"),
        name="even_in",
    )(s, mod, gpre, w_pack, *rope_a, *rope_r)


def _attn_kernel(q_ref, kp_ref, kc_ref, kn_ref, kx_ref, vp_ref, vc_ref, vn_ref, vx_ref, sink_ref, band_ref,
                 o_ref, *, nb):
    T = BLOCK
    j = pl.program_id(1)
    k_lo = jnp.where(j > 0, 0, T)
    k_hi = jnp.where(j < nb - 1, 3 * T, jnp.where(j < nb, 2 * T, 0))
    kj = lax.broadcasted_iota(jnp.int32, (1, band_ref.shape[1]), 1)
    exists = ((kj >= k_lo) & (kj < k_hi)) | (kj >= 3 * T)
    bias = band_ref[...] + jnp.where(exists, 0.0, NEG)
    lo = lax.broadcasted_iota(jnp.int32, (1, LANES), 1) < HEAD_DIM
    q = q_ref[...].astype(F32)
    for g in range(A_KV_HEADS):
        gl = slice(g * LANES, (g + 1) * LANES)
        kd = jnp.concatenate([kp_ref[:, gl], kc_ref[:, gl], kn_ref[:, gl], kx_ref[:, gl]], axis=0)
        vd = jnp.concatenate([vp_ref[:, gl], vc_ref[:, gl], vn_ref[:, gl], vx_ref[:, gl]], axis=0)
        qs = []
        for c in range(2):
            qc = q[:, (2 * g + c) * LANES:(2 * g + c + 1) * LANES]
            qs.append(jnp.where(lo, qc, 0.0))
            qs.append(jnp.where(lo, 0.0, qc))
        q_stack = jnp.concatenate(qs, axis=0).astype(BF16)
        s_all = lax.dot_general(q_stack, kd, (((1,), (1,)), ((), ())), preferred_element_type=F32)
        ps, inv = [], []
        for rc in range(4 * T // ATT_RC):
            rows = slice(rc * ATT_RC, (rc + 1) * ATT_RC)
            h = 4 * g + (rc * ATT_RC) // T
            sink = sink_ref[h:h + 1, 0:1]
            s = s_all[rows] + bias[rows]
            m = jnp.maximum(jnp.max(s, axis=-1, keepdims=True), sink)
            p = jnp.exp(s - m)
            inv.append(1.0 / (jnp.sum(p, axis=-1, keepdims=True) + jnp.exp(sink - m)))
            ps.append(p.astype(BF16))
        p = jnp.concatenate(ps, axis=0)
        o = jnp.dot(p, vd, preferred_element_type=F32) * jnp.concatenate(inv, axis=0)
        for c in range(2):
            oc = jnp.where(lo, o[(2 * c) * T:(2 * c + 1) * T], o[(2 * c + 1) * T:(2 * c + 2) * T])
            o_ref[:, (2 * g + c) * LANES:(2 * g + c + 1) * LANES] = oc.astype(BF16)


def _attention(aq, akd, avd, sink_lanes, *, n_batch, seq_len, ctx_len, with_ctx_queries):
    T = BLOCK
    nb = seq_len // T
    ncb = ctx_len // T
    n_lat = n_batch * seq_len
    n_q_blocks = nb + (ncb if with_ctx_queries else 0)
    n_rows = n_lat + (n_batch * ctx_len if with_ctx_queries else 0)

    def q_index(b, j):
        return (jnp.where(j < nb, b * nb + j, n_lat // T + b * ncb + (j - nb)), 0)

    def kv_index(off):
        def index(b, j):
            return (b * nb + jnp.clip(j + off, 0, nb - 1), 0)
        return index

    def ctx_index(b, j):
        return (n_lat // ctx_len + b, 0)

    kv_specs = [pl.BlockSpec((T, 2 * LANES), kv_index(-1)), pl.BlockSpec((T, 2 * LANES), kv_index(0)),
                pl.BlockSpec((T, 2 * LANES), kv_index(1)), pl.BlockSpec((ctx_len, 2 * LANES), ctx_index)]
    qi = np.arange(4 * T)[:, None] % T
    kj = np.arange(3 * T + ctx_len)[None, :]
    band = jnp.asarray(np.where((np.abs(kj - T - qi) <= WINDOW) | (kj >= 3 * T), 0.0, NEG), dtype=F32)
    return pl.pallas_call(
        partial(_attn_kernel, nb=nb),
        out_shape=jax.ShapeDtypeStruct((n_rows, A_Q_HEADS * HEAD_DIM), BF16),
        grid=(n_batch, n_q_blocks),
        in_specs=[pl.BlockSpec((T, A_Q_HEADS * HEAD_DIM), q_index)] + kv_specs + kv_specs
        + [_const_spec((A_Q_HEADS, LANES)), _const_spec(band.shape)],
        out_specs=pl.BlockSpec((T, A_Q_HEADS * HEAD_DIM), q_index),
        compiler_params=_cparams("arbitrary", "arbitrary"),
        name="attention",
    )(aq, akd, akd, akd, akd, avd, avd, avd, avd, sink_lanes, band)


def _log_sigmoid(x):
    return jnp.minimum(x, 0.0) - jnp.log(1.0 + jnp.exp(-jnp.abs(x)))


def _lo_head():
    return lax.broadcasted_iota(jnp.int32, (1, LANES), 1) < R_DIM


_ROW_DK_F, _ROW_DK_B, _ROW_DQ_F, _ROW_DQ_B, _ROW_DC = (i * R_CHUNK for i in range(5))
_RET_ROWS = 4 * R_CHUNK + SUBLANES


def _ret_tables_kernel(dec_ref, tt_ref, rows_ref):
    T = R_CHUNK
    lg_f = _log_sigmoid(dec_ref[0:1, :])
    lg_b = _log_sigmoid(dec_ref[1:2, :])
    t = lax.broadcasted_iota(jnp.int32, (T, 1), 0).astype(F32)
    rows_ref[_ROW_DK_F:_ROW_DK_F + T, :] = jnp.exp((T - 1.0 - t) * lg_f)
    rows_ref[_ROW_DK_B:_ROW_DK_B + T, :] = jnp.exp(t * lg_b)
    rows_ref[_ROW_DQ_F:_ROW_DQ_F + T, :] = jnp.exp((t + 1.0) * lg_f)
    rows_ref[_ROW_DQ_B:_ROW_DQ_B + T, :] = jnp.exp((T - t) * lg_b)
    rows_ref[_ROW_DC:_ROW_DC + SUBLANES, :] = jnp.concatenate(
        [jnp.exp(T * lg_f), jnp.exp(T * lg_b), jnp.zeros((SUBLANES - 2, lg_f.shape[1]), F32)], axis=0)
    d = t - lax.broadcasted_iota(jnp.int32, (1, T), 1).astype(F32)
    for h in range(R_HEADS):
        a = lg_f[0:1, h * R_DIM:h * R_DIM + 1]
        b = lg_b[0:1, h * R_DIM:h * R_DIM + 1]
        fwd = jnp.exp(jnp.maximum(d, 0.0) * a)
        bwd = jnp.exp(jnp.maximum(-d, 0.0) * b)
        tt_ref[h * T:(h + 1) * T, :] = jnp.where(d > 0, fwd, jnp.where(d < 0, bwd, 2.0))


def _retention_tables(dec_lanes):
    T = R_CHUNK
    W = dec_lanes.shape[1]
    return pl.pallas_call(
        _ret_tables_kernel,
        out_shape=[jax.ShapeDtypeStruct((R_HEADS * T, T), F32), jax.ShapeDtypeStruct((_RET_ROWS, W), F32)],
        grid=(1,),
        in_specs=[_const_spec((2, W))],
        out_specs=[pl.BlockSpec((R_HEADS * T, T), lambda i: (0, 0)), pl.BlockSpec((_RET_ROWS, W), lambda i: (0, 0))],
        compiler_params=_cparams("arbitrary"),
        name="retention_tables",
    )(dec_lanes)


def _ret_state_kernel(kf_ref, vf_ref, kb_ref, vb_ref, rows_ref, sf_ref, sb_ref, sf_acc, sb_acc):
    T = R_CHUNK

    @pl.when(pl.program_id(1) == 0)
    def _():
        sf_acc[...] = jnp.zeros_like(sf_acc)
        sb_acc[...] = jnp.zeros_like(sb_acc)

    lo = _lo_head()

    def chunk(acc, s_ref, k_ref, v_ref, ci, d_k, d_c):
        s_ref[ci] = acc[...].astype(BF16)
        r = slice(ci * T, (ci + 1) * T)
        kd = k_ref[r, :].astype(F32) * d_k
        for p in range(R_HEADS // 2):
            ls = slice(p * LANES, (p + 1) * LANES)
            kv = jnp.dot(kd[:, ls].T.astype(BF16), v_ref[r, ls], preferred_element_type=F32)
            rs = slice(p * R_DIM, (p + 1) * R_DIM)
            acc[rs, :] = acc[rs, :] * d_c[:, ls] + jnp.where(lo, kv[0:R_DIM], kv[R_DIM:])

    dkf = rows_ref[_ROW_DK_F:_ROW_DK_F + T, :]
    dkb = rows_ref[_ROW_DK_B:_ROW_DK_B + T, :]
    dcf = rows_ref[_ROW_DC:_ROW_DC + 1, :]
    dcb = rows_ref[_ROW_DC + 1:_ROW_DC + 2, :]
    for ci in range(RET_CH):
        chunk(sf_acc, sf_ref, kf_ref, vf_ref, ci, dkf, dcf)
    for ci in reversed(range(RET_CH)):
        chunk(sb_acc, sb_ref, kb_ref, vb_ref, ci, dkb, dcb)


def _ret_step_maps(n_batch, seq_len, ctx_len):
    rows = RET_CH * R_CHUNK
    ncs, nls = ctx_len // rows, seq_len // rows
    ctx_base = n_batch * nls

    def fwd(b, j):
        return jnp.where(j < ncs, ctx_base + b * ncs + j, b * nls + (j - ncs))

    def bwd(b, j):
        return jnp.where(j < ncs, ctx_base + b * ncs + (ncs - 1 - j), b * nls + (nls - 1 - (j - ncs)))

    return ncs + nls, fwd, bwd


def _retention_states(rk, rv, tab_rows, *, n_batch, seq_len, ctx_len):
    T = R_CHUNK
    W = R_HEADS * R_DIM
    n_steps, fwd, bwd = _ret_step_maps(n_batch, seq_len, ctx_len)
    n_chunks = n_batch * n_steps * RET_CH
    rows = pl.BlockSpec((RET_CH * T, W), lambda b, j: (fwd(b, j), 0))
    rows_b = pl.BlockSpec((RET_CH * T, W), lambda b, j: (bwd(b, j), 0))
    state = jax.ShapeDtypeStruct((n_chunks, W // 2, LANES), BF16)
    return pl.pallas_call(
        _ret_state_kernel,
        out_shape=[state, state],
        grid=(n_batch, n_steps),
        in_specs=[rows, rows, rows_b, rows_b, _const_spec(tab_rows.shape)],
        out_specs=[pl.BlockSpec((RET_CH, W // 2, LANES), lambda b, j: (fwd(b, j), 0, 0)),
                   pl.BlockSpec((RET_CH, W // 2, LANES), lambda b, j: (bwd(b, j), 0, 0))],
        scratch_shapes=[pltpu.VMEM((W // 2, LANES), F32), pltpu.VMEM((W // 2, LANES), F32)],
        compiler_params=_cparams("arbitrary", "arbitrary"),
        name="retention_states",
    )(rk, rv, rk, rv, tab_rows)


def _ret_out_kernel(q_ref, k_ref, v_ref, g_ref, sf_ref, sb_ref, tt_ref, rows_ref, o_ref):
    T = R_CHUNK
    lo = _lo_head()

    def state_block(s):
        s = s.astype(F32)
        return jnp.concatenate([jnp.where(lo, s, 0.0), jnp.where(lo, 0.0, s)], axis=0).astype(BF16)

    for c in range(RET_OUT_CH):
        r = slice(c * T, (c + 1) * T)
        for p in range(R_HEADS // 2):
            ls = slice(p * LANES, (p + 1) * LANES)
            q = q_ref[r, ls]
            qf = q.astype(F32)
            q_stack = jnp.concatenate([jnp.where(lo, qf, 0.0), jnp.where(lo, 0.0, qf)], axis=0).astype(BF16)
            sc = lax.dot_general(q_stack, k_ref[r, ls], (((1,), (1,)), ((), ())), preferred_element_type=F32)
            att = (sc * tt_ref[2 * p * T:(2 * p + 2) * T, :]).astype(BF16)
            oh = jnp.dot(att, v_ref[r, ls], preferred_element_type=F32)
            o = jnp.where(lo, oh[:T], oh[T:])
            rs = slice(p * R_DIM, (p + 1) * R_DIM)
            states = jnp.concatenate([state_block(sf_ref[c, rs, :]), state_block(sb_ref[c, rs, :])], axis=1)
            oi = jnp.dot(q, states, preferred_element_type=F32)
            o = (o + oi[:, :LANES] * rows_ref[_ROW_DQ_F:_ROW_DQ_F + T, ls]
                 + oi[:, LANES:] * rows_ref[_ROW_DQ_B:_ROW_DQ_B + T, ls])
            o2 = o * o
            ms = jnp.where(lo, jnp.sum(jnp.where(lo, o2, 0.0), axis=-1, keepdims=True),
                           jnp.sum(jnp.where(lo, 0.0, o2), axis=-1, keepdims=True)) * (1.0 / R_DIM)
            g = g_ref[r, ls]
            o_ref[r, ls] = (o * lax.rsqrt(ms + EPS) * (g * jax.nn.sigmoid(g))).astype(BF16)


def _retention_out(rq, rk, rv, rg, sf, sb, tab_tt, tab_rows, *, n_rows):
    T = R_CHUNK
    W = R_HEADS * R_DIM
    rows = pl.BlockSpec((RET_OUT_CH * T, W), lambda c: (c, 0))
    st = pl.BlockSpec((RET_OUT_CH, W // 2, LANES), lambda c: (c, 0, 0))
    return pl.pallas_call(
        _ret_out_kernel,
        out_shape=jax.ShapeDtypeStruct((n_rows, W), BF16),
        grid=(n_rows // (RET_OUT_CH * T),),
        in_specs=[rows, rows, rows, rows, st, st, _const_spec(tab_tt.shape), _const_spec(tab_rows.shape)],
        out_specs=rows,
        compiler_params=_cparams("arbitrary"),
        name="retention_out",
    )(rq, rk, rv, rg, sf, sb, tab_tt, tab_rows)


def _hy_in_kernel(x_ref, xp_ref, xn_ref, mod_ref, gpre_ref, w_ref, bin_ref, wsh_ref, bsh_ref,
                  x0_ref, u_ref, *, tm, n_lat, tiles_per_seq, tiles_per_ctx):
    D = x_ref.shape[1]
    H = SUBLANES
    t = pl.program_id(0)
    lat_tiles = n_lat // tm
    pos = jnp.where(t < lat_tiles, t % tiles_per_seq, (t - lat_tiles) % tiles_per_ctx)
    n_pos = jnp.where(t < lat_tiles, tiles_per_seq, tiles_per_ctx)
    xs = jnp.concatenate([xp_ref[...], x_ref[...], xn_ref[...]], axis=0)
    hb = _adaln_in(xs, mod_ref, gpre_ref, 1).astype(BF16)
    keep_prev = jnp.where(pos == 0, 0.0, 1.0)
    keep_next = jnp.where(pos == n_pos - 1, 0.0, 1.0)
    n_ext = tm + 2 * H

    def conv(c):
        cols = slice(c * D, (c + 1) * D)
        z = jnp.dot(hb, w_ref[:, cols], preferred_element_type=F32) + bin_ref[:, cols]
        z = jnp.concatenate([z[0:H] * keep_prev, z[H:H + tm], z[H + tm:] * keep_next], axis=0)
        zm = pltpu.roll(z, 1, 0)[H:H + tm]
        zp = pltpu.roll(z, n_ext - 1, 0)[H:H + tm]
        return (zm * wsh_ref[0:1, cols] + z[H:H + tm] * wsh_ref[1:2, cols] + zp * wsh_ref[2:3, cols]
                + bsh_ref[:, cols])

    x0_ref[...] = conv(0).astype(BF16)
    u_ref[...] = (conv(2) * conv(1)).astype(BF16)


def _hy_in(s, mod, gpre, w_in, b_in, w_sh, b_sh, *, layer, n_rows, geom, ctx_len):
    n_lat, seq_len, _ = geom
    D = s.shape[1]
    tm = HY_TM
    H = SUBLANES
    blocks_per_tile = tm // H
    last_block = s.shape[0] // H - 1
    kern = partial(_hy_in_kernel, tm=tm, n_lat=n_lat, tiles_per_seq=seq_len // tm,
                   tiles_per_ctx=max(ctx_len // tm, 1))
    out = jax.ShapeDtypeStruct((n_rows, D), BF16)
    return pl.pallas_call(
        kern,
        out_shape=[out, out],
        grid=(n_rows // tm,),
        in_specs=[
            pl.BlockSpec((tm, D), lambda t: (t, 0)),
            pl.BlockSpec((H, D), lambda t: (jnp.maximum(t * blocks_per_tile - 1, 0), 0)),
            pl.BlockSpec((H, D), lambda t: (jnp.minimum((t + 1) * blocks_per_tile, last_block), 0)),
            pl.BlockSpec((1, 1, N_MOD, D), _mod_spec(layer, tm, *geom)),
            pl.BlockSpec((1, 1, D), lambda t: (3 * layer + 1, 0, 0), pipeline_mode=pl.Buffered(1)),
            _const_spec((D, 3 * D)),
            _const_spec((1, 3 * D)),
            _const_spec((3, 3 * D)),
            _const_spec((1, 3 * D)),
        ],
        out_specs=[pl.BlockSpec((tm, D), lambda t: (t, 0))] * 2,
        compiler_params=_cparams("arbitrary"),
        name="hyena_in",
    )(s, s, s, mod, gpre, w_in, b_in, w_sh, b_sh)


def _hy_filter_kernel(z_ref, f0_ref, fb0_ref, f1_ref, fb1_ref, f2_ref, fb2_ref, f3_ref, fr_ref, dl_ref,
                      kfb_ref):
    D = dl_ref.shape[1]
    hp = lax.Precision.HIGHEST
    z = z_ref[...]
    fr = fr_ref[...]
    a = jnp.sin(fr * (jnp.dot(z, f0_ref[...], precision=hp, preferred_element_type=F32) + fb0_ref[...]))
    a = jnp.sin(fr * (jnp.dot(a, f1_ref[...], precision=hp, preferred_element_type=F32) + fb1_ref[...]))
    a = jnp.sin(fr * (jnp.dot(a, f2_ref[...], precision=hp, preferred_element_type=F32) + fb2_ref[...]))
    k = jnp.dot(a.astype(BF16), f3_ref[...].astype(BF16), preferred_element_type=F32)
    decay = jnp.exp(-z[:, 0:1] * dl_ref[...])
    k_f = k[:, :D] * decay
    k_b = k[:, D:] * decay
    row = lax.broadcasted_iota(jnp.int32, (z.shape[0], 1), 0) + pl.program_id(0) * z.shape[0]
    k_b = jnp.where(row == 0, 0.0, k_b)
    kfb_ref[0] = k_f.astype(BF16)
    kfb_ref[1] = k_b.astype(BF16)


def _hy_filter(L, f0, fb0, f1, fb1, f2, fb2, f3, freq, D):
    t = np.linspace(0.0, 1.0, L)[:, None]
    bands = (HY_EMB - 1) // 2
    w = 2.0 * math.pi * np.arange(L)[:, None] / L
    f = np.linspace(1e-4, bands - 1, bands)[None]
    z = jnp.asarray(np.concatenate([t, np.cos(f * w), -np.sin(f * w), np.zeros((L, HY_EMB_PAD - HY_EMB))], -1),
                    dtype=F32)
    f0p = jnp.concatenate([f0, jnp.zeros((HY_EMB_PAD - HY_EMB, f0.shape[1]), F32)], 0)
    deltas = jnp.asarray(np.abs(np.linspace(HY_MIN_DECAY, HY_MAX_DECAY, D))[None], dtype=F32)
    tl = min(512, L)
    O = f0.shape[1]
    return pl.pallas_call(
        _hy_filter_kernel,
        out_shape=jax.ShapeDtypeStruct((2, L, D), BF16),
        grid=(L // tl,),
        in_specs=[pl.BlockSpec((tl, HY_EMB_PAD), lambda i: (i, 0)),
                  _const_spec((HY_EMB_PAD, O)), _const_spec((1, O)),
                  _const_spec((O, O)), _const_spec((1, O)),
                  _const_spec((O, O)), _const_spec((1, O)),
                  _const_spec((O, 2 * D)), _const_spec((1, O)), _const_spec((1, D))],
        out_specs=pl.BlockSpec((2, tl, D), lambda i: (0, i, 0)),
        compiler_params=_cparams("arbitrary"),
        name="hyena_filter",
    )(z, f0p, fb0[None], f1, fb1[None], f2, fb2[None], f3, freq[None], deltas)


def _dft_matrices(L, tf):
    k2 = 2 * np.arange(L, dtype=np.int64)[:, None] + 1
    ang = ((k2 * np.arange(L, dtype=np.int64)[None, :]) % (4 * L)) * (2.0 * math.pi / (4 * L))
    cos, nsin = np.cos(ang), -np.sin(ang)
    fwd = np.concatenate([cos.reshape(L // tf, tf, L), nsin.reshape(L // tf, tf, L)], axis=1).reshape(2 * L, L)
    inv = fwd.T * (1.0 / L)
    return jnp.asarray(fwd, dtype=F32).astype(BF16), jnp.asarray(inv, dtype=F32).astype(BF16)


def _spectrum_kernel(w_ref, kfb_ref, o_ref):
    tf = w_ref.shape[0] // 2
    k_f = kfb_ref[0].astype(F32)
    k_b = kfb_ref[1].astype(F32)
    o_ref[0:tf, :] = jnp.dot(w_ref[0:tf, :], (k_f + k_b).astype(BF16), preferred_element_type=F32)
    o_ref[tf:, :] = jnp.dot(w_ref[tf:, :], (k_f - k_b).astype(BF16), preferred_element_type=F32)


def _filter_spectrum(fwd, kfb, tf):
    _, L, D = kfb.shape
    return pl.pallas_call(
        _spectrum_kernel,
        out_shape=jax.ShapeDtypeStruct((2 * L, D), F32),
        grid=(L // tf,),
        in_specs=[pl.BlockSpec((2 * tf, L), lambda j: (j, 0)), _const_spec((2, L, D))],
        out_specs=pl.BlockSpec((2 * tf, D), lambda j: (j, 0)),
        compiler_params=_cparams("arbitrary"),
        name="filter_spectrum",
    )(fwd, kfb)


def _dft_fwd_kernel(w_ref, u_ref, kh_ref, o_ref):
    tf = w_ref.shape[0] // 2
    uh = jnp.dot(w_ref[...], u_ref[...], preferred_element_type=F32)
    ur, ui = uh[:tf], uh[tf:]
    kr, ki = kh_ref[0:tf, :], kh_ref[tf:, :]
    o_ref[0:tf, :] = (ur * kr - ui * ki).astype(BF16)
    o_ref[tf:, :] = (ur * ki + ui * kr).astype(BF16)


def _dft_forward(fwd, u, khat, *, n_batch, L, row_off, tf):
    D = u.shape[1]
    seq0 = row_off // L
    return pl.pallas_call(
        _dft_fwd_kernel,
        out_shape=jax.ShapeDtypeStruct((n_batch * 2 * L, D), BF16),
        grid=(n_batch, L // tf),
        in_specs=[pl.BlockSpec((2 * tf, L), lambda b, j: (j, 0)),
                  pl.BlockSpec((L, D), lambda b, j: (seq0 + b, 0)),
                  pl.BlockSpec((2 * tf, D), lambda b, j: (j, 0))],
        out_specs=pl.BlockSpec((2 * tf, D), lambda b, j: (b * (L // tf) + j, 0)),
        compiler_params=_cparams("arbitrary", "arbitrary"),
        name="dft_forward",
    )(fwd, u, khat)


def _dft_inv_kernel(w_ref, y_ref, u_ref, x0_ref, bias_ref, o_ref, acc):
    kk = pl.program_id(2)

    @pl.when(kk == 0)
    def _():
        acc[...] = jnp.zeros_like(acc)

    acc[...] += jnp.dot(w_ref[...], y_ref[...], preferred_element_type=F32)

    @pl.when(kk == pl.num_programs(2) - 1)
    def _():
        y = acc[...] + u_ref[...].astype(F32) * bias_ref[...]
        o_ref[...] = (x0_ref[...].astype(F32) * y).astype(BF16)


def _dft_inverse(inv, yhat, u, x0, bias, *, n_batch, L, row_off):
    D = u.shape[1]
    tm = min(1024, L)
    tk = min(2048, 2 * L)
    m_tiles = L // tm
    k_tiles = 2 * L // tk
    blk0 = row_off // tm
    rows_in = pl.BlockSpec((tm, D), lambda b, i, kk: (blk0 + b * m_tiles + i, 0))
    return pl.pallas_call(
        _dft_inv_kernel,
        out_shape=jax.ShapeDtypeStruct((n_batch * L, D), BF16),
        grid=(n_batch, m_tiles, k_tiles),
        in_specs=[pl.BlockSpec((tm, tk), lambda b, i, kk: (i, kk)),
                  pl.BlockSpec((tk, D), lambda b, i, kk: (b * k_tiles + kk, 0)),
                  rows_in, rows_in, _const_spec((1, D))],
        out_specs=pl.BlockSpec((tm, D), lambda b, i, kk: (b * m_tiles + i, 0)),
        scratch_shapes=[pltpu.VMEM((tm, D), F32)],
        compiler_params=_cparams("arbitrary", "arbitrary", "arbitrary"),
        name="dft_inverse",
    )(inv, yhat, u, x0, bias)


FFT_N2 = 256
FFT_J = 16
FFT_JB = 64


def _fft_expand_kernel(t_ref, tt_ref, ma_ref, mi_ref):
    J = FFT_J
    R, H1 = t_ref.shape[1], t_ref.shape[2]
    W = H1 * J
    Rp = tt_ref.shape[2]
    hp = lax.Precision.HIGHEST
    col = lax.broadcasted_iota(jnp.int32, (1, W), 1)
    spread = (lax.broadcasted_iota(jnp.int32, (H1, 1), 0) == col // J).astype(F32)
    row_j = lax.broadcasted_iota(jnp.int32, (R, 1), 0) % J
    ma = jnp.dot(t_ref[0], spread, precision=hp, preferred_element_type=F32)
    ma_ref[0] = jnp.where(row_j == col % J, ma, 0.0).astype(BF16)
    rowi = lax.broadcasted_iota(jnp.int32, (W, 1), 0)
    spread_t = (rowi // J == lax.broadcasted_iota(jnp.int32, (1, H1), 1)).astype(F32)
    col_j = lax.broadcasted_iota(jnp.int32, (1, Rp), 1) % J
    mi = jnp.dot(spread_t, tt_ref[0], precision=hp, preferred_element_type=F32)
    mi_ref[0] = jnp.where(rowi % J == col_j, mi, 0.0).astype(BF16)


def _fft_matrices(L):
    N = 2 * L
    N1 = N // FFT_N2
    H1, K1, J, A = N1 // 2, N1 // 2 + 1, FFT_J, FFT_N2 // FFT_J
    R = K1 * 2 * J
    Rp = R + (-R) % LANES
    a = np.arange(A, dtype=np.int64)[:, None, None, None]
    k1 = np.arange(K1, dtype=np.int64)[None, :, None, None]
    j = np.arange(J, dtype=np.int64)[None, None, :, None]
    n1 = np.arange(H1, dtype=np.int64)[None, None, None, :]
    ang = ((k1 * (FFT_N2 * n1 + J * a + j)) % N) * (2.0 * math.pi / N)
    base = np.stack([np.cos(ang), -np.sin(ang)], axis=2)
    kk = np.arange(K1)
    w = np.where((kk == 0) | (kk == N1 // 2), 1.0, 2.0) / N
    t = jnp.asarray(base.reshape(A, R, H1), dtype=F32)
    tt = np.transpose((base * w[None, :, None, None, None]).reshape(A, R, H1), (0, 2, 1))
    tt = jnp.asarray(np.pad(tt, ((0, 0), (0, 0), (0, Rp - R))), dtype=F32)
    ma, mi = pl.pallas_call(
        _fft_expand_kernel,
        out_shape=[jax.ShapeDtypeStruct((A, R, H1 * J), BF16), jax.ShapeDtypeStruct((A, H1 * J, Rp), BF16)],
        grid=(A,),
        in_specs=[pl.BlockSpec((1, R, H1), lambda i: (i, 0, 0)), pl.BlockSpec((1, H1, Rp), lambda i: (i, 0, 0))],
        out_specs=[pl.BlockSpec((1, R, H1 * J), lambda i: (i, 0, 0)),
                   pl.BlockSpec((1, H1 * J, Rp), lambda i: (i, 0, 0))],
        compiler_params=_cparams("arbitrary"),
        name="fft_matrices",
    )(t, tt)
    n2 = np.arange(FFT_N2, dtype=np.int64)
    angf = ((n2[:, None] * n2[None, :]) % FFT_N2) * (2.0 * math.pi / FFT_N2)
    fr, fi = np.cos(angf), -np.sin(angf)
    cat = lambda top, bot: jnp.asarray(np.concatenate([top, bot], axis=0), dtype=F32).astype(BF16)
    return dict(ma=ma, mi=mi, k1=K1, h1=H1,
                mre=cat(fr, fi), mim=cat(-fi, fr), gre=cat(fr, -fi), gim=cat(fi, fr))


def _fft_a_kernel(u_ref, ma_ref, z_ref):
    H1, JB, D = u_ref.shape
    K1 = z_ref.shape[1]
    J = FFT_J
    a = pl.program_id(1)
    for q in range(JB // J):
        rows = slice(q * J, (q + 1) * J)
        data = u_ref[:, rows, :].reshape(H1 * J, D)
        res = jnp.dot(ma_ref[a * (JB // J) + q], data, preferred_element_type=F32)
        z_ref[0, :, :, rows, :] = res.astype(BF16).reshape(K1, 2, J, D)


def _fft_a(u3, mats, *, n_batch):
    D = u3.shape[2]
    K1, H1 = mats["k1"], mats["h1"]
    return pl.pallas_call(
        _fft_a_kernel,
        out_shape=jax.ShapeDtypeStruct((n_batch, K1, 2, FFT_N2, D), BF16),
        grid=(n_batch, FFT_N2 // FFT_JB),
        in_specs=[pl.BlockSpec((H1, FFT_JB, D), lambda b, a: (b, a, 0)), _const_spec(mats["ma"].shape)],
        out_specs=pl.BlockSpec((1, K1, 2, FFT_JB, D), lambda b, a: (b, 0, 0, a, 0)),
        compiler_params=_cparams("arbitrary", "arbitrary"),
        name="fft_stage_a",
    )(u3, mats["ma"])


def _dft256(mre_ref, mim_ref, z_ref, idx):
    n2 = FFT_N2
    x = (jnp.dot(mre_ref[...], z_ref[idx + (0,)], preferred_element_type=F32)
         + jnp.dot(mim_ref[...], z_ref[idx + (1,)], preferred_element_type=F32))
    return x[:n2], x[n2:]


def _fft_spec_kernel(zf_ref, zb_ref, mre_ref, mim_ref, o_ref):
    fr, fi = _dft256(mre_ref, mim_ref, zf_ref, (0, 0))
    br, bi = _dft256(mre_ref, mim_ref, zb_ref, (0, 0))
    o_ref[0, 0] = fr + br
    o_ref[0, 1] = fi - bi


def _fft_spectrum(zfilt, mats):
    _, K1, _, n2, D = zfilt.shape
    blk = (1, 1, 2, n2, D)
    return pl.pallas_call(
        _fft_spec_kernel,
        out_shape=jax.ShapeDtypeStruct((K1, 2, n2, D), F32),
        grid=(K1,),
        in_specs=[pl.BlockSpec(blk, lambda k: (0, k, 0, 0, 0)), pl.BlockSpec(blk, lambda k: (1, k, 0, 0, 0)),
                  _const_spec(mats["mre"].shape), _const_spec(mats["mim"].shape)],
        out_specs=pl.BlockSpec((1, 2, n2, D), lambda k: (k, 0, 0, 0)),
        compiler_params=_cparams("arbitrary"),
        name="fft_filter_spectrum",
    )(zfilt, zfilt, mats["mre"], mats["mim"])


def _fft_mid_kernel(z_ref, kh_ref, mre_ref, mim_ref, gre_ref, gim_ref, o_ref):
    n2 = FFT_N2
    kr, ki = kh_ref[0, 0], kh_ref[0, 1]
    for b in range(z_ref.shape[0]):
        xr, xi = _dft256(mre_ref, mim_ref, z_ref, (b, 0))
        yr = (xr * kr - xi * ki).astype(BF16)
        yi = (xr * ki + xi * kr).astype(BF16)
        zp = (jnp.dot(gre_ref[...], yr, preferred_element_type=F32)
              + jnp.dot(gim_ref[...], yi, preferred_element_type=F32))
        o_ref[b, 0, 0] = zp[:n2].astype(BF16)
        o_ref[b, 0, 1] = zp[n2:].astype(BF16)


def _fft_mid(z, khat, mats):
    n_batch, K1, _, n2, D = z.shape
    nb = 2 if n_batch % 2 == 0 else 1
    blk = pl.BlockSpec((nb, 1, 2, n2, D), lambda k, b: (b, k, 0, 0, 0))
    mat = [_const_spec(mats[m].shape) for m in ("mre", "mim", "gre", "gim")]
    return pl.pallas_call(
        _fft_mid_kernel,
        out_shape=jax.ShapeDtypeStruct(z.shape, BF16),
        grid=(K1, n_batch // nb),
        in_specs=[blk, pl.BlockSpec((1, 2, n2, D), lambda k, b: (k, 0, 0, 0))] + mat,
        out_specs=blk,
        compiler_params=_cparams("arbitrary", "arbitrary"),
        name="fft_stage_b",
    )(z, khat, mats["mre"], mats["mim"], mats["gre"], mats["gim"])


def _fft_a_inv_kernel(zp_ref, mi_ref, u_ref, x0_ref, bias_ref, o_ref):
    H1, JB, D = u_ref.shape
    K1 = zp_ref.shape[1]
    J = FFT_J
    a = pl.program_id(1)
    k_pad = mi_ref.shape[2] - K1 * 2 * J
    for q in range(JB // J):
        rows = slice(q * J, (q + 1) * J)
        zz = zp_ref[0, :, :, rows, :].reshape(K1 * 2 * J, D)
        if k_pad:
            zz = jnp.concatenate([zz, jnp.zeros((k_pad, D), BF16)], axis=0)
        y = jnp.dot(mi_ref[a * (JB // J) + q], zz, preferred_element_type=F32).reshape(H1, J, D)
        y = y + u_ref[:, rows, :].astype(F32) * bias_ref[...]
        o_ref[:, rows, :] = (x0_ref[:, rows, :].astype(F32) * y).astype(BF16)


def _fft_a_inv(zp, mats, u3, x03, bias, *, n_blocks_out):
    n_batch, K1, _, n2, D = zp.shape
    H1 = mats["h1"]
    rows = pl.BlockSpec((H1, FFT_JB, D), lambda b, a: (b, a, 0))
    return pl.pallas_call(
        _fft_a_inv_kernel,
        out_shape=jax.ShapeDtypeStruct((n_blocks_out, n2, D), BF16),
        grid=(n_batch, n2 // FFT_JB),
        in_specs=[pl.BlockSpec((1, K1, 2, FFT_JB, D), lambda b, a: (b, 0, 0, a, 0)),
                  _const_spec(mats["mi"].shape), rows, rows, _const_spec((1, D))],
        out_specs=rows,
        compiler_params=_cparams("arbitrary", "arbitrary"),
        name="fft_stage_a_inv",
    )(zp, mats["mi"], u3, x03, bias)


def kernel(x, c, ctx, c_ctx, w_mod, b_mod, norm_pre, norm_post, ffn_gate, ffn_up, ffn_down,
           mix_w_in, attn_sink, ret_decay, mix_w_out, hy_w_in, hy_b_in, hy_short_w, hy_short_b,
           hy_f0, hy_fb0, hy_f1, hy_fb1, hy_f2, hy_fb2, hy_f3, hy_freq, hy_bias, hy_w_out):
    Bn, L, D = x.shape
    C = ctx.shape[1]
    n_lat = Bn * L
    n_all = n_lat + Bn * C
    geom = (n_lat, L, Bn)
    assert L % PROJ_TM == 0 and L % HY_TM == 0 and n_lat % C == 0 and (Bn * C) % PROJ_TM == 0
    assert L % min(FFN_TM, L) == 0 and (Bn * C) % min(FFN_TM, L) == 0
    assert C % (RET_CH * R_CHUNK) == 0 and L % (RET_CH * R_CHUNK) == 0 and C % HY_TM == 0
    assert L % FFT_N2 == 0 and n_all % FFT_N2 == 0
    last_reader = DEPTH - 1 if (DEPTH - 1) % 2 == 0 else DEPTH - 2

    c_all = jnp.concatenate([c, c_ctx[None], jnp.zeros((SUBLANES - (Bn + 1) % SUBLANES, D), F32)], axis=0)
    mod = _modulation(c_all, w_mod, b_mod).reshape(DEPTH, c_all.shape[0], N_MOD, D)

    rope_a = _rope_tables(_axial_angles(L), PROJ_TM)
    rope_r = _rope_tables(_line_angles(L), PROJ_TM)
    fft = _fft_matrices(L)

    npre = norm_pre.reshape(DEPTH * 3, 1, D)
    npost = norm_post.reshape(DEPTH * 3, 1, D)
    wg, wu, wd = ffn_gate.astype(BF16), ffn_up.astype(BF16), ffn_down.astype(BF16)
    mix_wo, hy_wo = mix_w_out.astype(BF16), hy_w_out.astype(BF16)

    def ffn(s, l, i, j, n_rows, mix=None):
        return _ffn(s, mod, npre, npost, wg, wu, wd, layer=l, i=i, j=j, n_rows=n_rows, geom=geom, mix=mix)

    s = (x.reshape(n_lat, D), ctx.reshape(Bn * C, D))
    for l in range(DEPTH):
        ctx_live = l <= last_reader
        ctx_full = l < last_reader
        n_in = n_all if ctx_live else n_lat
        n_out = n_all if ctx_full else n_lat
        s = ffn(s, l, 0, 0, n_in)
        if l % 2 == 0:
            e = l // 2
            aq, rq, rg, akd, avd, rk, rv = _even_in(s, mod, npre, _pack_even_w_in(mix_w_in[e]), rope_a, rope_r,
                                                    layer=l, n_rows=n_in, geom=geom)
            sink_lanes = jnp.broadcast_to(attn_sink[e][:, None], (A_Q_HEADS, LANES))
            dec_lanes = jnp.repeat(ret_decay[e], R_DIM, axis=1)
            a = _attention(aq, akd, avd, sink_lanes, n_batch=Bn, seq_len=L, ctx_len=C,
                           with_ctx_queries=ctx_full)
            tab_tt, tab_rows = _retention_tables(dec_lanes)
            sf, sb = _retention_states(rk, rv, tab_rows, n_batch=Bn, seq_len=L, ctx_len=C)
            r = _retention_out(rq, rk, rv, rg, sf, sb, tab_tt, tab_rows, n_rows=n_out)
            mix = ([a, r], mix_wo, e)
        else:
            o = l // 2
            x0, u = _hy_in(s, mod, npre, hy_w_in[o].astype(BF16), hy_b_in[o][None], hy_short_w[o],
                           hy_short_b[o][None], layer=l, n_rows=n_in, geom=geom, ctx_len=C)
            filt = (hy_f0[o], hy_fb0[o], hy_f1[o], hy_fb1[o], hy_f2[o], hy_fb2[o], hy_f3[o], hy_freq[o])
            bias = hy_bias[o][None]
            blocks = lambda arr: arr.reshape(arr.shape[0] // FFT_N2, FFT_N2, D)
            kfb = _hy_filter(L, *filt, D)
            khat = _fft_spectrum(_fft_a(blocks(kfb.reshape(2 * L, D)), fft, n_batch=2), fft)
            zp = _fft_mid(_fft_a(blocks(u), fft, n_batch=Bn), khat, fft)
            yg = _fft_a_inv(zp, fft, blocks(u), blocks(x0), bias, n_blocks_out=n_lat // FFT_N2).reshape(n_lat, D)
            if ctx_full:
                tf = min(DFT_TF, C)
                fwd, inv = _dft_matrices(C, tf)
                khat_c = _filter_spectrum(fwd, _hy_filter(C, *filt, D), tf)
                yhat = _dft_forward(fwd, u, khat_c, n_batch=Bn, L=C, row_off=n_lat, tf=tf)
                yg = (yg, _dft_inverse(inv, yhat, u, x0, bias, n_batch=Bn, L=C, row_off=n_lat))
            mix = ([yg], hy_wo, o)
        s = ffn(s, l, 2, 1, n_out, mix)
    return s[:n_lat].reshape(Bn, L, D)
```

```python
import math
from functools import partial

import jax
import jax.numpy as jnp
import numpy as np
from jax import lax
from jax.experimental import pallas as pl
from jax.experimental.pallas import tpu as pltpu

F32 = jnp.float32
BF16 = jnp.bfloat16

DEPTH = 4
GRID_W = 64
EPS = 1e-6
NEG = -1e30
N_MOD = 9
FFN_RESIDUAL = 0.5
HEAD_DIM = 64
A_Q_HEADS = 8
A_KV_HEADS = 2
WINDOW = 128
BLOCK = 128
ROPE_BASE = 10000.0
R_DIM = 64
R_HEADS = 8
R_CHUNK = 128
HY_EMB = 33
HY_EMB_PAD = 64
HY_MAX_DECAY = math.log(1e-2) / 0.3
HY_MIN_DECAY = math.log(1e-2) / 1.5

LANES = 128
SUBLANES = 8
V7X_VMEM_LIMIT_BYTES = 56 * 1024 * 1024

FFN_TM = 1024
FFN_SUB = 2
FFN_TF = 256
PROJ_TM = 1024
HY_TM = 256
MOD_TN = 2304
DFT_TF = 256
RET_CH = 2
RET_OUT_CH = 4
ATT_RC = 256
ATT_QB = 2


def _cparams(*sem):
    return pltpu.CompilerParams(dimension_semantics=sem, vmem_limit_bytes=V7X_VMEM_LIMIT_BYTES)


def _const_spec(shape):
    zeros = (0,) * len(shape)
    return pl.BlockSpec(shape, lambda *_: zeros, pipeline_mode=pl.Buffered(1))


def _mod_spec(layer, tm, n_lat, seq_len, n_batch):
    def index(t, *_):
        return (layer, jnp.where(t * tm < n_lat, (t * tm) // seq_len, n_batch), 0, 0)
    return index


def _adaln_in(x, mod_ref, gpre_ref, i):
    shift = mod_ref[0, 0, 3 * i:3 * i + 1, :]
    scale = mod_ref[0, 0, 3 * i + 1:3 * i + 2, :]
    gain = gpre_ref[0] * (1.0 + scale)
    inv = lax.rsqrt(jnp.mean(x * x, axis=-1, keepdims=True) + EPS)
    return (x * inv) * gain + shift


def _adaln_out(x, y, mod_ref, gpost_ref, i, w):
    gate = mod_ref[0, 0, 3 * i + 2:3 * i + 3, :]
    gain = (w * gate) * gpost_ref[0]
    inv = lax.rsqrt(jnp.mean(y * y, axis=-1, keepdims=True) + EPS)
    return x + (y * inv) * gain


def _mod_kernel(c_ref, w_ref, b_ref, o_ref):
    c = c_ref[...]
    a = (c * jax.nn.sigmoid(c)).astype(BF16)
    o_ref[0] = jnp.dot(a, w_ref[0].astype(BF16), preferred_element_type=F32) + b_ref[0]


def _modulation(c_all, w_mod, b_mod):
    depth, D, W = w_mod.shape
    rows = c_all.shape[0]
    return pl.pallas_call(
        _mod_kernel,
        out_shape=jax.ShapeDtypeStruct((depth, rows, W), F32),
        grid=(depth, W // MOD_TN),
        in_specs=[
            pl.BlockSpec((rows, D), lambda l, j: (0, 0)),
            pl.BlockSpec((1, D, MOD_TN), lambda l, j: (l, 0, j)),
            pl.BlockSpec((1, 1, MOD_TN), lambda l, j: (l, 0, j)),
        ],
        out_specs=pl.BlockSpec((1, rows, MOD_TN), lambda l, j: (l, 0, j)),
        compiler_params=_cparams("arbitrary", "arbitrary"),
        name="modulation",
    )(c_all, w_mod, b_mod.reshape(depth, 1, W))


def _ffn_kernel(*refs, i, tf, n_chunks, n_parts, n_sub, split, lat_tiles):
    row_refs, k = [], 0
    for is_split in split:
        row_refs.append(refs[k:k + 2] if is_split else refs[k:k + 1])
        k += 2 if is_split else 1
    mod_ref, npre_ref, npost_ref = refs[k:k + 3]
    rest = refs[k + 3:]
    if n_parts:
        nmix_ref, wo_ref = rest[:2]
        rest = rest[2:]
    wg_ref, wu_ref, wd_ref, o_ref = rest
    is_ctx = (lax.broadcasted_iota(jnp.int32, (1, 1), 0) + pl.program_id(0)) >= lat_tiles

    def read(src, rows):
        if len(src) == 1:
            return src[0][rows, :]
        return jnp.where(is_ctx, src[1][rows, :], src[0][rows, :])

    sub = o_ref.shape[0] // n_sub
    for b in range(n_sub):
        rows = slice(b * sub, (b + 1) * sub)
        x = read(row_refs[-1], rows)
        if n_parts:
            y = None
            row = 0
            for src in row_refs[:-1]:
                width = src[0].shape[1]
                part = jnp.dot(read(src, rows), wo_ref[0, row:row + width, :], preferred_element_type=F32)
                y = part if y is None else y + part
                row += width
            x = _adaln_out(x, y, mod_ref, nmix_ref, 1, 1.0)
        hb = _adaln_in(x, mod_ref, npre_ref, i).astype(BF16)
        acts = []
        for j in range(n_chunks):
            cols = slice(j * tf, (j + 1) * tf)
            g = jnp.dot(hb, wg_ref[0, 0, :, cols], preferred_element_type=F32)
            u = jnp.dot(hb, wu_ref[0, 0, :, cols], preferred_element_type=F32)
            acts.append((g * jax.nn.sigmoid(g) * u).astype(BF16))
        y = jnp.dot(jnp.concatenate(acts, axis=1), wd_ref[0, 0], preferred_element_type=F32)
        o_ref[rows, :] = _adaln_out(x, y, mod_ref, npost_ref, i, FFN_RESIDUAL)


def _ffn(s, mod, npre, npost, wg, wu, wd, *, layer, i, j, n_rows, geom, mix=None):
    n_lat = geom[0]
    D = wd.shape[3]
    d_ff = wd.shape[2]
    tm = min(FFN_TM, geom[1])
    lat_tiles = n_lat // tm
    rows = lambda t: (t, 0)
    norm = lambda sub: pl.BlockSpec((1, 1, D), lambda t: (3 * layer + sub, 0, 0), pipeline_mode=pl.Buffered(1))
    weight = lambda shape: pl.BlockSpec((1, 1) + shape, lambda t: (layer, j, 0, 0), pipeline_mode=pl.Buffered(1))
    parts, extra_specs, extra_args = [], [], []
    if mix is not None:
        parts, w_out, e = mix
        extra_specs = [norm(1), pl.BlockSpec((1,) + w_out.shape[1:], lambda t: (e, 0, 0),
                                             pipeline_mode=pl.Buffered(1))]
        extra_args = [npost, w_out]
    row_specs, row_args, split = [], [], []
    for src in list(parts) + [s]:
        if isinstance(src, tuple):
            lat, ctx = src
            row_specs += [pl.BlockSpec((tm, lat.shape[1]), lambda t: (jnp.minimum(t, lat_tiles - 1), 0)),
                          pl.BlockSpec((tm, ctx.shape[1]), lambda t: (jnp.maximum(t - lat_tiles, 0), 0))]
            row_args += [lat, ctx]
        else:
            row_specs.append(pl.BlockSpec((tm, src.shape[1]), rows))
            row_args.append(src)
        split.append(isinstance(src, tuple))
    return pl.pallas_call(
        partial(_ffn_kernel, i=i, tf=FFN_TF, n_chunks=d_ff // FFN_TF, n_parts=len(parts), n_sub=FFN_SUB,
                split=tuple(split), lat_tiles=lat_tiles),
        out_shape=jax.ShapeDtypeStruct((n_rows, D), F32),
        grid=(n_rows // tm,),
        in_specs=row_specs + [pl.BlockSpec((1, 1, N_MOD, D), _mod_spec(layer, tm, *geom)), norm(i), norm(i)]
        + extra_specs + [weight((D, d_ff)), weight((D, d_ff)), weight((d_ff, D))],
        out_specs=pl.BlockSpec((tm, D), rows),
        compiler_params=_cparams("arbitrary"),
        name="ffn",
    )(*row_args, mod, npre, npost, *extra_args, wg, wu, wd)


_AQ = (0, 512)
_RQ = (512, 1024)
_RG = (1024, 1536)
_AK = (1536, 1792)
_AV = (1792, 2048)
_RK = (2048, 2560)
_RV = (2560, 3072)
EVEN_W = 3072


def _pack_even_w_in(w_in):
    aq, rq, rg = w_in[:, 0:512], w_in[:, 512:1024], w_in[:, 1024:1536]
    ak, av = w_in[:, 1536:1664], w_in[:, 1664:1792]
    rk, rv = w_in[:, 1792:2304], w_in[:, 2304:2816]

    def dup(a):
        g0, g1 = a[:, :HEAD_DIM], a[:, HEAD_DIM:]
        return jnp.concatenate([g0, g0, g1, g1], axis=1)

    return jnp.concatenate([aq, rq, rg, dup(ak), dup(av), rk, rv], axis=1).astype(BF16)


def _rope_tables(ang, ident_rows):
    L = ang.shape[0]
    cos, sin = jnp.cos(ang), jnp.sin(ang)
    zero = jnp.zeros_like(sin)
    cos_t = jnp.tile(cos, (1, 4))
    sin_a = jnp.tile(jnp.concatenate([-sin, zero], axis=1), (1, 2))
    sin_b = jnp.tile(jnp.concatenate([zero, sin], axis=1), (1, 2))
    pad_one = jnp.ones((ident_rows, LANES), F32)
    pad_zero = jnp.zeros((ident_rows, LANES), F32)
    return (jnp.concatenate([cos_t, pad_one], 0), jnp.concatenate([sin_a, pad_zero], 0),
            jnp.concatenate([sin_b, pad_zero], 0))


def _axial_angles(L):
    n_rows = L // GRID_W
    row = jnp.repeat(jnp.arange(n_rows, dtype=F32), GRID_W)
    col = jnp.tile(jnp.arange(GRID_W, dtype=F32), n_rows)
    nf = HEAD_DIM // 4
    inv = ROPE_BASE ** (-jnp.arange(nf, dtype=F32) / nf)
    return jnp.concatenate([row[:, None] * inv, col[:, None] * inv], -1)


def _line_angles(L):
    inv = ROPE_BASE ** (-jnp.linspace(0.0, 1.0, R_DIM // 2, dtype=F32))
    return jnp.arange(L, dtype=F32)[:, None] * inv


def _rope(z, cos, sin_a, sin_b):
    outs = []
    for c in range(z.shape[1] // LANES):
        zc = z[:, c * LANES:(c + 1) * LANES]
        outs.append(zc * cos + pltpu.roll(zc, 96, 1) * sin_a + pltpu.roll(zc, 32, 1) * sin_b)
    return outs[0] if len(outs) == 1 else jnp.concatenate(outs, axis=1)


def _even_in_kernel(x_ref, mod_ref, gpre_ref, w_ref, ca_ref, saa_ref, sab_ref, cr_ref, sra_ref, srb_ref,
                    aq_ref, rq_ref, rg_ref, ak_ref, av_ref, rk_ref, rv_ref):
    hb = _adaln_in(x_ref[...], mod_ref, gpre_ref, 1).astype(BF16)

    def proj(cols):
        return jnp.dot(hb, w_ref[:, cols[0]:cols[1]], preferred_element_type=F32)

    rope_a = (ca_ref[...], saa_ref[...], sab_ref[...])
    rope_r = (cr_ref[...], sra_ref[...], srb_ref[...])
    aq_ref[...] = (_rope(proj(_AQ), *rope_a) * HEAD_DIM ** -0.5).astype(BF16)
    rq_ref[...] = _rope(proj(_RQ), *rope_r).astype(BF16)
    rg_ref[...] = proj(_RG)
    ak_ref[...] = _rope(proj(_AK), *rope_a).astype(BF16)
    av_ref[...] = proj(_AV).astype(BF16)
    rk_ref[...] = (_rope(proj(_RK), *rope_r) * R_DIM ** -0.5).astype(BF16)
    rv_ref[...] = proj(_RV).astype(BF16)


def _even_in(s, mod, gpre, w_pack, rope_a, rope_r, *, layer, n_rows, geom):
    n_lat, seq_len, _ = geom
    D = s.shape[1]
    tm = min(PROJ_TM, seq_len)
    tiles_per_seq = seq_len // tm

    def rope_index(t):
        return (jnp.where(t * tm < n_lat, t % tiles_per_seq, tiles_per_seq), 0)

    rope_spec = pl.BlockSpec((tm, LANES), rope_index)

    def out(width, dtype):
        return jax.ShapeDtypeStruct((n_rows, width), dtype), pl.BlockSpec((tm, width), lambda t: (t, 0))

    outs = [out(512, BF16), out(512, BF16), out(512, F32), out(256, BF16), out(256, BF16),
            out(512, BF16), out(512, BF16)]
    return pl.pallas_call(
        _even_in_kernel,
        out_shape=[o[0] for o in outs],
        grid=(n_rows // tm,),
        in_specs=[
            pl.BlockSpec((tm, D), lambda t: (t, 0)),
            pl.BlockSpec((1, 1, N_MOD, D), _mod_spec(layer, tm, *geom)),
            pl.BlockSpec((1, 1, D), lambda t: (3 * layer + 1, 0, 0), pipeline_mode=pl.Buffered(1)),
            _const_spec((D, EVEN_W)),
        ] + [rope_spec] * 6,
        out_specs=[o[1] for o in outs],
        compiler_params=_cparams("arbitrary"),
        name="even_in",
    )(s, mod, gpre, w_pack, *rope_a, *rope_r)


def _attn_kernel(q_ref, k0_ref, k1_ref, k2_ref, k3_ref, kx_ref, v0_ref, v1_ref, v2_ref, v3_ref, vx_ref,
                 sink_ref, band_ref, o_ref, *, nb):
    T = BLOCK
    j = pl.program_id(1)
    kj = lax.broadcasted_iota(jnp.int32, (1, band_ref.shape[1]), 1)
    lo = lax.broadcasted_iota(jnp.int32, (1, LANES), 1) < HEAD_DIM
    k_refs = (k0_ref, k1_ref, k2_ref, k3_ref)
    v_refs = (v0_ref, v1_ref, v2_ref, v3_ref)
    def scores(i, g):
        n = ATT_QB * j + i
        k_lo = jnp.where(n > 0, 0, T)
        k_hi = jnp.where(n < nb - 1, 3 * T, jnp.where(n < nb, 2 * T, 0))
        exists = ((kj >= k_lo) & (kj < k_hi)) | (kj >= 3 * T)
        bias = band_ref[...] + jnp.where(exists, 0.0, NEG)
        gl = slice(g * LANES, (g + 1) * LANES)
        kd = jnp.concatenate([r[:, gl] for r in k_refs[i:i + 3]] + [kx_ref[:, gl]], axis=0)
        vd = jnp.concatenate([r[:, gl] for r in v_refs[i:i + 3]] + [vx_ref[:, gl]], axis=0)
        qs = []
        for c in range(2):
            cl = slice((2 * g + c) * LANES, (2 * g + c + 1) * LANES)
            qc = q_ref[i * T:(i + 1) * T, cl].astype(F32)
            qs.append(jnp.where(lo, qc, 0.0))
            qs.append(jnp.where(lo, 0.0, qc))
        q_stack = jnp.concatenate(qs, axis=0).astype(BF16)
        s_all = lax.dot_general(q_stack, kd, (((1,), (1,)), ((), ())), preferred_element_type=F32)
        return s_all, vd, bias

    def finish(i, g, s_all, vd, bias):
        ps, inv = [], []
        for rc in range(4 * T // ATT_RC):
            rows = slice(rc * ATT_RC, (rc + 1) * ATT_RC)
            heads = range(4 * g + (rc * ATT_RC) // T, 4 * g + ((rc + 1) * ATT_RC - 1) // T + 1)
            rows_per_head = min(ATT_RC, T)
            sink = jnp.concatenate(
                [jnp.broadcast_to(sink_ref[h:h + 1, 0:1], (rows_per_head, 1)) for h in heads], axis=0)
            s = s_all[rows] + bias[rows]
            m = jnp.maximum(jnp.max(s, axis=-1, keepdims=True), sink)
            p = jnp.exp(s - m)
            inv.append(1.0 / (jnp.sum(p, axis=-1, keepdims=True) + jnp.exp(sink - m)))
            ps.append(p.astype(BF16))
        p = jnp.concatenate(ps, axis=0)
        o = jnp.dot(p, vd, preferred_element_type=F32) * jnp.concatenate(inv, axis=0)
        for c in range(2):
            oc = jnp.where(lo, o[(2 * c) * T:(2 * c + 1) * T], o[(2 * c + 1) * T:(2 * c + 2) * T])
            o_ref[i * T:(i + 1) * T, (2 * g + c) * LANES:(2 * g + c + 1) * LANES] = oc.astype(BF16)

    items = [(i, g) for i in range(ATT_QB) for g in range(A_KV_HEADS)]
    ahead = 2
    pending = [scores(*it) for it in items[:ahead]]
    for k, it in enumerate(items):
        finish(*it, *pending[k])
        if k + ahead < len(items):
            pending.append(scores(*items[k + ahead]))


def _attention(aq, akd, avd, sink_lanes, *, n_batch, seq_len, ctx_len, with_ctx_queries):
    T = BLOCK
    QB = ATT_QB
    nb = seq_len // T
    ncb = ctx_len // T
    n_lat = n_batch * seq_len
    lat_steps, ctx_steps = nb // QB, ncb // QB
    n_steps = lat_steps + (ctx_steps if with_ctx_queries else 0)
    n_rows = n_lat + (n_batch * ctx_len if with_ctx_queries else 0)

    def q_index(b, j):
        return (jnp.where(j < lat_steps, b * lat_steps + j, n_lat // (QB * T) + b * ctx_steps + (j - lat_steps)), 0)

    def kv_index(off):
        def index(b, j):
            return (b * nb + jnp.clip(QB * j + off, 0, nb - 1), 0)
        return index

    def ctx_index(b, j):
        return (n_lat // ctx_len + b, 0)

    kv_specs = [pl.BlockSpec((T, 2 * LANES), kv_index(off)) for off in range(-1, QB + 1)]
    kv_specs.append(pl.BlockSpec((ctx_len, 2 * LANES), ctx_index))
    qi = np.arange(4 * T)[:, None] % T
    kj = np.arange(3 * T + ctx_len)[None, :]
    band = jnp.asarray(np.where((np.abs(kj - T - qi) <= WINDOW) | (kj >= 3 * T), 0.0, NEG), dtype=F32)
    n_kv = len(kv_specs)
    return pl.pallas_call(
        partial(_attn_kernel, nb=nb),
        out_shape=jax.ShapeDtypeStruct((n_rows, A_Q_HEADS * HEAD_DIM), BF16),
        grid=(n_batch, n_steps),
        in_specs=[pl.BlockSpec((QB * T, A_Q_HEADS * HEAD_DIM), q_index)] + kv_specs + kv_specs
        + [_const_spec((A_Q_HEADS, LANES)), _const_spec(band.shape)],
        out_specs=pl.BlockSpec((QB * T, A_Q_HEADS * HEAD_DIM), q_index),
        compiler_params=_cparams("arbitrary", "arbitrary"),
        name="attention",
    )(aq, *([akd] * n_kv), *([avd] * n_kv), sink_lanes, band)


def _log_sigmoid(x):
    return jnp.minimum(x, 0.0) - jnp.log(1.0 + jnp.exp(-jnp.abs(x)))


def _lo_head():
    return lax.broadcasted_iota(jnp.int32, (1, LANES), 1) < R_DIM


_ROW_DK_F, _ROW_DK_B, _ROW_DQ_F, _ROW_DQ_B, _ROW_DC = (i * R_CHUNK for i in range(5))
_RET_ROWS = 4 * R_CHUNK + SUBLANES


def _ret_tables_kernel(dec_ref, tt_ref, rows_ref):
    T = R_CHUNK
    lg_f = _log_sigmoid(dec_ref[0:1, :])
    lg_b = _log_sigmoid(dec_ref[1:2, :])
    t = lax.broadcasted_iota(jnp.int32, (T, 1), 0).astype(F32)
    rows_ref[_ROW_DK_F:_ROW_DK_F + T, :] = jnp.exp((T - 1.0 - t) * lg_f)
    rows_ref[_ROW_DK_B:_ROW_DK_B + T, :] = jnp.exp(t * lg_b)
    rows_ref[_ROW_DQ_F:_ROW_DQ_F + T, :] = jnp.exp((t + 1.0) * lg_f)
    rows_ref[_ROW_DQ_B:_ROW_DQ_B + T, :] = jnp.exp((T - t) * lg_b)
    rows_ref[_ROW_DC:_ROW_DC + SUBLANES, :] = jnp.concatenate(
        [jnp.exp(T * lg_f), jnp.exp(T * lg_b), jnp.zeros((SUBLANES - 2, lg_f.shape[1]), F32)], axis=0)
    d = t - lax.broadcasted_iota(jnp.int32, (1, T), 1).astype(F32)
    for h in range(R_HEADS):
        a = lg_f[0:1, h * R_DIM:h * R_DIM + 1]
        b = lg_b[0:1, h * R_DIM:h * R_DIM + 1]
        fwd = jnp.exp(jnp.maximum(d, 0.0) * a)
        bwd = jnp.exp(jnp.maximum(-d, 0.0) * b)
        tt_ref[h * T:(h + 1) * T, :] = jnp.where(d > 0, fwd, jnp.where(d < 0, bwd, 2.0))


def _retention_tables(dec_lanes):
    T = R_CHUNK
    W = dec_lanes.shape[1]
    return pl.pallas_call(
        _ret_tables_kernel,
        out_shape=[jax.ShapeDtypeStruct((R_HEADS * T, T), F32), jax.ShapeDtypeStruct((_RET_ROWS, W), F32)],
        grid=(1,),
        in_specs=[_const_spec((2, W))],
        out_specs=[pl.BlockSpec((R_HEADS * T, T), lambda i: (0, 0)), pl.BlockSpec((_RET_ROWS, W), lambda i: (0, 0))],
        compiler_params=_cparams("arbitrary"),
        name="retention_tables",
    )(dec_lanes)


def _ret_state_kernel(kf_ref, vf_ref, kb_ref, vb_ref, rows_ref, sf_ref, sb_ref, sf_acc, sb_acc):
    T = R_CHUNK

    @pl.when(pl.program_id(1) == 0)
    def _():
        sf_acc[...] = jnp.zeros_like(sf_acc)
        sb_acc[...] = jnp.zeros_like(sb_acc)

    lo = _lo_head()

    def chunk(acc, s_ref, k_ref, v_ref, ci, d_k, d_c):
        s_ref[ci] = acc[...].astype(BF16)
        r = slice(ci * T, (ci + 1) * T)
        kd = k_ref[r, :].astype(F32) * d_k
        for p in range(R_HEADS // 2):
            ls = slice(p * LANES, (p + 1) * LANES)
            kv = jnp.dot(kd[:, ls].T.astype(BF16), v_ref[r, ls], preferred_element_type=F32)
            rs = slice(p * R_DIM, (p + 1) * R_DIM)
            acc[rs, :] = acc[rs, :] * d_c[:, ls] + jnp.where(lo, kv[0:R_DIM], kv[R_DIM:])

    dkf = rows_ref[_ROW_DK_F:_ROW_DK_F + T, :]
    dkb = rows_ref[_ROW_DK_B:_ROW_DK_B + T, :]
    dcf = rows_ref[_ROW_DC:_ROW_DC + 1, :]
    dcb = rows_ref[_ROW_DC + 1:_ROW_DC + 2, :]
    for ci in range(RET_CH):
        chunk(sf_acc, sf_ref, kf_ref, vf_ref, ci, dkf, dcf)
    for ci in reversed(range(RET_CH)):
        chunk(sb_acc, sb_ref, kb_ref, vb_ref, ci, dkb, dcb)


def _ret_step_maps(n_batch, seq_len, ctx_len):
    rows = RET_CH * R_CHUNK
    ncs, nls = ctx_len // rows, seq_len // rows
    ctx_base = n_batch * nls

    def fwd(b, j):
        return jnp.where(j < ncs, ctx_base + b * ncs + j, b * nls + (j - ncs))

    def bwd(b, j):
        return jnp.where(j < ncs, ctx_base + b * ncs + (ncs - 1 - j), b * nls + (nls - 1 - (j - ncs)))

    return ncs + nls, fwd, bwd


def _retention_states(rk, rv, tab_rows, *, n_batch, seq_len, ctx_len):
    T = R_CHUNK
    W = R_HEADS * R_DIM
    n_steps, fwd, bwd = _ret_step_maps(n_batch, seq_len, ctx_len)
    n_chunks = n_batch * n_steps * RET_CH
    rows = pl.BlockSpec((RET_CH * T, W), lambda b, j: (fwd(b, j), 0))
    rows_b = pl.BlockSpec((RET_CH * T, W), lambda b, j: (bwd(b, j), 0))
    state = jax.ShapeDtypeStruct((n_chunks, W // 2, LANES), BF16)
    return pl.pallas_call(
        _ret_state_kernel,
        out_shape=[state, state],
        grid=(n_batch, n_steps),
        in_specs=[rows, rows, rows_b, rows_b, _const_spec(tab_rows.shape)],
        out_specs=[pl.BlockSpec((RET_CH, W // 2, LANES), lambda b, j: (fwd(b, j), 0, 0)),
                   pl.BlockSpec((RET_CH, W // 2, LANES), lambda b, j: (bwd(b, j), 0, 0))],
        scratch_shapes=[pltpu.VMEM((W // 2, LANES), F32), pltpu.VMEM((W // 2, LANES), F32)],
        compiler_params=_cparams("arbitrary", "arbitrary"),
        name="retention_states",
    )(rk, rv, rk, rv, tab_rows)


def _ret_out_kernel(q_ref, k_ref, v_ref, g_ref, sf_ref, sb_ref, tt_ref, rows_ref, o_ref):
    T = R_CHUNK
    lo = _lo_head()

    def state_block(s):
        s = s.astype(F32)
        return jnp.concatenate([jnp.where(lo, s, 0.0), jnp.where(lo, 0.0, s)], axis=0).astype(BF16)

    for c in range(RET_OUT_CH):
        r = slice(c * T, (c + 1) * T)
        for p in range(R_HEADS // 2):
            ls = slice(p * LANES, (p + 1) * LANES)
            q = q_ref[r, ls]
            qf = q.astype(F32)
            q_stack = jnp.concatenate([jnp.where(lo, qf, 0.0), jnp.where(lo, 0.0, qf)], axis=0).astype(BF16)
            sc = lax.dot_general(q_stack, k_ref[r, ls], (((1,), (1,)), ((), ())), preferred_element_type=F32)
            att = (sc * tt_ref[2 * p * T:(2 * p + 2) * T, :]).astype(BF16)
            oh = jnp.dot(att, v_ref[r, ls], preferred_element_type=F32)
            o = jnp.where(lo, oh[:T], oh[T:])
            rs = slice(p * R_DIM, (p + 1) * R_DIM)
            states = jnp.concatenate([state_block(sf_ref[c, rs, :]), state_block(sb_ref[c, rs, :])], axis=1)
            oi = jnp.dot(q, states, preferred_element_type=F32)
            o = (o + oi[:, :LANES] * rows_ref[_ROW_DQ_F:_ROW_DQ_F + T, ls]
                 + oi[:, LANES:] * rows_ref[_ROW_DQ_B:_ROW_DQ_B + T, ls])
            o2 = o * o
            ms = jnp.where(lo, jnp.sum(jnp.where(lo, o2, 0.0), axis=-1, keepdims=True),
                           jnp.sum(jnp.where(lo, 0.0, o2), axis=-1, keepdims=True)) * (1.0 / R_DIM)
            g = g_ref[r, ls]
            o_ref[r, ls] = (o * lax.rsqrt(ms + EPS) * (g * jax.nn.sigmoid(g))).astype(BF16)


def _retention_out(rq, rk, rv, rg, sf, sb, tab_tt, tab_rows, *, n_rows):
    T = R_CHUNK
    W = R_HEADS * R_DIM
    rows = pl.BlockSpec((RET_OUT_CH * T, W), lambda c: (c, 0))
    st = pl.BlockSpec((RET_OUT_CH, W // 2, LANES), lambda c: (c, 0, 0))
    return pl.pallas_call(
        _ret_out_kernel,
        out_shape=jax.ShapeDtypeStruct((n_rows, W), BF16),
        grid=(n_rows // (RET_OUT_CH * T),),
        in_specs=[rows, rows, rows, rows, st, st, _const_spec(tab_tt.shape), _const_spec(tab_rows.shape)],
        out_specs=rows,
        compiler_params=_cparams("arbitrary"),
        name="retention_out",
    )(rq, rk, rv, rg, sf, sb, tab_tt, tab_rows)


def _hy_in_kernel(x_ref, xp_ref, xn_ref, mod_ref, gpre_ref, w_ref, bin_ref, wsh_ref, bsh_ref,
                  x0_ref, u_ref, *, tm, n_lat, tiles_per_seq, tiles_per_ctx):
    D = x_ref.shape[1]
    H = SUBLANES
    t = pl.program_id(0)
    lat_tiles = n_lat // tm
    pos = jnp.where(t < lat_tiles, t % tiles_per_seq, (t - lat_tiles) % tiles_per_ctx)
    n_pos = jnp.where(t < lat_tiles, tiles_per_seq, tiles_per_ctx)
    xs = jnp.concatenate([xp_ref[...], x_ref[...], xn_ref[...]], axis=0)
    hb = _adaln_in(xs, mod_ref, gpre_ref, 1).astype(BF16)
    keep_prev = jnp.where(pos == 0, 0.0, 1.0)
    keep_next = jnp.where(pos == n_pos - 1, 0.0, 1.0)
    n_ext = tm + 2 * H

    def conv(c):
        cols = slice(c * D, (c + 1) * D)
        z = jnp.dot(hb, w_ref[:, cols], preferred_element_type=F32) + bin_ref[:, cols]
        z = jnp.concatenate([z[0:H] * keep_prev, z[H:H + tm], z[H + tm:] * keep_next], axis=0)
        zm = pltpu.roll(z, 1, 0)[H:H + tm]
        zp = pltpu.roll(z, n_ext - 1, 0)[H:H + tm]
        return (zm * wsh_ref[0:1, cols] + z[H:H + tm] * wsh_ref[1:2, cols] + zp * wsh_ref[2:3, cols]
                + bsh_ref[:, cols])

    x0_ref[...] = conv(0).astype(BF16)
    u_ref[...] = (conv(2) * conv(1)).astype(BF16)


def _hy_in(s, mod, gpre, w_in, b_in, w_sh, b_sh, *, layer, n_rows, geom, ctx_len):
    n_lat, seq_len, _ = geom
    D = s.shape[1]
    tm = HY_TM
    H = SUBLANES
    blocks_per_tile = tm // H
    last_block = s.shape[0] // H - 1
    kern = partial(_hy_in_kernel, tm=tm, n_lat=n_lat, tiles_per_seq=seq_len // tm,
                   tiles_per_ctx=max(ctx_len // tm, 1))
    out = jax.ShapeDtypeStruct((n_rows, D), BF16)
    return pl.pallas_call(
        kern,
        out_shape=[out, out],
        grid=(n_rows // tm,),
        in_specs=[
            pl.BlockSpec((tm, D), lambda t: (t, 0)),
            pl.BlockSpec((H, D), lambda t: (jnp.maximum(t * blocks_per_tile - 1, 0), 0)),
            pl.BlockSpec((H, D), lambda t: (jnp.minimum((t + 1) * blocks_per_tile, last_block), 0)),
            pl.BlockSpec((1, 1, N_MOD, D), _mod_spec(layer, tm, *geom)),
            pl.BlockSpec((1, 1, D), lambda t: (3 * layer + 1, 0, 0), pipeline_mode=pl.Buffered(1)),
            _const_spec((D, 3 * D)),
            _const_spec((1, 3 * D)),
            _const_spec((3, 3 * D)),
            _const_spec((1, 3 * D)),
        ],
        out_specs=[pl.BlockSpec((tm, D), lambda t: (t, 0))] * 2,
        compiler_params=_cparams("arbitrary"),
        name="hyena_in",
    )(s, s, s, mod, gpre, w_in, b_in, w_sh, b_sh)


def _hy_filter_kernel(z_ref, f0_ref, fb0_ref, f1_ref, fb1_ref, f2_ref, fb2_ref, f3_ref, fr_ref, dl_ref,
                      kfb_ref):
    D = dl_ref.shape[1]
    hp = lax.Precision.HIGHEST
    z = z_ref[...]
    fr = fr_ref[...]
    a = jnp.sin(fr * (jnp.dot(z, f0_ref[...], precision=hp, preferred_element_type=F32) + fb0_ref[...]))
    a = jnp.sin(fr * (jnp.dot(a, f1_ref[...], precision=hp, preferred_element_type=F32) + fb1_ref[...]))
    a = jnp.sin(fr * (jnp.dot(a, f2_ref[...], precision=hp, preferred_element_type=F32) + fb2_ref[...]))
    k = jnp.dot(a.astype(BF16), f3_ref[...].astype(BF16), preferred_element_type=F32)
    decay = jnp.exp(-z[:, 0:1] * dl_ref[...])
    k_f = k[:, :D] * decay
    k_b = k[:, D:] * decay
    row = lax.broadcasted_iota(jnp.int32, (z.shape[0], 1), 0) + pl.program_id(0) * z.shape[0]
    k_b = jnp.where(row == 0, 0.0, k_b)
    kfb_ref[0] = k_f.astype(BF16)
    kfb_ref[1] = k_b.astype(BF16)


def _hy_filter(L, f0, fb0, f1, fb1, f2, fb2, f3, freq, D):
    t = np.linspace(0.0, 1.0, L)[:, None]
    bands = (HY_EMB - 1) // 2
    w = 2.0 * math.pi * np.arange(L)[:, None] / L
    f = np.linspace(1e-4, bands - 1, bands)[None]
    z = jnp.asarray(np.concatenate([t, np.cos(f * w), -np.sin(f * w), np.zeros((L, HY_EMB_PAD - HY_EMB))], -1),
                    dtype=F32)
    f0p = jnp.concatenate([f0, jnp.zeros((HY_EMB_PAD - HY_EMB, f0.shape[1]), F32)], 0)
    deltas = jnp.asarray(np.abs(np.linspace(HY_MIN_DECAY, HY_MAX_DECAY, D))[None], dtype=F32)
    tl = min(512, L)
    O = f0.shape[1]
    return pl.pallas_call(
        _hy_filter_kernel,
        out_shape=jax.ShapeDtypeStruct((2, L, D), BF16),
        grid=(L // tl,),
        in_specs=[pl.BlockSpec((tl, HY_EMB_PAD), lambda i: (i, 0)),
                  _const_spec((HY_EMB_PAD, O)), _const_spec((1, O)),
                  _const_spec((O, O)), _const_spec((1, O)),
                  _const_spec((O, O)), _const_spec((1, O)),
                  _const_spec((O, 2 * D)), _const_spec((1, O)), _const_spec((1, D))],
        out_specs=pl.BlockSpec((2, tl, D), lambda i: (0, i, 0)),
        compiler_params=_cparams("arbitrary"),
        name="hyena_filter",
    )(z, f0p, fb0[None], f1, fb1[None], f2, fb2[None], f3, freq[None], deltas)


def _dft_matrices(L, tf):
    k2 = 2 * np.arange(L, dtype=np.int64)[:, None] + 1
    ang = ((k2 * np.arange(L, dtype=np.int64)[None, :]) % (4 * L)) * (2.0 * math.pi / (4 * L))
    cos, nsin = np.cos(ang), -np.sin(ang)
    fwd = np.concatenate([cos.reshape(L // tf, tf, L), nsin.reshape(L // tf, tf, L)], axis=1).reshape(2 * L, L)
    inv = fwd.T * (1.0 / L)
    return jnp.asarray(fwd, dtype=F32).astype(BF16), jnp.asarray(inv, dtype=F32).astype(BF16)


def _spectrum_kernel(w_ref, kfb_ref, o_ref):
    tf = w_ref.shape[0] // 2
    k_f = kfb_ref[0].astype(F32)
    k_b = kfb_ref[1].astype(F32)
    o_ref[0:tf, :] = jnp.dot(w_ref[0:tf, :], (k_f + k_b).astype(BF16), preferred_element_type=F32)
    o_ref[tf:, :] = jnp.dot(w_ref[tf:, :], (k_f - k_b).astype(BF16), preferred_element_type=F32)


def _filter_spectrum(fwd, kfb, tf):
    _, L, D = kfb.shape
    return pl.pallas_call(
        _spectrum_kernel,
        out_shape=jax.ShapeDtypeStruct((2 * L, D), F32),
        grid=(L // tf,),
        in_specs=[pl.BlockSpec((2 * tf, L), lambda j: (j, 0)), _const_spec((2, L, D))],
        out_specs=pl.BlockSpec((2 * tf, D), lambda j: (j, 0)),
        compiler_params=_cparams("arbitrary"),
        name="filter_spectrum",
    )(fwd, kfb)


def _dft_fwd_kernel(w_ref, u_ref, kh_ref, o_ref):
    tf = w_ref.shape[0] // 2
    uh = jnp.dot(w_ref[...], u_ref[...], preferred_element_type=F32)
    ur, ui = uh[:tf], uh[tf:]
    kr, ki = kh_ref[0:tf, :], kh_ref[tf:, :]
    o_ref[0:tf, :] = (ur * kr - ui * ki).astype(BF16)
    o_ref[tf:, :] = (ur * ki + ui * kr).astype(BF16)


def _dft_forward(fwd, u, khat, *, n_batch, L, row_off, tf):
    D = u.shape[1]
    seq0 = row_off // L
    return pl.pallas_call(
        _dft_fwd_kernel,
        out_shape=jax.ShapeDtypeStruct((n_batch * 2 * L, D), BF16),
        grid=(n_batch, L // tf),
        in_specs=[pl.BlockSpec((2 * tf, L), lambda b, j: (j, 0)),
                  pl.BlockSpec((L, D), lambda b, j: (seq0 + b, 0)),
                  pl.BlockSpec((2 * tf, D), lambda b, j: (j, 0))],
        out_specs=pl.BlockSpec((2 * tf, D), lambda b, j: (b * (L // tf) + j, 0)),
        compiler_params=_cparams("arbitrary", "arbitrary"),
        name="dft_forward",
    )(fwd, u, khat)


def _dft_inv_kernel(w_ref, y_ref, u_ref, x0_ref, bias_ref, o_ref, acc):
    kk = pl.program_id(2)

    @pl.when(kk == 0)
    def _():
        acc[...] = jnp.zeros_like(acc)

    acc[...] += jnp.dot(w_ref[...], y_ref[...], preferred_element_type=F32)

    @pl.when(kk == pl.num_programs(2) - 1)
    def _():
        y = acc[...] + u_ref[...].astype(F32) * bias_ref[...]
        o_ref[...] = (x0_ref[...].astype(F32) * y).astype(BF16)


def _dft_inverse(inv, yhat, u, x0, bias, *, n_batch, L, row_off):
    D = u.shape[1]
    tm = min(1024, L)
    tk = min(2048, 2 * L)
    m_tiles = L // tm
    k_tiles = 2 * L // tk
    blk0 = row_off // tm
    rows_in = pl.BlockSpec((tm, D), lambda b, i, kk: (blk0 + b * m_tiles + i, 0))
    return pl.pallas_call(
        _dft_inv_kernel,
        out_shape=jax.ShapeDtypeStruct((n_batch * L, D), BF16),
        grid=(n_batch, m_tiles, k_tiles),
        in_specs=[pl.BlockSpec((tm, tk), lambda b, i, kk: (i, kk)),
                  pl.BlockSpec((tk, D), lambda b, i, kk: (b * k_tiles + kk, 0)),
                  rows_in, rows_in, _const_spec((1, D))],
        out_specs=pl.BlockSpec((tm, D), lambda b, i, kk: (b * m_tiles + i, 0)),
        scratch_shapes=[pltpu.VMEM((tm, D), F32)],
        compiler_params=_cparams("arbitrary", "arbitrary", "arbitrary"),
        name="dft_inverse",
    )(inv, yhat, u, x0, bias)


FFT_N2 = 256
FFT_J = 16
FFT_JB = 64


def _fft_expand_kernel(t_ref, tt_ref, ma_ref, mi_ref):
    J = FFT_J
    R, H1 = t_ref.shape[1], t_ref.shape[2]
    W = H1 * J
    Rp = tt_ref.shape[2]
    hp = lax.Precision.HIGHEST
    col = lax.broadcasted_iota(jnp.int32, (1, W), 1)
    spread = (lax.broadcasted_iota(jnp.int32, (H1, 1), 0) == col // J).astype(F32)
    row_j = lax.broadcasted_iota(jnp.int32, (R, 1), 0) % J
    ma = jnp.dot(t_ref[0], spread, precision=hp, preferred_element_type=F32)
    ma_ref[0] = jnp.where(row_j == col % J, ma, 0.0).astype(BF16)
    rowi = lax.broadcasted_iota(jnp.int32, (W, 1), 0)
    spread_t = (rowi // J == lax.broadcasted_iota(jnp.int32, (1, H1), 1)).astype(F32)
    col_j = lax.broadcasted_iota(jnp.int32, (1, Rp), 1) % J
    mi = jnp.dot(spread_t, tt_ref[0], precision=hp, preferred_element_type=F32)
    mi_ref[0] = jnp.where(rowi % J == col_j, mi, 0.0).astype(BF16)


def _fft_matrices(L):
    N = 2 * L
    N1 = N // FFT_N2
    H1, K1, J, A = N1 // 2, N1 // 2 + 1, FFT_J, FFT_N2 // FFT_J
    R = K1 * 2 * J
    Rp = R + (-R) % LANES
    a = np.arange(A, dtype=np.int64)[:, None, None, None]
    k1 = np.arange(K1, dtype=np.int64)[None, :, None, None]
    j = np.arange(J, dtype=np.int64)[None, None, :, None]
    n1 = np.arange(H1, dtype=np.int64)[None, None, None, :]
    ang = ((k1 * (FFT_N2 * n1 + J * a + j)) % N) * (2.0 * math.pi / N)
    base = np.stack([np.cos(ang), -np.sin(ang)], axis=2)
    kk = np.arange(K1)
    w = np.where((kk == 0) | (kk == N1 // 2), 1.0, 2.0) / N
    t = jnp.asarray(base.reshape(A, R, H1), dtype=F32)
    tt = np.transpose((base * w[None, :, None, None, None]).reshape(A, R, H1), (0, 2, 1))
    tt = jnp.asarray(np.pad(tt, ((0, 0), (0, 0), (0, Rp - R))), dtype=F32)
    ma, mi = pl.pallas_call(
        _fft_expand_kernel,
        out_shape=[jax.ShapeDtypeStruct((A, R, H1 * J), BF16), jax.ShapeDtypeStruct((A, H1 * J, Rp), BF16)],
        grid=(A,),
        in_specs=[pl.BlockSpec((1, R, H1), lambda i: (i, 0, 0)), pl.BlockSpec((1, H1, Rp), lambda i: (i, 0, 0))],
        out_specs=[pl.BlockSpec((1, R, H1 * J), lambda i: (i, 0, 0)),
                   pl.BlockSpec((1, H1 * J, Rp), lambda i: (i, 0, 0))],
        compiler_params=_cparams("arbitrary"),
        name="fft_matrices",
    )(t, tt)
    n2 = np.arange(FFT_N2, dtype=np.int64)
    angf = ((n2[:, None] * n2[None, :]) % FFT_N2) * (2.0 * math.pi / FFT_N2)
    fr, fi = np.cos(angf), -np.sin(angf)
    cat = lambda top, bot: jnp.asarray(np.concatenate([top, bot], axis=0), dtype=F32).astype(BF16)
    return dict(ma=ma, mi=mi, k1=K1, h1=H1,
                mre=cat(fr, fi), mim=cat(-fi, fr), gre=cat(fr, -fi), gim=cat(fi, fr))


def _fft_a_kernel(u_ref, ma_ref, z_ref):
    H1, JB, D = u_ref.shape
    K1 = z_ref.shape[1]
    J = FFT_J
    a = pl.program_id(1)
    for q in range(JB // J):
        rows = slice(q * J, (q + 1) * J)
        data = u_ref[:, rows, :].reshape(H1 * J, D)
        res = jnp.dot(ma_ref[a * (JB // J) + q], data, preferred_element_type=F32)
        z_ref[0, :, :, rows, :] = res.astype(BF16).reshape(K1, 2, J, D)


def _fft_a(u3, mats, *, n_batch):
    D = u3.shape[2]
    K1, H1 = mats["k1"], mats["h1"]
    return pl.pallas_call(
        _fft_a_kernel,
        out_shape=jax.ShapeDtypeStruct((n_batch, K1, 2, FFT_N2, D), BF16),
        grid=(n_batch, FFT_N2 // FFT_JB),
        in_specs=[pl.BlockSpec((H1, FFT_JB, D), lambda b, a: (b, a, 0)), _const_spec(mats["ma"].shape)],
        out_specs=pl.BlockSpec((1, K1, 2, FFT_JB, D), lambda b, a: (b, 0, 0, a, 0)),
        compiler_params=_cparams("arbitrary", "arbitrary"),
        name="fft_stage_a",
    )(u3, mats["ma"])


def _dft256(mre_ref, mim_ref, z_ref, idx):
    n2 = FFT_N2
    x = (jnp.dot(mre_ref[...], z_ref[idx + (0,)], preferred_element_type=F32)
         + jnp.dot(mim_ref[...], z_ref[idx + (1,)], preferred_element_type=F32))
    return x[:n2], x[n2:]


def _fft_spec_kernel(zf_ref, zb_ref, mre_ref, mim_ref, o_ref):
    fr, fi = _dft256(mre_ref, mim_ref, zf_ref, (0, 0))
    br, bi = _dft256(mre_ref, mim_ref, zb_ref, (0, 0))
    o_ref[0, 0] = fr + br
    o_ref[0, 1] = fi - bi


def _fft_spectrum(zfilt, mats):
    _, K1, _, n2, D = zfilt.shape
    blk = (1, 1, 2, n2, D)
    return pl.pallas_call(
        _fft_spec_kernel,
        out_shape=jax.ShapeDtypeStruct((K1, 2, n2, D), F32),
        grid=(K1,),
        in_specs=[pl.BlockSpec(blk, lambda k: (0, k, 0, 0, 0)), pl.BlockSpec(blk, lambda k: (1, k, 0, 0, 0)),
                  _const_spec(mats["mre"].shape), _const_spec(mats["mim"].shape)],
        out_specs=pl.BlockSpec((1, 2, n2, D), lambda k: (k, 0, 0, 0)),
        compiler_params=_cparams("arbitrary"),
        name="fft_filter_spectrum",
    )(zfilt, zfilt, mats["mre"], mats["mim"])


def _fft_mid_kernel(z_ref, kh_ref, mre_ref, mim_ref, gre_ref, gim_ref, o_ref):
    n2 = FFT_N2
    kr, ki = kh_ref[0, 0], kh_ref[0, 1]
    for b in range(z_ref.shape[0]):
        xr, xi = _dft256(mre_ref, mim_ref, z_ref, (b, 0))
        yr = (xr * kr - xi * ki).astype(BF16)
        yi = (xr * ki + xi * kr).astype(BF16)
        zp = (jnp.dot(gre_ref[...], yr, preferred_element_type=F32)
              + jnp.dot(gim_ref[...], yi, preferred_element_type=F32))
        o_ref[b, 0, 0] = zp[:n2].astype(BF16)
        o_ref[b, 0, 1] = zp[n2:].astype(BF16)


def _fft_mid(z, khat, mats):
    n_batch, K1, _, n2, D = z.shape
    nb = 2 if n_batch % 2 == 0 else 1
    blk = pl.BlockSpec((nb, 1, 2, n2, D), lambda k, b: (b, k, 0, 0, 0))
    mat = [_const_spec(mats[m].shape) for m in ("mre", "mim", "gre", "gim")]
    return pl.pallas_call(
        _fft_mid_kernel,
        out_shape=jax.ShapeDtypeStruct(z.shape, BF16),
        grid=(K1, n_batch // nb),
        in_specs=[blk, pl.BlockSpec((1, 2, n2, D), lambda k, b: (k, 0, 0, 0))] + mat,
        out_specs=blk,
        compiler_params=_cparams("arbitrary", "arbitrary"),
        name="fft_stage_b",
    )(z, khat, mats["mre"], mats["mim"], mats["gre"], mats["gim"])


def _fft_a_inv_kernel(zp_ref, mi_ref, u_ref, x0_ref, bias_ref, o_ref):
    H1, JB, D = u_ref.shape
    K1 = zp_ref.shape[1]
    J = FFT_J
    a = pl.program_id(1)
    k_pad = mi_ref.shape[2] - K1 * 2 * J
    for q in range(JB // J):
        rows = slice(q * J, (q + 1) * J)
        zz = zp_ref[0, :, :, rows, :].reshape(K1 * 2 * J, D)
        if k_pad:
            zz = jnp.concatenate([zz, jnp.zeros((k_pad, D), BF16)], axis=0)
        y = jnp.dot(mi_ref[a * (JB // J) + q], zz, preferred_element_type=F32).reshape(H1, J, D)
        y = y + u_ref[:, rows, :].astype(F32) * bias_ref[...]
        o_ref[:, rows, :] = (x0_ref[:, rows, :].astype(F32) * y).astype(BF16)


def _fft_a_inv(zp, mats, u3, x03, bias, *, n_blocks_out):
    n_batch, K1, _, n2, D = zp.shape
    H1 = mats["h1"]
    rows = pl.BlockSpec((H1, FFT_JB, D), lambda b, a: (b, a, 0))
    return pl.pallas_call(
        _fft_a_inv_kernel,
        out_shape=jax.ShapeDtypeStruct((n_blocks_out, n2, D), BF16),
        grid=(n_batch, n2 // FFT_JB),
        in_specs=[pl.BlockSpec((1, K1, 2, FFT_JB, D), lambda b, a: (b, 0, 0, a, 0)),
                  _const_spec(mats["mi"].shape), rows, rows, _const_spec((1, D))],
        out_specs=rows,
        compiler_params=_cparams("arbitrary", "arbitrary"),
        name="fft_stage_a_inv",
    )(zp, mats["mi"], u3, x03, bias)


def kernel(x, c, ctx, c_ctx, w_mod, b_mod, norm_pre, norm_post, ffn_gate, ffn_up, ffn_down,
           mix_w_in, attn_sink, ret_decay, mix_w_out, hy_w_in, hy_b_in, hy_short_w, hy_short_b,
           hy_f0, hy_fb0, hy_f1, hy_fb1, hy_f2, hy_fb2, hy_f3, hy_freq, hy_bias, hy_w_out):
    Bn, L, D = x.shape
    C = ctx.shape[1]
    n_lat = Bn * L
    n_all = n_lat + Bn * C
    geom = (n_lat, L, Bn)
    assert L % min(PROJ_TM, L) == 0 and L % HY_TM == 0 and n_lat % C == 0 and (Bn * C) % min(PROJ_TM, L) == 0
    assert L % min(FFN_TM, L) == 0 and (Bn * C) % min(FFN_TM, L) == 0
    assert C % (RET_CH * R_CHUNK) == 0 and L % (RET_CH * R_CHUNK) == 0 and C % HY_TM == 0
    assert L % FFT_N2 == 0 and n_all % FFT_N2 == 0
    last_reader = DEPTH - 1 if (DEPTH - 1) % 2 == 0 else DEPTH - 2

    c_all = jnp.concatenate([c, c_ctx[None], jnp.zeros((SUBLANES - (Bn + 1) % SUBLANES, D), F32)], axis=0)
    mod = _modulation(c_all, w_mod, b_mod).reshape(DEPTH, c_all.shape[0], N_MOD, D)

    rope_a = _rope_tables(_axial_angles(L), min(PROJ_TM, L))
    rope_r = _rope_tables(_line_angles(L), min(PROJ_TM, L))
    fft = _fft_matrices(L)

    npre = norm_pre.reshape(DEPTH * 3, 1, D)
    npost = norm_post.reshape(DEPTH * 3, 1, D)
    wg, wu, wd = ffn_gate.astype(BF16), ffn_up.astype(BF16), ffn_down.astype(BF16)
    mix_wo, hy_wo = mix_w_out.astype(BF16), hy_w_out.astype(BF16)

    def ffn(s, l, i, j, n_rows, mix=None):
        return _ffn(s, mod, npre, npost, wg, wu, wd, layer=l, i=i, j=j, n_rows=n_rows, geom=geom, mix=mix)

    s = (x.reshape(n_lat, D), ctx.reshape(Bn * C, D))
    for l in range(DEPTH):
        ctx_live = l <= last_reader
        ctx_full = l < last_reader
        n_in = n_all if ctx_live else n_lat
        n_out = n_all if ctx_full else n_lat
        s = ffn(s, l, 0, 0, n_in)
        if l % 2 == 0:
            e = l // 2
            aq, rq, rg, akd, avd, rk, rv = _even_in(s, mod, npre, _pack_even_w_in(mix_w_in[e]), rope_a, rope_r,
                                                    layer=l, n_rows=n_in, geom=geom)
            sink_lanes = jnp.broadcast_to(attn_sink[e][:, None], (A_Q_HEADS, LANES))
            dec_lanes = jnp.repeat(ret_decay[e], R_DIM, axis=1)
            a = _attention(aq, akd, avd, sink_lanes, n_batch=Bn, seq_len=L, ctx_len=C,
                           with_ctx_queries=ctx_full)
            tab_tt, tab_rows = _retention_tables(dec_lanes)
            sf, sb = _retention_states(rk, rv, tab_rows, n_batch=Bn, seq_len=L, ctx_len=C)
            r = _retention_out(rq, rk, rv, rg, sf, sb, tab_tt, tab_rows, n_rows=n_out)
            mix = ([a, r], mix_wo, e)
        else:
            o = l // 2
            x0, u = _hy_in(s, mod, npre, hy_w_in[o].astype(BF16), hy_b_in[o][None], hy_short_w[o],
                           hy_short_b[o][None], layer=l, n_rows=n_in, geom=geom, ctx_len=C)
            filt = (hy_f0[o], hy_fb0[o], hy_f1[o], hy_fb1[o], hy_f2[o], hy_fb2[o], hy_f3[o], hy_freq[o])
            bias = hy_bias[o][None]
            blocks = lambda arr: arr.reshape(arr.shape[0] // FFT_N2, FFT_N2, D)
            kfb = _hy_filter(L, *filt, D)
            khat = _fft_spectrum(_fft_a(blocks(kfb.reshape(2 * L, D)), fft, n_batch=2), fft)
            zp = _fft_mid(_fft_a(blocks(u), fft, n_batch=Bn), khat, fft)
            yg = _fft_a_inv(zp, fft, blocks(u), blocks(x0), bias, n_blocks_out=n_lat // FFT_N2).reshape(n_lat, D)
            if ctx_full:
                tf = min(DFT_TF, C)
                fwd, inv = _dft_matrices(C, tf)
                khat_c = _filter_spectrum(fwd, _hy_filter(C, *filt, D), tf)
                yhat = _dft_forward(fwd, u, khat_c, n_batch=Bn, L=C, row_off=n_lat, tf=tf)
                yg = (yg, _dft_inverse(inv, yhat, u, x0, bias, n_batch=Bn, L=C, row_off=n_lat))
            mix = ([yg], hy_wo, o)
        s = ffn(s, l, 2, 1, n_out, mix)
    return s[:n_lat].reshape(Bn, L, D)
```

```python
import math
from functools import partial

import jax
import jax.numpy as jnp
import numpy as np
from jax import lax
from jax.experimental import pallas as pl
from jax.experimental.pallas import tpu as pltpu

F32 = jnp.float32
BF16 = jnp.bfloat16

DEPTH = 4
GRID_W = 64
EPS = 1e-6
NEG = -1e30
N_MOD = 9
FFN_RESIDUAL = 0.5
HEAD_DIM = 64
A_Q_HEADS = 8
A_KV_HEADS = 2
WINDOW = 128
BLOCK = 128
ROPE_BASE = 10000.0
R_DIM = 64
R_HEADS = 8
R_CHUNK = 128
HY_EMB = 33
HY_EMB_PAD = 64
HY_MAX_DECAY = math.log(1e-2) / 0.3
HY_MIN_DECAY = math.log(1e-2) / 1.5

LANES = 128
SUBLANES = 8
V7X_VMEM_LIMIT_BYTES = 56 * 1024 * 1024

FFN_TM = 1024
FFN_SUB = 2
FFN_TF = 256
PROJ_TM = 1024
HY_TM = 256
MOD_TN = 2304
DFT_TF = 256
RET_LG = 8
RET_OUT_CH = 4
ATT_RC = 256
ATT_QB = 2


def _cparams(*sem):
    return pltpu.CompilerParams(dimension_semantics=sem, vmem_limit_bytes=V7X_VMEM_LIMIT_BYTES)


def _const_spec(shape):
    zeros = (0,) * len(shape)
    return pl.BlockSpec(shape, lambda *_: zeros, pipeline_mode=pl.Buffered(1))


def _mod_spec(layer, tm, n_lat, seq_len, n_batch):
    def index(t, *_):
        return (layer, jnp.where(t * tm < n_lat, (t * tm) // seq_len, n_batch), 0, 0)
    return index


def _adaln_in(x, mod_ref, gpre_ref, i):
    shift = mod_ref[0, 0, 3 * i:3 * i + 1, :]
    scale = mod_ref[0, 0, 3 * i + 1:3 * i + 2, :]
    gain = gpre_ref[0] * (1.0 + scale)
    inv = lax.rsqrt(jnp.mean(x * x, axis=-1, keepdims=True) + EPS)
    return (x * inv) * gain + shift


def _adaln_out(x, y, mod_ref, gpost_ref, i, w):
    gate = mod_ref[0, 0, 3 * i + 2:3 * i + 3, :]
    gain = (w * gate) * gpost_ref[0]
    inv = lax.rsqrt(jnp.mean(y * y, axis=-1, keepdims=True) + EPS)
    return x + (y * inv) * gain


def _mod_kernel(c_ref, w_ref, b_ref, o_ref):
    c = c_ref[...]
    a = (c * jax.nn.sigmoid(c)).astype(BF16)
    o_ref[0] = jnp.dot(a, w_ref[0].astype(BF16), preferred_element_type=F32) + b_ref[0]


def _modulation(c_all, w_mod, b_mod):
    depth, D, W = w_mod.shape
    rows = c_all.shape[0]
    return pl.pallas_call(
        _mod_kernel,
        out_shape=jax.ShapeDtypeStruct((depth, rows, W), F32),
        grid=(depth, W // MOD_TN),
        in_specs=[
            pl.BlockSpec((rows, D), lambda l, j: (0, 0)),
            pl.BlockSpec((1, D, MOD_TN), lambda l, j: (l, 0, j)),
            pl.BlockSpec((1, 1, MOD_TN), lambda l, j: (l, 0, j)),
        ],
        out_specs=pl.BlockSpec((1, rows, MOD_TN), lambda l, j: (l, 0, j)),
        compiler_params=_cparams("arbitrary", "arbitrary"),
        name="modulation",
    )(c_all, w_mod, b_mod.reshape(depth, 1, W))


def _ffn_kernel(*refs, i, tf, n_chunks, n_parts, n_sub, split, lat_tiles):
    row_refs, k = [], 0
    for is_split in split:
        row_refs.append(refs[k:k + 2] if is_split else refs[k:k + 1])
        k += 2 if is_split else 1
    mod_ref, npre_ref, npost_ref = refs[k:k + 3]
    rest = refs[k + 3:]
    if n_parts:
        nmix_ref, wo_ref = rest[:2]
        rest = rest[2:]
    wg_ref, wu_ref, wd_ref, o_ref = rest
    is_ctx = (lax.broadcasted_iota(jnp.int32, (1, 1), 0) + pl.program_id(0)) >= lat_tiles

    def read(src, rows):
        if len(src) == 1:
            return src[0][rows, :]
        return jnp.where(is_ctx, src[1][rows, :], src[0][rows, :])

    sub = o_ref.shape[0] // n_sub
    for b in range(n_sub):
        rows = slice(b * sub, (b + 1) * sub)
        x = read(row_refs[-1], rows)
        if n_parts:
            y = None
            row = 0
            for src in row_refs[:-1]:
                width = src[0].shape[1]
                part = jnp.dot(read(src, rows), wo_ref[0, row:row + width, :], preferred_element_type=F32)
                y = part if y is None else y + part
                row += width
            x = _adaln_out(x, y, mod_ref, nmix_ref, 1, 1.0)
        hb = _adaln_in(x, mod_ref, npre_ref, i).astype(BF16)
        acts = []
        for j in range(n_chunks):
            cols = slice(j * tf, (j + 1) * tf)
            g = jnp.dot(hb, wg_ref[0, 0, :, cols], preferred_element_type=F32)
            u = jnp.dot(hb, wu_ref[0, 0, :, cols], preferred_element_type=F32)
            acts.append((g * jax.nn.sigmoid(g) * u).astype(BF16))
        y = jnp.dot(jnp.concatenate(acts, axis=1), wd_ref[0, 0], preferred_element_type=F32)
        o_ref[rows, :] = _adaln_out(x, y, mod_ref, npost_ref, i, FFN_RESIDUAL)


def _ffn(s, mod, npre, npost, wg, wu, wd, *, layer, i, j, n_rows, geom, mix=None):
    n_lat = geom[0]
    D = wd.shape[3]
    d_ff = wd.shape[2]
    tm = min(FFN_TM, geom[1])
    lat_tiles = n_lat // tm
    rows = lambda t: (t, 0)
    norm = lambda sub: pl.BlockSpec((1, 1, D), lambda t: (3 * layer + sub, 0, 0), pipeline_mode=pl.Buffered(1))
    weight = lambda shape: pl.BlockSpec((1, 1) + shape, lambda t: (layer, j, 0, 0), pipeline_mode=pl.Buffered(1))
    parts, extra_specs, extra_args = [], [], []
    if mix is not None:
        parts, w_out, e = mix
        extra_specs = [norm(1), pl.BlockSpec((1,) + w_out.shape[1:], lambda t: (e, 0, 0),
                                             pipeline_mode=pl.Buffered(1))]
        extra_args = [npost, w_out]
    row_specs, row_args, split = [], [], []
    for src in list(parts) + [s]:
        if isinstance(src, tuple):
            lat, ctx = src
            row_specs += [pl.BlockSpec((tm, lat.shape[1]), lambda t: (jnp.minimum(t, lat_tiles - 1), 0)),
                          pl.BlockSpec((tm, ctx.shape[1]), lambda t: (jnp.maximum(t - lat_tiles, 0), 0))]
            row_args += [lat, ctx]
        else:
            row_specs.append(pl.BlockSpec((tm, src.shape[1]), rows))
            row_args.append(src)
        split.append(isinstance(src, tuple))
    return pl.pallas_call(
        partial(_ffn_kernel, i=i, tf=FFN_TF, n_chunks=d_ff // FFN_TF, n_parts=len(parts), n_sub=FFN_SUB,
                split=tuple(split), lat_tiles=lat_tiles),
        out_shape=jax.ShapeDtypeStruct((n_rows, D), F32),
        grid=(n_rows // tm,),
        in_specs=row_specs + [pl.BlockSpec((1, 1, N_MOD, D), _mod_spec(layer, tm, *geom)), norm(i), norm(i)]
        + extra_specs + [weight((D, d_ff)), weight((D, d_ff)), weight((d_ff, D))],
        out_specs=pl.BlockSpec((tm, D), rows),
        compiler_params=_cparams("arbitrary"),
        name="ffn",
    )(*row_args, mod, npre, npost, *extra_args, wg, wu, wd)


_AQ = (0, 512)
_RQ = (512, 1024)
_RG = (1024, 1536)
_AK = (1536, 1792)
_AV = (1792, 2048)
_RK = (2048, 2560)
_RV = (2560, 3072)
EVEN_W = 3072


def _pack_even_w_in(w_in):
    aq, rq, rg = w_in[:, 0:512], w_in[:, 512:1024], w_in[:, 1024:1536]
    ak, av = w_in[:, 1536:1664], w_in[:, 1664:1792]
    rk, rv = w_in[:, 1792:2304], w_in[:, 2304:2816]

    def dup(a):
        g0, g1 = a[:, :HEAD_DIM], a[:, HEAD_DIM:]
        return jnp.concatenate([g0, g0, g1, g1], axis=1)

    return jnp.concatenate([aq, rq, rg, dup(ak), dup(av), rk, rv], axis=1).astype(BF16)


def _rope_tables(ang, ident_rows):
    L = ang.shape[0]
    cos, sin = jnp.cos(ang), jnp.sin(ang)
    zero = jnp.zeros_like(sin)
    cos_t = jnp.tile(cos, (1, 4))
    sin_a = jnp.tile(jnp.concatenate([-sin, zero], axis=1), (1, 2))
    sin_b = jnp.tile(jnp.concatenate([zero, sin], axis=1), (1, 2))
    pad_one = jnp.ones((ident_rows, LANES), F32)
    pad_zero = jnp.zeros((ident_rows, LANES), F32)
    return (jnp.concatenate([cos_t, pad_one], 0), jnp.concatenate([sin_a, pad_zero], 0),
            jnp.concatenate([sin_b, pad_zero], 0))


def _axial_angles(L):
    n_rows = L // GRID_W
    row = jnp.repeat(jnp.arange(n_rows, dtype=F32), GRID_W)
    col = jnp.tile(jnp.arange(GRID_W, dtype=F32), n_rows)
    nf = HEAD_DIM // 4
    inv = ROPE_BASE ** (-jnp.arange(nf, dtype=F32) / nf)
    return jnp.concatenate([row[:, None] * inv, col[:, None] * inv], -1)


def _line_angles(L):
    inv = ROPE_BASE ** (-jnp.linspace(0.0, 1.0, R_DIM // 2, dtype=F32))
    return jnp.arange(L, dtype=F32)[:, None] * inv


def _rope(z, cos, sin_a, sin_b):
    outs = []
    for c in range(z.shape[1] // LANES):
        zc = z[:, c * LANES:(c + 1) * LANES]
        outs.append(zc * cos + pltpu.roll(zc, 96, 1) * sin_a + pltpu.roll(zc, 32, 1) * sin_b)
    return outs[0] if len(outs) == 1 else jnp.concatenate(outs, axis=1)


def _even_in_kernel(x_ref, mod_ref, gpre_ref, w_ref, ca_ref, saa_ref, sab_ref, cr_ref, sra_ref, srb_ref,
                    aq_ref, rq_ref, rg_ref, ak_ref, av_ref, rk_ref, rv_ref):
    hb = _adaln_in(x_ref[...], mod_ref, gpre_ref, 1).astype(BF16)

    def proj(cols):
        return jnp.dot(hb, w_ref[:, cols[0]:cols[1]], preferred_element_type=F32)

    rope_a = (ca_ref[...], saa_ref[...], sab_ref[...])
    rope_r = (cr_ref[...], sra_ref[...], srb_ref[...])
    aq_ref[...] = (_rope(proj(_AQ), *rope_a) * HEAD_DIM ** -0.5).astype(BF16)
    rq_ref[...] = _rope(proj(_RQ), *rope_r).astype(BF16)
    rg_ref[...] = proj(_RG)
    ak_ref[...] = _rope(proj(_AK), *rope_a).astype(BF16)
    av_ref[...] = proj(_AV).astype(BF16)
    rk_ref[...] = (_rope(proj(_RK), *rope_r) * R_DIM ** -0.5).astype(BF16)
    rv_ref[...] = proj(_RV).astype(BF16)


def _even_in(s, mod, gpre, w_pack, rope_a, rope_r, *, layer, n_rows, geom):
    n_lat, seq_len, _ = geom
    D = s.shape[1]
    tm = min(PROJ_TM, seq_len)
    tiles_per_seq = seq_len // tm

    def rope_index(t):
        return (jnp.where(t * tm < n_lat, t % tiles_per_seq, tiles_per_seq), 0)

    rope_spec = pl.BlockSpec((tm, LANES), rope_index)

    def out(width, dtype):
        return jax.ShapeDtypeStruct((n_rows, width), dtype), pl.BlockSpec((tm, width), lambda t: (t, 0))

    outs = [out(512, BF16), out(512, BF16), out(512, F32), out(256, BF16), out(256, BF16),
            out(512, BF16), out(512, BF16)]
    return pl.pallas_call(
        _even_in_kernel,
        out_shape=[o[0] for o in outs],
        grid=(n_rows // tm,),
        in_specs=[
            pl.BlockSpec((tm, D), lambda t: (t, 0)),
            pl.BlockSpec((1, 1, N_MOD, D), _mod_spec(layer, tm, *geom)),
            pl.BlockSpec((1, 1, D), lambda t: (3 * layer + 1, 0, 0), pipeline_mode=pl.Buffered(1)),
            _const_spec((D, EVEN_W)),
        ] + [rope_spec] * 6,
        out_specs=[o[1] for o in outs],
        compiler_params=_cparams("arbitrary"),
        name="even_in",
    )(s, mod, gpre, w_pack, *rope_a, *rope_r)


def _attn_kernel(q_ref, k0_ref, k1_ref, k2_ref, k3_ref, kx_ref, v0_ref, v1_ref, v2_ref, v3_ref, vx_ref,
                 sink_ref, band_ref, o_ref, *, nb):
    T = BLOCK
    j = pl.program_id(1)
    kj = lax.broadcasted_iota(jnp.int32, (1, band_ref.shape[1]), 1)
    lo = lax.broadcasted_iota(jnp.int32, (1, LANES), 1) < HEAD_DIM
    k_refs = (k0_ref, k1_ref, k2_ref, k3_ref)
    v_refs = (v0_ref, v1_ref, v2_ref, v3_ref)
    def scores(i, g):
        n = ATT_QB * j + i
        k_lo = jnp.where(n > 0, 0, T)
        k_hi = jnp.where(n < nb - 1, 3 * T, jnp.where(n < nb, 2 * T, 0))
        exists = ((kj >= k_lo) & (kj < k_hi)) | (kj >= 3 * T)
        bias = band_ref[...] + jnp.where(exists, 0.0, NEG)
        gl = slice(g * LANES, (g + 1) * LANES)
        kd = jnp.concatenate([r[:, gl] for r in k_refs[i:i + 3]] + [kx_ref[:, gl]], axis=0)
        vd = jnp.concatenate([r[:, gl] for r in v_refs[i:i + 3]] + [vx_ref[:, gl]], axis=0)
        qs = []
        for c in range(2):
            cl = slice((2 * g + c) * LANES, (2 * g + c + 1) * LANES)
            qc = q_ref[i * T:(i + 1) * T, cl].astype(F32)
            qs.append(jnp.where(lo, qc, 0.0))
            qs.append(jnp.where(lo, 0.0, qc))
        q_stack = jnp.concatenate(qs, axis=0).astype(BF16)
        s_all = lax.dot_general(q_stack, kd, (((1,), (1,)), ((), ())), preferred_element_type=F32)
        return s_all, vd, bias

    def finish(i, g, s_all, vd, bias):
        ps, inv = [], []
        for rc in range(4 * T // ATT_RC):
            rows = slice(rc * ATT_RC, (rc + 1) * ATT_RC)
            heads = range(4 * g + (rc * ATT_RC) // T, 4 * g + ((rc + 1) * ATT_RC - 1) // T + 1)
            rows_per_head = min(ATT_RC, T)
            sink = jnp.concatenate(
                [jnp.broadcast_to(sink_ref[h:h + 1, 0:1], (rows_per_head, 1)) for h in heads], axis=0)
            s = s_all[rows] + bias[rows]
            m = jnp.maximum(jnp.max(s, axis=-1, keepdims=True), sink)
            p = jnp.exp(s - m)
            inv.append(1.0 / (jnp.sum(p, axis=-1, keepdims=True) + jnp.exp(sink - m)))
            ps.append(p.astype(BF16))
        p = jnp.concatenate(ps, axis=0)
        o = jnp.dot(p, vd, preferred_element_type=F32) * jnp.concatenate(inv, axis=0)
        for c in range(2):
            oc = jnp.where(lo, o[(2 * c) * T:(2 * c + 1) * T], o[(2 * c + 1) * T:(2 * c + 2) * T])
            o_ref[i * T:(i + 1) * T, (2 * g + c) * LANES:(2 * g + c + 1) * LANES] = oc.astype(BF16)

    items = [(i, g) for i in range(ATT_QB) for g in range(A_KV_HEADS)]
    ahead = 2
    pending = [scores(*it) for it in items[:ahead]]
    for k, it in enumerate(items):
        finish(*it, *pending[k])
        if k + ahead < len(items):
            pending.append(scores(*items[k + ahead]))


def _attention(aq, akd, avd, sink_lanes, *, n_batch, seq_len, ctx_len, with_ctx_queries):
    T = BLOCK
    QB = ATT_QB
    nb = seq_len // T
    ncb = ctx_len // T
    n_lat = n_batch * seq_len
    lat_steps, ctx_steps = nb // QB, ncb // QB
    n_steps = lat_steps + (ctx_steps if with_ctx_queries else 0)
    n_rows = n_lat + (n_batch * ctx_len if with_ctx_queries else 0)

    def q_index(b, j):
        return (jnp.where(j < lat_steps, b * lat_steps + j, n_lat // (QB * T) + b * ctx_steps + (j - lat_steps)), 0)

    def kv_index(off):
        def index(b, j):
            return (b * nb + jnp.clip(QB * j + off, 0, nb - 1), 0)
        return index

    def ctx_index(b, j):
        return (n_lat // ctx_len + b, 0)

    kv_specs = [pl.BlockSpec((T, 2 * LANES), kv_index(off)) for off in range(-1, QB + 1)]
    kv_specs.append(pl.BlockSpec((ctx_len, 2 * LANES), ctx_index))
    qi = np.arange(4 * T)[:, None] % T
    kj = np.arange(3 * T + ctx_len)[None, :]
    band = jnp.asarray(np.where((np.abs(kj - T - qi) <= WINDOW) | (kj >= 3 * T), 0.0, NEG), dtype=F32)
    n_kv = len(kv_specs)
    return pl.pallas_call(
        partial(_attn_kernel, nb=nb),
        out_shape=jax.ShapeDtypeStruct((n_rows, A_Q_HEADS * HEAD_DIM), BF16),
        grid=(n_batch, n_steps),
        in_specs=[pl.BlockSpec((QB * T, A_Q_HEADS * HEAD_DIM), q_index)] + kv_specs + kv_specs
        + [_const_spec((A_Q_HEADS, LANES)), _const_spec(band.shape)],
        out_specs=pl.BlockSpec((QB * T, A_Q_HEADS * HEAD_DIM), q_index),
        compiler_params=_cparams("arbitrary", "arbitrary"),
        name="attention",
    )(aq, *([akd] * n_kv), *([avd] * n_kv), sink_lanes, band)


def _log_sigmoid(x):
    return jnp.minimum(x, 0.0) - jnp.log(1.0 + jnp.exp(-jnp.abs(x)))


def _lo_head():
    return lax.broadcasted_iota(jnp.int32, (1, LANES), 1) < R_DIM


_ROW_DK_F, _ROW_DK_B, _ROW_DQ_F, _ROW_DQ_B, _ROW_DC = (i * R_CHUNK for i in range(5))
_RET_ROWS = 4 * R_CHUNK + SUBLANES


def _ret_tables_kernel(dec_ref, tt_ref, rows_ref):
    T = R_CHUNK
    lg_f = _log_sigmoid(dec_ref[0:1, :])
    lg_b = _log_sigmoid(dec_ref[1:2, :])
    t = lax.broadcasted_iota(jnp.int32, (T, 1), 0).astype(F32)
    rows_ref[_ROW_DK_F:_ROW_DK_F + T, :] = jnp.exp((T - 1.0 - t) * lg_f)
    rows_ref[_ROW_DK_B:_ROW_DK_B + T, :] = jnp.exp(t * lg_b)
    rows_ref[_ROW_DQ_F:_ROW_DQ_F + T, :] = jnp.exp((t + 1.0) * lg_f)
    rows_ref[_ROW_DQ_B:_ROW_DQ_B + T, :] = jnp.exp((T - t) * lg_b)
    rows_ref[_ROW_DC:_ROW_DC + SUBLANES, :] = jnp.concatenate(
        [jnp.exp(T * lg_f), jnp.exp(T * lg_b), jnp.zeros((SUBLANES - 2, lg_f.shape[1]), F32)], axis=0)
    d = t - lax.broadcasted_iota(jnp.int32, (1, T), 1).astype(F32)
    for h in range(R_HEADS):
        a = lg_f[0:1, h * R_DIM:h * R_DIM + 1]
        b = lg_b[0:1, h * R_DIM:h * R_DIM + 1]
        fwd = jnp.exp(jnp.maximum(d, 0.0) * a)
        bwd = jnp.exp(jnp.maximum(-d, 0.0) * b)
        tt_ref[h * T:(h + 1) * T, :] = jnp.where(d > 0, fwd, jnp.where(d < 0, bwd, 2.0))


def _retention_tables(dec_lanes):
    T = R_CHUNK
    W = dec_lanes.shape[1]
    return pl.pallas_call(
        _ret_tables_kernel,
        out_shape=[jax.ShapeDtypeStruct((R_HEADS * T, T), F32), jax.ShapeDtypeStruct((_RET_ROWS, W), F32)],
        grid=(1,),
        in_specs=[_const_spec((2, W))],
        out_specs=[pl.BlockSpec((R_HEADS * T, T), lambda i: (0, 0)), pl.BlockSpec((_RET_ROWS, W), lambda i: (0, 0))],
        compiler_params=_cparams("arbitrary"),
        name="retention_tables",
    )(dec_lanes)


def _ret_state_kernel(kc_ref, vc_ref, kf_ref, vf_ref, kb_ref, vb_ref, rows_ref,
                      sfc_ref, sbc_ref, sfl_ref, sbl_ref, sf_acc, sb_acc):
    T = R_CHUNK
    lo = _lo_head()
    dkf = rows_ref[_ROW_DK_F:_ROW_DK_F + T, :]
    dkb = rows_ref[_ROW_DK_B:_ROW_DK_B + T, :]
    dcf = rows_ref[_ROW_DC:_ROW_DC + 1, :]
    dcb = rows_ref[_ROW_DC + 1:_ROW_DC + 2, :]

    def chunk(acc, s_ref, k_ref, v_ref, ci, d_k, d_c):
        s_ref[ci] = acc[...].astype(BF16)
        r = slice(ci * T, (ci + 1) * T)
        kd = k_ref[r, :].astype(F32) * d_k
        for p in range(R_HEADS // 2):
            ls = slice(p * LANES, (p + 1) * LANES)
            kv = jnp.dot(kd[:, ls].T.astype(BF16), v_ref[r, ls], preferred_element_type=F32)
            rs = slice(p * R_DIM, (p + 1) * R_DIM)
            acc[rs, :] = acc[rs, :] * d_c[:, ls] + jnp.where(lo, kv[0:R_DIM], kv[R_DIM:])

    def sweep(k_f, v_f, s_f, k_b, v_b, s_b):
        n = s_f.shape[0]
        for ci in range(n):
            chunk(sf_acc, s_f, k_f, v_f, ci, dkf, dcf)
        for ci in reversed(range(n)):
            chunk(sb_acc, s_b, k_b, v_b, ci, dkb, dcb)

    @pl.when(pl.program_id(1) == 0)
    def _():
        sf_acc[...] = jnp.zeros_like(sf_acc)
        sb_acc[...] = jnp.zeros_like(sb_acc)
        sweep(kc_ref, vc_ref, sfc_ref, kc_ref, vc_ref, sbc_ref)

    @pl.when(pl.program_id(1) > 0)
    def _():
        sweep(kf_ref, vf_ref, sfl_ref, kb_ref, vb_ref, sbl_ref)


def _retention_states(rk, rv, tab_rows, *, n_batch, seq_len, ctx_len):
    T = R_CHUNK
    W = R_HEADS * R_DIM
    ncc, nlc = ctx_len // T, seq_len // T
    lg = math.gcd(RET_LG, nlc)
    nlg = nlc // lg
    ctx_rows = pl.BlockSpec((ctx_len, W), lambda b, j: ((n_batch * seq_len) // ctx_len + b, 0))
    fwd = lambda b, j: b * nlg + jnp.maximum(j - 1, 0)
    bwd = lambda b, j: b * nlg + (nlg - 1 - jnp.maximum(j - 1, 0))
    rows_f = pl.BlockSpec((lg * T, W), lambda b, j: (fwd(b, j), 0))
    rows_b = pl.BlockSpec((lg * T, W), lambda b, j: (bwd(b, j), 0))
    st = (W // 2, LANES)
    ctx_state = jax.ShapeDtypeStruct((n_batch * ncc,) + st, BF16)
    lat_state = jax.ShapeDtypeStruct((n_batch * nlc,) + st, BF16)
    sfc, sbc, sfl, sbl = pl.pallas_call(
        _ret_state_kernel,
        out_shape=[ctx_state, ctx_state, lat_state, lat_state],
        grid=(n_batch, 1 + nlg),
        in_specs=[ctx_rows, ctx_rows, rows_f, rows_f, rows_b, rows_b, _const_spec(tab_rows.shape)],
        out_specs=[pl.BlockSpec((ncc,) + st, lambda b, j: (b, 0, 0)),
                   pl.BlockSpec((ncc,) + st, lambda b, j: (b, 0, 0)),
                   pl.BlockSpec((lg,) + st, lambda b, j: (fwd(b, j), 0, 0)),
                   pl.BlockSpec((lg,) + st, lambda b, j: (bwd(b, j), 0, 0))],
        scratch_shapes=[pltpu.VMEM(st, F32), pltpu.VMEM(st, F32)],
        compiler_params=_cparams("arbitrary", "arbitrary"),
        name="retention_states",
    )(rk, rv, rk, rv, rk, rv, tab_rows)
    return jnp.concatenate([sfl, sfc], axis=0), jnp.concatenate([sbl, sbc], axis=0)


def _ret_out_kernel(q_ref, k_ref, v_ref, g_ref, sf_ref, sb_ref, tt_ref, rows_ref, o_ref):
    T = R_CHUNK
    lo = _lo_head()

    def state_block(s):
        s = s.astype(F32)
        return jnp.concatenate([jnp.where(lo, s, 0.0), jnp.where(lo, 0.0, s)], axis=0).astype(BF16)

    for c in range(RET_OUT_CH):
        r = slice(c * T, (c + 1) * T)
        for p in range(R_HEADS // 2):
            ls = slice(p * LANES, (p + 1) * LANES)
            q = q_ref[r, ls]
            qf = q.astype(F32)
            q_stack = jnp.concatenate([jnp.where(lo, qf, 0.0), jnp.where(lo, 0.0, qf)], axis=0).astype(BF16)
            sc = lax.dot_general(q_stack, k_ref[r, ls], (((1,), (1,)), ((), ())), preferred_element_type=F32)
            att = (sc * tt_ref[2 * p * T:(2 * p + 2) * T, :]).astype(BF16)
            oh = jnp.dot(att, v_ref[r, ls], preferred_element_type=F32)
            o = jnp.where(lo, oh[:T], oh[T:])
            rs = slice(p * R_DIM, (p + 1) * R_DIM)
            states = jnp.concatenate([state_block(sf_ref[c, rs, :]), state_block(sb_ref[c, rs, :])], axis=1)
            oi = jnp.dot(q, states, preferred_element_type=F32)
            o = (o + oi[:, :LANES] * rows_ref[_ROW_DQ_F:_ROW_DQ_F + T, ls]
                 + oi[:, LANES:] * rows_ref[_ROW_DQ_B:_ROW_DQ_B + T, ls])
            o2 = o * o
            ms = jnp.where(lo, jnp.sum(jnp.where(lo, o2, 0.0), axis=-1, keepdims=True),
                           jnp.sum(jnp.where(lo, 0.0, o2), axis=-1, keepdims=True)) * (1.0 / R_DIM)
            g = g_ref[r, ls]
            o_ref[r, ls] = (o * lax.rsqrt(ms + EPS) * (g * jax.nn.sigmoid(g))).astype(BF16)


def _retention_out(rq, rk, rv, rg, sf, sb, tab_tt, tab_rows, *, n_rows):
    T = R_CHUNK
    W = R_HEADS * R_DIM
    rows = pl.BlockSpec((RET_OUT_CH * T, W), lambda c: (c, 0))
    st = pl.BlockSpec((RET_OUT_CH, W // 2, LANES), lambda c: (c, 0, 0))
    return pl.pallas_call(
        _ret_out_kernel,
        out_shape=jax.ShapeDtypeStruct((n_rows, W), BF16),
        grid=(n_rows // (RET_OUT_CH * T),),
        in_specs=[rows, rows, rows, rows, st, st, _const_spec(tab_tt.shape), _const_spec(tab_rows.shape)],
        out_specs=rows,
        compiler_params=_cparams("arbitrary"),
        name="retention_out",
    )(rq, rk, rv, rg, sf, sb, tab_tt, tab_rows)


def _hy_in_kernel(x_ref, xp_ref, xn_ref, mod_ref, gpre_ref, w_ref, bin_ref, wsh_ref, bsh_ref,
                  x0_ref, u_ref, *, tm, n_lat, tiles_per_seq, tiles_per_ctx):
    D = x_ref.shape[1]
    H = SUBLANES
    t = pl.program_id(0)
    lat_tiles = n_lat // tm
    pos = jnp.where(t < lat_tiles, t % tiles_per_seq, (t - lat_tiles) % tiles_per_ctx)
    n_pos = jnp.where(t < lat_tiles, tiles_per_seq, tiles_per_ctx)
    xs = jnp.concatenate([xp_ref[...], x_ref[...], xn_ref[...]], axis=0)
    hb = _adaln_in(xs, mod_ref, gpre_ref, 1).astype(BF16)
    keep_prev = jnp.where(pos == 0, 0.0, 1.0)
    keep_next = jnp.where(pos == n_pos - 1, 0.0, 1.0)
    n_ext = tm + 2 * H

    def conv(c):
        cols = slice(c * D, (c + 1) * D)
        z = jnp.dot(hb, w_ref[:, cols], preferred_element_type=F32) + bin_ref[:, cols]
        z = jnp.concatenate([z[0:H] * keep_prev, z[H:H + tm], z[H + tm:] * keep_next], axis=0)
        zm = pltpu.roll(z, 1, 0)[H:H + tm]
        zp = pltpu.roll(z, n_ext - 1, 0)[H:H + tm]
        return (zm * wsh_ref[0:1, cols] + z[H:H + tm] * wsh_ref[1:2, cols] + zp * wsh_ref[2:3, cols]
                + bsh_ref[:, cols])

    x0_ref[...] = conv(0).astype(BF16)
    u_ref[...] = (conv(2) * conv(1)).astype(BF16)


def _hy_in(s, mod, gpre, w_in, b_in, w_sh, b_sh, *, layer, n_rows, geom, ctx_len):
    n_lat, seq_len, _ = geom
    D = s.shape[1]
    tm = HY_TM
    H = SUBLANES
    blocks_per_tile = tm // H
    last_block = s.shape[0] // H - 1
    kern = partial(_hy_in_kernel, tm=tm, n_lat=n_lat, tiles_per_seq=seq_len // tm,
                   tiles_per_ctx=max(ctx_len // tm, 1))
    out = jax.ShapeDtypeStruct((n_rows, D), BF16)
    return pl.pallas_call(
        kern,
        out_shape=[out, out],
        grid=(n_rows // tm,),
        in_specs=[
            pl.BlockSpec((tm, D), lambda t: (t, 0)),
            pl.BlockSpec((H, D), lambda t: (jnp.maximum(t * blocks_per_tile - 1, 0), 0)),
            pl.BlockSpec((H, D), lambda t: (jnp.minimum((t + 1) * blocks_per_tile, last_block), 0)),
            pl.BlockSpec((1, 1, N_MOD, D), _mod_spec(layer, tm, *geom)),
            pl.BlockSpec((1, 1, D), lambda t: (3 * layer + 1, 0, 0), pipeline_mode=pl.Buffered(1)),
            _const_spec((D, 3 * D)),
            _const_spec((1, 3 * D)),
            _const_spec((3, 3 * D)),
            _const_spec((1, 3 * D)),
        ],
        out_specs=[pl.BlockSpec((tm, D), lambda t: (t, 0))] * 2,
        compiler_params=_cparams("arbitrary"),
        name="hyena_in",
    )(s, s, s, mod, gpre, w_in, b_in, w_sh, b_sh)


def _hy_filter_kernel(z_ref, f0_ref, fb0_ref, f1_ref, fb1_ref, f2_ref, fb2_ref, f3_ref, fr_ref, dl_ref,
                      kfb_ref):
    D = dl_ref.shape[1]
    hp = lax.Precision.HIGHEST
    z = z_ref[...]
    fr = fr_ref[...]
    a = jnp.sin(fr * (jnp.dot(z, f0_ref[...], precision=hp, preferred_element_type=F32) + fb0_ref[...]))
    a = jnp.sin(fr * (jnp.dot(a, f1_ref[...], precision=hp, preferred_element_type=F32) + fb1_ref[...]))
    a = jnp.sin(fr * (jnp.dot(a, f2_ref[...], precision=hp, preferred_element_type=F32) + fb2_ref[...]))
    k = jnp.dot(a.astype(BF16), f3_ref[...].astype(BF16), preferred_element_type=F32)
    decay = jnp.exp(-z[:, 0:1] * dl_ref[...])
    k_f = k[:, :D] * decay
    k_b = k[:, D:] * decay
    row = lax.broadcasted_iota(jnp.int32, (z.shape[0], 1), 0) + pl.program_id(0) * z.shape[0]
    k_b = jnp.where(row == 0, 0.0, k_b)
    kfb_ref[0] = k_f.astype(BF16)
    kfb_ref[1] = k_b.astype(BF16)


def _hy_filter(L, f0, fb0, f1, fb1, f2, fb2, f3, freq, D):
    t = np.linspace(0.0, 1.0, L)[:, None]
    bands = (HY_EMB - 1) // 2
    w = 2.0 * math.pi * np.arange(L)[:, None] / L
    f = np.linspace(1e-4, bands - 1, bands)[None]
    z = jnp.asarray(np.concatenate([t, np.cos(f * w), -np.sin(f * w), np.zeros((L, HY_EMB_PAD - HY_EMB))], -1),
                    dtype=F32)
    f0p = jnp.concatenate([f0, jnp.zeros((HY_EMB_PAD - HY_EMB, f0.shape[1]), F32)], 0)
    deltas = jnp.asarray(np.abs(np.linspace(HY_MIN_DECAY, HY_MAX_DECAY, D))[None], dtype=F32)
    tl = min(512, L)
    O = f0.shape[1]
    return pl.pallas_call(
        _hy_filter_kernel,
        out_shape=jax.ShapeDtypeStruct((2, L, D), BF16),
        grid=(L // tl,),
        in_specs=[pl.BlockSpec((tl, HY_EMB_PAD), lambda i: (i, 0)),
                  _const_spec((HY_EMB_PAD, O)), _const_spec((1, O)),
                  _const_spec((O, O)), _const_spec((1, O)),
                  _const_spec((O, O)), _const_spec((1, O)),
                  _const_spec((O, 2 * D)), _const_spec((1, O)), _const_spec((1, D))],
        out_specs=pl.BlockSpec((2, tl, D), lambda i: (0, i, 0)),
        compiler_params=_cparams("arbitrary"),
        name="hyena_filter",
    )(z, f0p, fb0[None], f1, fb1[None], f2, fb2[None], f3, freq[None], deltas)


def _dft_matrices(L, tf):
    k2 = 2 * np.arange(L, dtype=np.int64)[:, None] + 1
    ang = ((k2 * np.arange(L, dtype=np.int64)[None, :]) % (4 * L)) * (2.0 * math.pi / (4 * L))
    cos, nsin = np.cos(ang), -np.sin(ang)
    fwd = np.concatenate([cos.reshape(L // tf, tf, L), nsin.reshape(L // tf, tf, L)], axis=1).reshape(2 * L, L)
    inv = fwd.T * (1.0 / L)
    return jnp.asarray(fwd, dtype=F32).astype(BF16), jnp.asarray(inv, dtype=F32).astype(BF16)


def _spectrum_kernel(w_ref, kfb_ref, o_ref):
    tf = w_ref.shape[0] // 2
    k_f = kfb_ref[0].astype(F32)
    k_b = kfb_ref[1].astype(F32)
    o_ref[0:tf, :] = jnp.dot(w_ref[0:tf, :], (k_f + k_b).astype(BF16), preferred_element_type=F32)
    o_ref[tf:, :] = jnp.dot(w_ref[tf:, :], (k_f - k_b).astype(BF16), preferred_element_type=F32)


def _filter_spectrum(fwd, kfb, tf):
    _, L, D = kfb.shape
    return pl.pallas_call(
        _spectrum_kernel,
        out_shape=jax.ShapeDtypeStruct((2 * L, D), F32),
        grid=(L // tf,),
        in_specs=[pl.BlockSpec((2 * tf, L), lambda j: (j, 0)), _const_spec((2, L, D))],
        out_specs=pl.BlockSpec((2 * tf, D), lambda j: (j, 0)),
        compiler_params=_cparams("arbitrary"),
        name="filter_spectrum",
    )(fwd, kfb)


def _dft_fwd_kernel(w_ref, u_ref, kh_ref, o_ref):
    tf = w_ref.shape[0] // 2
    uh = jnp.dot(w_ref[...], u_ref[...], preferred_element_type=F32)
    ur, ui = uh[:tf], uh[tf:]
    kr, ki = kh_ref[0:tf, :], kh_ref[tf:, :]
    o_ref[0:tf, :] = (ur * kr - ui * ki).astype(BF16)
    o_ref[tf:, :] = (ur * ki + ui * kr).astype(BF16)


def _dft_forward(fwd, u, khat, *, n_batch, L, row_off, tf):
    D = u.shape[1]
    seq0 = row_off // L
    return pl.pallas_call(
        _dft_fwd_kernel,
        out_shape=jax.ShapeDtypeStruct((n_batch * 2 * L, D), BF16),
        grid=(n_batch, L // tf),
        in_specs=[pl.BlockSpec((2 * tf, L), lambda b, j: (j, 0)),
                  pl.BlockSpec((L, D), lambda b, j: (seq0 + b, 0)),
                  pl.BlockSpec((2 * tf, D), lambda b, j: (j, 0))],
        out_specs=pl.BlockSpec((2 * tf, D), lambda b, j: (b * (L // tf) + j, 0)),
        compiler_params=_cparams("arbitrary", "arbitrary"),
        name="dft_forward",
    )(fwd, u, khat)


def _dft_inv_kernel(w_ref, y_ref, u_ref, x0_ref, bias_ref, o_ref, acc):
    kk = pl.program_id(2)

    @pl.when(kk == 0)
    def _():
        acc[...] = jnp.zeros_like(acc)

    acc[...] += jnp.dot(w_ref[...], y_ref[...], preferred_element_type=F32)

    @pl.when(kk == pl.num_programs(2) - 1)
    def _():
        y = acc[...] + u_ref[...].astype(F32) * bias_ref[...]
        o_ref[...] = (x0_ref[...].astype(F32) * y).astype(BF16)


def _dft_inverse(inv, yhat, u, x0, bias, *, n_batch, L, row_off):
    D = u.shape[1]
    tm = min(1024, L)
    tk = min(2048, 2 * L)
    m_tiles = L // tm
    k_tiles = 2 * L // tk
    blk0 = row_off // tm
    rows_in = pl.BlockSpec((tm, D), lambda b, i, kk: (blk0 + b * m_tiles + i, 0))
    return pl.pallas_call(
        _dft_inv_kernel,
        out_shape=jax.ShapeDtypeStruct((n_batch * L, D), BF16),
        grid=(n_batch, m_tiles, k_tiles),
        in_specs=[pl.BlockSpec((tm, tk), lambda b, i, kk: (i, kk)),
                  pl.BlockSpec((tk, D), lambda b, i, kk: (b * k_tiles + kk, 0)),
                  rows_in, rows_in, _const_spec((1, D))],
        out_specs=pl.BlockSpec((tm, D), lambda b, i, kk: (b * m_tiles + i, 0)),
        scratch_shapes=[pltpu.VMEM((tm, D), F32)],
        compiler_params=_cparams("arbitrary", "arbitrary", "arbitrary"),
        name="dft_inverse",
    )(inv, yhat, u, x0, bias)


FFT_N2 = 256
FFT_J = 16
FFT_JB = 64


def _fft_expand_kernel(t_ref, tt_ref, ma_ref, mi_ref):
    J = FFT_J
    R, H1 = t_ref.shape[1], t_ref.shape[2]
    W = H1 * J
    Rp = tt_ref.shape[2]
    hp = lax.Precision.HIGHEST
    col = lax.broadcasted_iota(jnp.int32, (1, W), 1)
    spread = (lax.broadcasted_iota(jnp.int32, (H1, 1), 0) == col // J).astype(F32)
    row_j = lax.broadcasted_iota(jnp.int32, (R, 1), 0) % J
    ma = jnp.dot(t_ref[0], spread, precision=hp, preferred_element_type=F32)
    ma_ref[0] = jnp.where(row_j == col % J, ma, 0.0).astype(BF16)
    rowi = lax.broadcasted_iota(jnp.int32, (W, 1), 0)
    spread_t = (rowi // J == lax.broadcasted_iota(jnp.int32, (1, H1), 1)).astype(F32)
    col_j = lax.broadcasted_iota(jnp.int32, (1, Rp), 1) % J
    mi = jnp.dot(spread_t, tt_ref[0], precision=hp, preferred_element_type=F32)
    mi_ref[0] = jnp.where(rowi % J == col_j, mi, 0.0).astype(BF16)


def _fft_matrices(L):
    N = 2 * L
    N1 = N // FFT_N2
    H1, K1, J, A = N1 // 2, N1 // 2 + 1, FFT_J, FFT_N2 // FFT_J
    R = K1 * 2 * J
    Rp = R + (-R) % LANES
    a = np.arange(A, dtype=np.int64)[:, None, None, None]
    k1 = np.arange(K1, dtype=np.int64)[None, :, None, None]
    j = np.arange(J, dtype=np.int64)[None, None, :, None]
    n1 = np.arange(H1, dtype=np.int64)[None, None, None, :]
    ang = ((k1 * (FFT_N2 * n1 + J * a + j)) % N) * (2.0 * math.pi / N)
    base = np.stack([np.cos(ang), -np.sin(ang)], axis=2)
    kk = np.arange(K1)
    w = np.where((kk == 0) | (kk == N1 // 2), 1.0, 2.0) / N
    t = jnp.asarray(base.reshape(A, R, H1), dtype=F32)
    tt = np.transpose((base * w[None, :, None, None, None]).reshape(A, R, H1), (0, 2, 1))
    tt = jnp.asarray(np.pad(tt, ((0, 0), (0, 0), (0, Rp - R))), dtype=F32)
    ma, mi = pl.pallas_call(
        _fft_expand_kernel,
        out_shape=[jax.ShapeDtypeStruct((A, R, H1 * J), BF16), jax.ShapeDtypeStruct((A, H1 * J, Rp), BF16)],
        grid=(A,),
        in_specs=[pl.BlockSpec((1, R, H1), lambda i: (i, 0, 0)), pl.BlockSpec((1, H1, Rp), lambda i: (i, 0, 0))],
        out_specs=[pl.BlockSpec((1, R, H1 * J), lambda i: (i, 0, 0)),
                   pl.BlockSpec((1, H1 * J, Rp), lambda i: (i, 0, 0))],
        compiler_params=_cparams("arbitrary"),
        name="fft_matrices",
    )(t, tt)
    n2 = np.arange(FFT_N2, dtype=np.int64)
    angf = ((n2[:, None] * n2[None, :]) % FFT_N2) * (2.0 * math.pi / FFT_N2)
    fr, fi = np.cos(angf), -np.sin(angf)
    cat = lambda top, bot: jnp.asarray(np.concatenate([top, bot], axis=0), dtype=F32).astype(BF16)
    return dict(ma=ma, mi=mi, k1=K1, h1=H1,
                mre=cat(fr, fi), mim=cat(-fi, fr), gre=cat(fr, -fi), gim=cat(fi, fr))


def _fft_a_kernel(u_ref, ma_ref, z_ref):
    H1, JB, D = u_ref.shape
    K1 = z_ref.shape[1]
    J = FFT_J
    a = pl.program_id(1)
    for q in range(JB // J):
        rows = slice(q * J, (q + 1) * J)
        data = u_ref[:, rows, :].reshape(H1 * J, D)
        res = jnp.dot(ma_ref[a * (JB // J) + q], data, preferred_element_type=F32)
        z_ref[0, :, :, rows, :] = res.astype(BF16).reshape(K1, 2, J, D)


def _fft_a(u3, mats, *, n_batch):
    D = u3.shape[2]
    K1, H1 = mats["k1"], mats["h1"]
    return pl.pallas_call(
        _fft_a_kernel,
        out_shape=jax.ShapeDtypeStruct((n_batch, K1, 2, FFT_N2, D), BF16),
        grid=(n_batch, FFT_N2 // FFT_JB),
        in_specs=[pl.BlockSpec((H1, FFT_JB, D), lambda b, a: (b, a, 0)), _const_spec(mats["ma"].shape)],
        out_specs=pl.BlockSpec((1, K1, 2, FFT_JB, D), lambda b, a: (b, 0, 0, a, 0)),
        compiler_params=_cparams("arbitrary", "arbitrary"),
        name="fft_stage_a",
    )(u3, mats["ma"])


def _dft256(mre_ref, mim_ref, z_ref, idx):
    n2 = FFT_N2
    x = (jnp.dot(mre_ref[...], z_ref[idx + (0,)], preferred_element_type=F32)
         + jnp.dot(mim_ref[...], z_ref[idx + (1,)], preferred_element_type=F32))
    return x[:n2], x[n2:]


def _fft_spec_kernel(zf_ref, zb_ref, mre_ref, mim_ref, o_ref):
    fr, fi = _dft256(mre_ref, mim_ref, zf_ref, (0, 0))
    br, bi = _dft256(mre_ref, mim_ref, zb_ref, (0, 0))
    o_ref[0, 0] = fr + br
    o_ref[0, 1] = fi - bi


def _fft_spectrum(zfilt, mats):
    _, K1, _, n2, D = zfilt.shape
    blk = (1, 1, 2, n2, D)
    return pl.pallas_call(
        _fft_spec_kernel,
        out_shape=jax.ShapeDtypeStruct((K1, 2, n2, D), F32),
        grid=(K1,),
        in_specs=[pl.BlockSpec(blk, lambda k: (0, k, 0, 0, 0)), pl.BlockSpec(blk, lambda k: (1, k, 0, 0, 0)),
                  _const_spec(mats["mre"].shape), _const_spec(mats["mim"].shape)],
        out_specs=pl.BlockSpec((1, 2, n2, D), lambda k: (k, 0, 0, 0)),
        compiler_params=_cparams("arbitrary"),
        name="fft_filter_spectrum",
    )(zfilt, zfilt, mats["mre"], mats["mim"])


def _fft_mid_kernel(z_ref, kh_ref, mre_ref, mim_ref, gre_ref, gim_ref, o_ref):
    n2 = FFT_N2
    kr, ki = kh_ref[0, 0], kh_ref[0, 1]
    for b in range(z_ref.shape[0]):
        xr, xi = _dft256(mre_ref, mim_ref, z_ref, (b, 0))
        yr = (xr * kr - xi * ki).astype(BF16)
        yi = (xr * ki + xi * kr).astype(BF16)
        zp = (jnp.dot(gre_ref[...], yr, preferred_element_type=F32)
              + jnp.dot(gim_ref[...], yi, preferred_element_type=F32))
        o_ref[b, 0, 0] = zp[:n2].astype(BF16)
        o_ref[b, 0, 1] = zp[n2:].astype(BF16)


def _fft_mid(z, khat, mats):
    n_batch, K1, _, n2, D = z.shape
    nb = math.gcd(4, n_batch)
    blk = pl.BlockSpec((nb, 1, 2, n2, D), lambda k, b: (b, k, 0, 0, 0))
    mat = [_const_spec(mats[m].shape) for m in ("mre", "mim", "gre", "gim")]
    return pl.pallas_call(
        _fft_mid_kernel,
        out_shape=jax.ShapeDtypeStruct(z.shape, BF16),
        grid=(K1, n_batch // nb),
        in_specs=[blk, pl.BlockSpec((1, 2, n2, D), lambda k, b: (k, 0, 0, 0))] + mat,
        out_specs=blk,
        compiler_params=_cparams("arbitrary", "arbitrary"),
        name="fft_stage_b",
    )(z, khat, mats["mre"], mats["mim"], mats["gre"], mats["gim"])


def _fft_a_inv_kernel(zp_ref, mi_ref, u_ref, x0_ref, bias_ref, o_ref):
    H1, JB, D = u_ref.shape
    K1 = zp_ref.shape[1]
    J = FFT_J
    a = pl.program_id(1)
    k_pad = mi_ref.shape[2] - K1 * 2 * J
    for q in range(JB // J):
        rows = slice(q * J, (q + 1) * J)
        zz = zp_ref[0, :, :, rows, :].reshape(K1 * 2 * J, D)
        if k_pad:
            zz = jnp.concatenate([zz, jnp.zeros((k_pad, D), BF16)], axis=0)
        y = jnp.dot(mi_ref[a * (JB // J) + q], zz, preferred_element_type=F32).reshape(H1, J, D)
        y = y + u_ref[:, rows, :].astype(F32) * bias_ref[...]
        o_ref[:, rows, :] = (x0_ref[:, rows, :].astype(F32) * y).astype(BF16)


def _fft_a_inv(zp, mats, u3, x03, bias, *, n_blocks_out):
    n_batch, K1, _, n2, D = zp.shape
    H1 = mats["h1"]
    rows = pl.BlockSpec((H1, FFT_JB, D), lambda b, a: (b, a, 0))
    return pl.pallas_call(
        _fft_a_inv_kernel,
        out_shape=jax.ShapeDtypeStruct((n_blocks_out, n2, D), BF16),
        grid=(n_batch, n2 // FFT_JB),
        in_specs=[pl.BlockSpec((1, K1, 2, FFT_JB, D), lambda b, a: (b, 0, 0, a, 0)),
                  _const_spec(mats["mi"].shape), rows, rows, _const_spec((1, D))],
        out_specs=rows,
        compiler_params=_cparams("arbitrary", "arbitrary"),
        name="fft_stage_a_inv",
    )(zp, mats["mi"], u3, x03, bias)


def kernel(x, c, ctx, c_ctx, w_mod, b_mod, norm_pre, norm_post, ffn_gate, ffn_up, ffn_down,
           mix_w_in, attn_sink, ret_decay, mix_w_out, hy_w_in, hy_b_in, hy_short_w, hy_short_b,
           hy_f0, hy_fb0, hy_f1, hy_fb1, hy_f2, hy_fb2, hy_f3, hy_freq, hy_bias, hy_w_out):
    Bn, L, D = x.shape
    C = ctx.shape[1]
    n_lat = Bn * L
    n_all = n_lat + Bn * C
    geom = (n_lat, L, Bn)
    assert L % min(PROJ_TM, L) == 0 and L % HY_TM == 0 and n_lat % C == 0 and (Bn * C) % min(PROJ_TM, L) == 0
    assert L % min(FFN_TM, L) == 0 and (Bn * C) % min(FFN_TM, L) == 0
    assert C % R_CHUNK == 0 and L % R_CHUNK == 0 and C % HY_TM == 0
    assert n_lat % (RET_OUT_CH * R_CHUNK) == 0 and n_all % (RET_OUT_CH * R_CHUNK) == 0
    assert L % FFT_N2 == 0 and n_all % FFT_N2 == 0
    last_reader = DEPTH - 1 if (DEPTH - 1) % 2 == 0 else DEPTH - 2

    c_all = jnp.concatenate([c, c_ctx[None], jnp.zeros((SUBLANES - (Bn + 1) % SUBLANES, D), F32)], axis=0)
    mod = _modulation(c_all, w_mod, b_mod).reshape(DEPTH, c_all.shape[0], N_MOD, D)

    rope_a = _rope_tables(_axial_angles(L), min(PROJ_TM, L))
    rope_r = _rope_tables(_line_angles(L), min(PROJ_TM, L))
    fft = _fft_matrices(L)

    npre = norm_pre.reshape(DEPTH * 3, 1, D)
    npost = norm_post.reshape(DEPTH * 3, 1, D)
    wg, wu, wd = ffn_gate.astype(BF16), ffn_up.astype(BF16), ffn_down.astype(BF16)
    mix_wo, hy_wo = mix_w_out.astype(BF16), hy_w_out.astype(BF16)

    def ffn(s, l, i, j, n_rows, mix=None):
        return _ffn(s, mod, npre, npost, wg, wu, wd, layer=l, i=i, j=j, n_rows=n_rows, geom=geom, mix=mix)

    s = (x.reshape(n_lat, D), ctx.reshape(Bn * C, D))
    for l in range(DEPTH):
        ctx_live = l <= last_reader
        ctx_full = l < last_reader
        n_in = n_all if ctx_live else n_lat
        n_out = n_all if ctx_full else n_lat
        s = ffn(s, l, 0, 0, n_in)
        if l % 2 == 0:
            e = l // 2
            aq, rq, rg, akd, avd, rk, rv = _even_in(s, mod, npre, _pack_even_w_in(mix_w_in[e]), rope_a, rope_r,
                                                    layer=l, n_rows=n_in, geom=geom)
            sink_lanes = jnp.broadcast_to(attn_sink[e][:, None], (A_Q_HEADS, LANES))
            dec_lanes = jnp.repeat(ret_decay[e], R_DIM, axis=1)
            a = _attention(aq, akd, avd, sink_lanes, n_batch=Bn, seq_len=L, ctx_len=C,
                           with_ctx_queries=ctx_full)
            tab_tt, tab_rows = _retention_tables(dec_lanes)
            sf, sb = _retention_states(rk, rv, tab_rows, n_batch=Bn, seq_len=L, ctx_len=C)
            r = _retention_out(rq, rk, rv, rg, sf, sb, tab_tt, tab_rows, n_rows=n_out)
            mix = ([a, r], mix_wo, e)
        else:
            o = l // 2
            x0, u = _hy_in(s, mod, npre, hy_w_in[o].astype(BF16), hy_b_in[o][None], hy_short_w[o],
                           hy_short_b[o][None], layer=l, n_rows=n_in, geom=geom, ctx_len=C)
            filt = (hy_f0[o], hy_fb0[o], hy_f1[o], hy_fb1[o], hy_f2[o], hy_fb2[o], hy_f3[o], hy_freq[o])
            bias = hy_bias[o][None]
            blocks = lambda arr: arr.reshape(arr.shape[0] // FFT_N2, FFT_N2, D)
            kfb = _hy_filter(L, *filt, D)
            khat = _fft_spectrum(_fft_a(blocks(kfb.reshape(2 * L, D)), fft, n_batch=2), fft)
            zp = _fft_mid(_fft_a(blocks(u), fft, n_batch=Bn), khat, fft)
            yg = _fft_a_inv(zp, fft, blocks(u), blocks(x0), bias, n_blocks_out=n_lat // FFT_N2).reshape(n_lat, D)
            if ctx_full:
                tf = min(DFT_TF, C)
                fwd, inv = _dft_matrices(C, tf)
                khat_c = _filter_spectrum(fwd, _hy_filter(C, *filt, D), tf)
                yhat = _dft_forward(fwd, u, khat_c, n_batch=Bn, L=C, row_off=n_lat, tf=tf)
                yg = (yg, _dft_inverse(inv, yhat, u, x0, bias, n_batch=Bn, L=C, row_off=n_lat))
            mix = ([yg], hy_wo, o)
        s = ffn(s, l, 2, 1, n_out, mix)
    return s[:n_lat].reshape(Bn, L, D)
```

```python
import math
from functools import partial

import jax
import jax.numpy as jnp
import numpy as np
from jax import lax
from jax.experimental import pallas as pl
from jax.experimental.pallas import tpu as pltpu

F32 = jnp.float32
BF16 = jnp.bfloat16

DEPTH = 4
GRID_W = 64
EPS = 1e-6
NEG = -1e30
N_MOD = 9
FFN_RESIDUAL = 0.5
HEAD_DIM = 64
A_Q_HEADS = 8
A_KV_HEADS = 2
WINDOW = 128
BLOCK = 128
ROPE_BASE = 10000.0
R_DIM = 64
R_HEADS = 8
R_CHUNK = 128
HY_EMB = 33
HY_EMB_PAD = 64
HY_MAX_DECAY = math.log(1e-2) / 0.3
HY_MIN_DECAY = math.log(1e-2) / 1.5

LANES = 128
SUBLANES = 8
V7X_VMEM_LIMIT_BYTES = 56 * 1024 * 1024

FFN_TM = 1024
FFN_SUB = 2
FFN_TF = 256
PROJ_TM = 1024
HY_TM = 256
HY_SUB = 2
MOD_TN = 2304
DFT_TF = 256
RET_LG = 8
RET_OUT_CH = 4
ATT_RC = 256
ATT_QB = 2


def _cparams(*sem):
    return pltpu.CompilerParams(dimension_semantics=sem, vmem_limit_bytes=V7X_VMEM_LIMIT_BYTES)


def _const_spec(shape):
    zeros = (0,) * len(shape)
    return pl.BlockSpec(shape, lambda *_: zeros, pipeline_mode=pl.Buffered(1))


def _mod_spec(layer, tm, n_lat, seq_len, n_batch):
    def index(t, *_):
        return (layer, jnp.where(t * tm < n_lat, (t * tm) // seq_len, n_batch), 0, 0)
    return index


def _adaln_in(x, mod_ref, gpre_ref, i):
    shift = mod_ref[0, 0, 3 * i:3 * i + 1, :]
    scale = mod_ref[0, 0, 3 * i + 1:3 * i + 2, :]
    gain = gpre_ref[0] * (1.0 + scale)
    inv = lax.rsqrt(jnp.mean(x * x, axis=-1, keepdims=True) + EPS)
    return (x * inv) * gain + shift


def _adaln_out(x, y, mod_ref, gpost_ref, i, w):
    gate = mod_ref[0, 0, 3 * i + 2:3 * i + 3, :]
    gain = (w * gate) * gpost_ref[0]
    inv = lax.rsqrt(jnp.mean(y * y, axis=-1, keepdims=True) + EPS)
    return x + (y * inv) * gain


def _mod_kernel(c_ref, w_ref, b_ref, o_ref):
    c = c_ref[...]
    a = (c * jax.nn.sigmoid(c)).astype(BF16)
    o_ref[0] = jnp.dot(a, w_ref[0].astype(BF16), preferred_element_type=F32) + b_ref[0]


def _modulation(c_all, w_mod, b_mod):
    depth, D, W = w_mod.shape
    rows = c_all.shape[0]
    return pl.pallas_call(
        _mod_kernel,
        out_shape=jax.ShapeDtypeStruct((depth, rows, W), F32),
        grid=(depth, W // MOD_TN),
        in_specs=[
            pl.BlockSpec((rows, D), lambda l, j: (0, 0)),
            pl.BlockSpec((1, D, MOD_TN), lambda l, j: (l, 0, j)),
            pl.BlockSpec((1, 1, MOD_TN), lambda l, j: (l, 0, j)),
        ],
        out_specs=pl.BlockSpec((1, rows, MOD_TN), lambda l, j: (l, 0, j)),
        compiler_params=_cparams("arbitrary", "arbitrary"),
        name="modulation",
    )(c_all, w_mod, b_mod.reshape(depth, 1, W))


def _ffn_kernel(*refs, i, tf, n_chunks, n_parts, n_sub, split, lat_tiles):
    row_refs, k = [], 0
    for is_split in split:
        row_refs.append(refs[k:k + 2] if is_split else refs[k:k + 1])
        k += 2 if is_split else 1
    mod_ref, npre_ref, npost_ref = refs[k:k + 3]
    rest = refs[k + 3:]
    if n_parts:
        nmix_ref, wo_ref = rest[:2]
        rest = rest[2:]
    wg_ref, wu_ref, wd_ref, o_ref = rest
    is_ctx = (lax.broadcasted_iota(jnp.int32, (1, 1), 0) + pl.program_id(0)) >= lat_tiles

    def read(src, rows):
        if len(src) == 1:
            return src[0][rows, :]
        return jnp.where(is_ctx, src[1][rows, :], src[0][rows, :])

    sub = o_ref.shape[0] // n_sub
    for b in range(n_sub):
        rows = slice(b * sub, (b + 1) * sub)
        x = read(row_refs[-1], rows)
        if n_parts:
            y = None
            row = 0
            for src in row_refs[:-1]:
                width = src[0].shape[1]
                part = jnp.dot(read(src, rows), wo_ref[0, row:row + width, :], preferred_element_type=F32)
                y = part if y is None else y + part
                row += width
            x = _adaln_out(x, y, mod_ref, nmix_ref, 1, 1.0)
        hb = _adaln_in(x, mod_ref, npre_ref, i).astype(BF16)
        acts = []
        for j in range(n_chunks):
            cols = slice(j * tf, (j + 1) * tf)
            g = jnp.dot(hb, wg_ref[0, 0, :, cols], preferred_element_type=F32)
            u = jnp.dot(hb, wu_ref[0, 0, :, cols], preferred_element_type=F32)
            acts.append((g * jax.nn.sigmoid(g) * u).astype(BF16))
        y = jnp.dot(jnp.concatenate(acts, axis=1), wd_ref[0, 0], preferred_element_type=F32)
        o_ref[rows, :] = _adaln_out(x, y, mod_ref, npost_ref, i, FFN_RESIDUAL)


def _ffn(s, mod, npre, npost, wg, wu, wd, *, layer, i, j, n_rows, geom, mix=None):
    n_lat = geom[0]
    D = wd.shape[3]
    d_ff = wd.shape[2]
    tm = min(FFN_TM, geom[1])
    lat_tiles = n_lat // tm
    rows = lambda t: (t, 0)
    norm = lambda sub: pl.BlockSpec((1, 1, D), lambda t: (3 * layer + sub, 0, 0), pipeline_mode=pl.Buffered(1))
    weight = lambda shape: pl.BlockSpec((1, 1) + shape, lambda t: (layer, j, 0, 0), pipeline_mode=pl.Buffered(1))
    parts, extra_specs, extra_args = [], [], []
    if mix is not None:
        parts, w_out, e = mix
        extra_specs = [norm(1), pl.BlockSpec((1,) + w_out.shape[1:], lambda t: (e, 0, 0),
                                             pipeline_mode=pl.Buffered(1))]
        extra_args = [npost, w_out]
    row_specs, row_args, split = [], [], []
    for src in list(parts) + [s]:
        if isinstance(src, tuple):
            lat, ctx = src
            row_specs += [pl.BlockSpec((tm, lat.shape[1]), lambda t: (jnp.minimum(t, lat_tiles - 1), 0)),
                          pl.BlockSpec((tm, ctx.shape[1]), lambda t: (jnp.maximum(t - lat_tiles, 0), 0))]
            row_args += [lat, ctx]
        else:
            row_specs.append(pl.BlockSpec((tm, src.shape[1]), rows))
            row_args.append(src)
        split.append(isinstance(src, tuple))
    return pl.pallas_call(
        partial(_ffn_kernel, i=i, tf=FFN_TF, n_chunks=d_ff // FFN_TF, n_parts=len(parts), n_sub=FFN_SUB,
                split=tuple(split), lat_tiles=lat_tiles),
        out_shape=jax.ShapeDtypeStruct((n_rows, D), F32),
        grid=(n_rows // tm,),
        in_specs=row_specs + [pl.BlockSpec((1, 1, N_MOD, D), _mod_spec(layer, tm, *geom)), norm(i), norm(i)]
        + extra_specs + [weight((D, d_ff)), weight((D, d_ff)), weight((d_ff, D))],
        out_specs=pl.BlockSpec((tm, D), rows),
        compiler_params=_cparams("arbitrary"),
        name="ffn",
    )(*row_args, mod, npre, npost, *extra_args, wg, wu, wd)


_AQ = (0, 512)
_RQ = (512, 1024)
_RG = (1024, 1536)
_AK = (1536, 1792)
_AV = (1792, 2048)
_RK = (2048, 2560)
_RV = (2560, 3072)
EVEN_W = 3072


def _pack_even_w_in(w_in):
    aq, rq, rg = w_in[:, 0:512], w_in[:, 512:1024], w_in[:, 1024:1536]
    ak, av = w_in[:, 1536:1664], w_in[:, 1664:1792]
    rk, rv = w_in[:, 1792:2304], w_in[:, 2304:2816]

    def dup(a):
        g0, g1 = a[:, :HEAD_DIM], a[:, HEAD_DIM:]
        return jnp.concatenate([g0, g0, g1, g1], axis=1)

    return jnp.concatenate([aq, rq, rg, dup(ak), dup(av), rk, rv], axis=1).astype(BF16)


def _rope_tables(ang, ident_rows):
    L = ang.shape[0]
    cos, sin = jnp.cos(ang), jnp.sin(ang)
    zero = jnp.zeros_like(sin)
    cos_t = jnp.tile(cos, (1, 4))
    sin_a = jnp.tile(jnp.concatenate([-sin, zero], axis=1), (1, 2))
    sin_b = jnp.tile(jnp.concatenate([zero, sin], axis=1), (1, 2))
    pad_one = jnp.ones((ident_rows, LANES), F32)
    pad_zero = jnp.zeros((ident_rows, LANES), F32)
    return (jnp.concatenate([cos_t, pad_one], 0), jnp.concatenate([sin_a, pad_zero], 0),
            jnp.concatenate([sin_b, pad_zero], 0))


def _axial_angles(L):
    n_rows = L // GRID_W
    row = jnp.repeat(jnp.arange(n_rows, dtype=F32), GRID_W)
    col = jnp.tile(jnp.arange(GRID_W, dtype=F32), n_rows)
    nf = HEAD_DIM // 4
    inv = ROPE_BASE ** (-jnp.arange(nf, dtype=F32) / nf)
    return jnp.concatenate([row[:, None] * inv, col[:, None] * inv], -1)


def _line_angles(L):
    inv = ROPE_BASE ** (-jnp.linspace(0.0, 1.0, R_DIM // 2, dtype=F32))
    return jnp.arange(L, dtype=F32)[:, None] * inv


def _rope(z, cos, sin_a, sin_b):
    outs = []
    for c in range(z.shape[1] // LANES):
        zc = z[:, c * LANES:(c + 1) * LANES]
        outs.append(zc * cos + pltpu.roll(zc, 96, 1) * sin_a + pltpu.roll(zc, 32, 1) * sin_b)
    return outs[0] if len(outs) == 1 else jnp.concatenate(outs, axis=1)


def _even_in_kernel(x_ref, mod_ref, gpre_ref, w_ref, ca_ref, saa_ref, sab_ref, cr_ref, sra_ref, srb_ref,
                    aq_ref, rq_ref, rg_ref, ak_ref, av_ref, rk_ref, rv_ref):
    hb = _adaln_in(x_ref[...], mod_ref, gpre_ref, 1).astype(BF16)

    def proj(cols):
        return jnp.dot(hb, w_ref[:, cols[0]:cols[1]], preferred_element_type=F32)

    rope_a = (ca_ref[...], saa_ref[...], sab_ref[...])
    rope_r = (cr_ref[...], sra_ref[...], srb_ref[...])
    aq_ref[...] = (_rope(proj(_AQ), *rope_a) * HEAD_DIM ** -0.5).astype(BF16)
    rq_ref[...] = _rope(proj(_RQ), *rope_r).astype(BF16)
    rg_ref[...] = proj(_RG)
    ak_ref[...] = _rope(proj(_AK), *rope_a).astype(BF16)
    av_ref[...] = proj(_AV).astype(BF16)
    rk_ref[...] = (_rope(proj(_RK), *rope_r) * R_DIM ** -0.5).astype(BF16)
    rv_ref[...] = proj(_RV).astype(BF16)


def _even_in(s, mod, gpre, w_pack, rope_a, rope_r, *, layer, n_rows, geom):
    n_lat, seq_len, _ = geom
    D = s.shape[1]
    tm = min(PROJ_TM, seq_len)
    tiles_per_seq = seq_len // tm

    def rope_index(t):
        return (jnp.where(t * tm < n_lat, t % tiles_per_seq, tiles_per_seq), 0)

    rope_spec = pl.BlockSpec((tm, LANES), rope_index)

    def out(width, dtype):
        return jax.ShapeDtypeStruct((n_rows, width), dtype), pl.BlockSpec((tm, width), lambda t: (t, 0))

    outs = [out(512, BF16), out(512, BF16), out(512, F32), out(256, BF16), out(256, BF16),
            out(512, BF16), out(512, BF16)]
    return pl.pallas_call(
        _even_in_kernel,
        out_shape=[o[0] for o in outs],
        grid=(n_rows // tm,),
        in_specs=[
            pl.BlockSpec((tm, D), lambda t: (t, 0)),
            pl.BlockSpec((1, 1, N_MOD, D), _mod_spec(layer, tm, *geom)),
            pl.BlockSpec((1, 1, D), lambda t: (3 * layer + 1, 0, 0), pipeline_mode=pl.Buffered(1)),
            _const_spec((D, EVEN_W)),
        ] + [rope_spec] * 6,
        out_specs=[o[1] for o in outs],
        compiler_params=_cparams("arbitrary"),
        name="even_in",
    )(s, mod, gpre, w_pack, *rope_a, *rope_r)


def _attn_kernel(q_ref, k0_ref, k1_ref, k2_ref, k3_ref, kx_ref, v0_ref, v1_ref, v2_ref, v3_ref, vx_ref,
                 sink_ref, band_ref, o_ref, *, nb):
    T = BLOCK
    j = pl.program_id(1)
    kj = lax.broadcasted_iota(jnp.int32, (1, band_ref.shape[1]), 1)
    lo = lax.broadcasted_iota(jnp.int32, (1, LANES), 1) < HEAD_DIM
    k_refs = (k0_ref, k1_ref, k2_ref, k3_ref)
    v_refs = (v0_ref, v1_ref, v2_ref, v3_ref)
    def scores(i, g):
        n = ATT_QB * j + i
        k_lo = jnp.where(n > 0, 0, T)
        k_hi = jnp.where(n < nb - 1, 3 * T, jnp.where(n < nb, 2 * T, 0))
        exists = ((kj >= k_lo) & (kj < k_hi)) | (kj >= 3 * T)
        bias = band_ref[...] + jnp.where(exists, 0.0, NEG)
        gl = slice(g * LANES, (g + 1) * LANES)
        kd = jnp.concatenate([r[:, gl] for r in k_refs[i:i + 3]] + [kx_ref[:, gl]], axis=0)
        vd = jnp.concatenate([r[:, gl] for r in v_refs[i:i + 3]] + [vx_ref[:, gl]], axis=0)
        qs = []
        for c in range(2):
            cl = slice((2 * g + c) * LANES, (2 * g + c + 1) * LANES)
            qc = q_ref[i * T:(i + 1) * T, cl].astype(F32)
            qs.append(jnp.where(lo, qc, 0.0))
            qs.append(jnp.where(lo, 0.0, qc))
        q_stack = jnp.concatenate(qs, axis=0).astype(BF16)
        s_all = lax.dot_general(q_stack, kd, (((1,), (1,)), ((), ())), preferred_element_type=F32)
        return s_all, vd, bias

    def finish(i, g, s_all, vd, bias):
        ps, inv = [], []
        for rc in range(4 * T // ATT_RC):
            rows = slice(rc * ATT_RC, (rc + 1) * ATT_RC)
            heads = range(4 * g + (rc * ATT_RC) // T, 4 * g + ((rc + 1) * ATT_RC - 1) // T + 1)
            rows_per_head = min(ATT_RC, T)
            sink = jnp.concatenate(
                [jnp.broadcast_to(sink_ref[h:h + 1, 0:1], (rows_per_head, 1)) for h in heads], axis=0)
            s = s_all[rows] + bias[rows]
            m = jnp.maximum(jnp.max(s, axis=-1, keepdims=True), sink)
            p = jnp.exp(s - m)
            inv.append(1.0 / (jnp.sum(p, axis=-1, keepdims=True) + jnp.exp(sink - m)))
            ps.append(p.astype(BF16))
        p = jnp.concatenate(ps, axis=0)
        o = jnp.dot(p, vd, preferred_element_type=F32) * jnp.concatenate(inv, axis=0)
        for c in range(2):
            oc = jnp.where(lo, o[(2 * c) * T:(2 * c + 1) * T], o[(2 * c + 1) * T:(2 * c + 2) * T])
            o_ref[i * T:(i + 1) * T, (2 * g + c) * LANES:(2 * g + c + 1) * LANES] = oc.astype(BF16)

    items = [(i, g) for i in range(ATT_QB) for g in range(A_KV_HEADS)]
    ahead = 2
    pending = [scores(*it) for it in items[:ahead]]
    for k, it in enumerate(items):
        finish(*it, *pending[k])
        if k + ahead < len(items):
            pending.append(scores(*items[k + ahead]))


def _attention(aq, akd, avd, sink_lanes, *, n_batch, seq_len, ctx_len, with_ctx_queries):
    T = BLOCK
    QB = ATT_QB
    nb = seq_len // T
    ncb = ctx_len // T
    n_lat = n_batch * seq_len
    lat_steps, ctx_steps = nb // QB, ncb // QB
    n_steps = lat_steps + (ctx_steps if with_ctx_queries else 0)
    n_rows = n_lat + (n_batch * ctx_len if with_ctx_queries else 0)

    def q_index(b, j):
        return (jnp.where(j < lat_steps, b * lat_steps + j, n_lat // (QB * T) + b * ctx_steps + (j - lat_steps)), 0)

    def kv_index(off):
        def index(b, j):
            return (b * nb + jnp.clip(QB * j + off, 0, nb - 1), 0)
        return index

    def ctx_index(b, j):
        return (n_lat // ctx_len + b, 0)

    kv_specs = [pl.BlockSpec((T, 2 * LANES), kv_index(off)) for off in range(-1, QB + 1)]
    kv_specs.append(pl.BlockSpec((ctx_len, 2 * LANES), ctx_index))
    qi = np.arange(4 * T)[:, None] % T
    kj = np.arange(3 * T + ctx_len)[None, :]
    band = jnp.asarray(np.where((np.abs(kj - T - qi) <= WINDOW) | (kj >= 3 * T), 0.0, NEG), dtype=F32)
    n_kv = len(kv_specs)
    return pl.pallas_call(
        partial(_attn_kernel, nb=nb),
        out_shape=jax.ShapeDtypeStruct((n_rows, A_Q_HEADS * HEAD_DIM), BF16),
        grid=(n_batch, n_steps),
        in_specs=[pl.BlockSpec((QB * T, A_Q_HEADS * HEAD_DIM), q_index)] + kv_specs + kv_specs
        + [_const_spec((A_Q_HEADS, LANES)), _const_spec(band.shape)],
        out_specs=pl.BlockSpec((QB * T, A_Q_HEADS * HEAD_DIM), q_index),
        compiler_params=_cparams("arbitrary", "arbitrary"),
        name="attention",
    )(aq, *([akd] * n_kv), *([avd] * n_kv), sink_lanes, band)


def _log_sigmoid(x):
    return jnp.minimum(x, 0.0) - jnp.log(1.0 + jnp.exp(-jnp.abs(x)))


def _lo_head():
    return lax.broadcasted_iota(jnp.int32, (1, LANES), 1) < R_DIM


_ROW_DK_F, _ROW_DK_B, _ROW_DQ_F, _ROW_DQ_B, _ROW_DC = (i * R_CHUNK for i in range(5))
_RET_ROWS = 4 * R_CHUNK + SUBLANES


def _ret_tables_kernel(dec_ref, tt_ref, rows_ref):
    T = R_CHUNK
    lg_f = _log_sigmoid(dec_ref[0:1, :])
    lg_b = _log_sigmoid(dec_ref[1:2, :])
    t = lax.broadcasted_iota(jnp.int32, (T, 1), 0).astype(F32)
    rows_ref[_ROW_DK_F:_ROW_DK_F + T, :] = jnp.exp((T - 1.0 - t) * lg_f)
    rows_ref[_ROW_DK_B:_ROW_DK_B + T, :] = jnp.exp(t * lg_b)
    rows_ref[_ROW_DQ_F:_ROW_DQ_F + T, :] = jnp.exp((t + 1.0) * lg_f)
    rows_ref[_ROW_DQ_B:_ROW_DQ_B + T, :] = jnp.exp((T - t) * lg_b)
    rows_ref[_ROW_DC:_ROW_DC + SUBLANES, :] = jnp.concatenate(
        [jnp.exp(T * lg_f), jnp.exp(T * lg_b), jnp.zeros((SUBLANES - 2, lg_f.shape[1]), F32)], axis=0)
    d = t - lax.broadcasted_iota(jnp.int32, (1, T), 1).astype(F32)
    for h in range(R_HEADS):
        a = lg_f[0:1, h * R_DIM:h * R_DIM + 1]
        b = lg_b[0:1, h * R_DIM:h * R_DIM + 1]
        fwd = jnp.exp(jnp.maximum(d, 0.0) * a)
        bwd = jnp.exp(jnp.maximum(-d, 0.0) * b)
        tt_ref[h * T:(h + 1) * T, :] = jnp.where(d > 0, fwd, jnp.where(d < 0, bwd, 2.0))


def _retention_tables(dec_lanes):
    T = R_CHUNK
    W = dec_lanes.shape[1]
    return pl.pallas_call(
        _ret_tables_kernel,
        out_shape=[jax.ShapeDtypeStruct((R_HEADS * T, T), F32), jax.ShapeDtypeStruct((_RET_ROWS, W), F32)],
        grid=(1,),
        in_specs=[_const_spec((2, W))],
        out_specs=[pl.BlockSpec((R_HEADS * T, T), lambda i: (0, 0)), pl.BlockSpec((_RET_ROWS, W), lambda i: (0, 0))],
        compiler_params=_cparams("arbitrary"),
        name="retention_tables",
    )(dec_lanes)


def _ret_state_kernel(kc_ref, vc_ref, kf_ref, vf_ref, kb_ref, vb_ref, rows_ref,
                      sfc_ref, sbc_ref, sfl_ref, sbl_ref, sf_acc, sb_acc):
    T = R_CHUNK
    lo = _lo_head()
    dkf = rows_ref[_ROW_DK_F:_ROW_DK_F + T, :]
    dkb = rows_ref[_ROW_DK_B:_ROW_DK_B + T, :]
    dcf = rows_ref[_ROW_DC:_ROW_DC + 1, :]
    dcb = rows_ref[_ROW_DC + 1:_ROW_DC + 2, :]

    def chunk(acc, s_ref, k_ref, v_ref, ci, d_k, d_c):
        s_ref[ci] = acc[...].astype(BF16)
        r = slice(ci * T, (ci + 1) * T)
        kd = k_ref[r, :].astype(F32) * d_k
        for p in range(R_HEADS // 2):
            ls = slice(p * LANES, (p + 1) * LANES)
            kv = jnp.dot(kd[:, ls].T.astype(BF16), v_ref[r, ls], preferred_element_type=F32)
            rs = slice(p * R_DIM, (p + 1) * R_DIM)
            acc[rs, :] = acc[rs, :] * d_c[:, ls] + jnp.where(lo, kv[0:R_DIM], kv[R_DIM:])

    def sweep(k_f, v_f, s_f, k_b, v_b, s_b):
        n = s_f.shape[0]
        for ci in range(n):
            chunk(sf_acc, s_f, k_f, v_f, ci, dkf, dcf)
        for ci in reversed(range(n)):
            chunk(sb_acc, s_b, k_b, v_b, ci, dkb, dcb)

    @pl.when(pl.program_id(1) == 0)
    def _():
        sf_acc[...] = jnp.zeros_like(sf_acc)
        sb_acc[...] = jnp.zeros_like(sb_acc)
        sweep(kc_ref, vc_ref, sfc_ref, kc_ref, vc_ref, sbc_ref)

    @pl.when(pl.program_id(1) > 0)
    def _():
        sweep(kf_ref, vf_ref, sfl_ref, kb_ref, vb_ref, sbl_ref)


def _retention_states(rk, rv, tab_rows, *, n_batch, seq_len, ctx_len):
    T = R_CHUNK
    W = R_HEADS * R_DIM
    ncc, nlc = ctx_len // T, seq_len // T
    lg = math.gcd(RET_LG, nlc)
    nlg = nlc // lg
    ctx_rows = pl.BlockSpec((ctx_len, W), lambda b, j: ((n_batch * seq_len) // ctx_len + b, 0))
    fwd = lambda b, j: b * nlg + jnp.maximum(j - 1, 0)
    bwd = lambda b, j: b * nlg + (nlg - 1 - jnp.maximum(j - 1, 0))
    rows_f = pl.BlockSpec((lg * T, W), lambda b, j: (fwd(b, j), 0))
    rows_b = pl.BlockSpec((lg * T, W), lambda b, j: (bwd(b, j), 0))
    st = (W // 2, LANES)
    ctx_state = jax.ShapeDtypeStruct((n_batch * ncc,) + st, BF16)
    lat_state = jax.ShapeDtypeStruct((n_batch * nlc,) + st, BF16)
    sfc, sbc, sfl, sbl = pl.pallas_call(
        _ret_state_kernel,
        out_shape=[ctx_state, ctx_state, lat_state, lat_state],
        grid=(n_batch, 1 + nlg),
        in_specs=[ctx_rows, ctx_rows, rows_f, rows_f, rows_b, rows_b, _const_spec(tab_rows.shape)],
        out_specs=[pl.BlockSpec((ncc,) + st, lambda b, j: (b, 0, 0)),
                   pl.BlockSpec((ncc,) + st, lambda b, j: (b, 0, 0)),
                   pl.BlockSpec((lg,) + st, lambda b, j: (fwd(b, j), 0, 0)),
                   pl.BlockSpec((lg,) + st, lambda b, j: (bwd(b, j), 0, 0))],
        scratch_shapes=[pltpu.VMEM(st, F32), pltpu.VMEM(st, F32)],
        compiler_params=_cparams("arbitrary", "arbitrary"),
        name="retention_states",
    )(rk, rv, rk, rv, rk, rv, tab_rows)
    return jnp.concatenate([sfl, sfc], axis=0), jnp.concatenate([sbl, sbc], axis=0)


def _ret_out_kernel(q_ref, k_ref, v_ref, g_ref, sf_ref, sb_ref, tt_ref, rows_ref, o_ref):
    T = R_CHUNK
    lo = _lo_head()

    def state_block(s):
        s = s.astype(F32)
        return jnp.concatenate([jnp.where(lo, s, 0.0), jnp.where(lo, 0.0, s)], axis=0).astype(BF16)

    for c in range(RET_OUT_CH):
        r = slice(c * T, (c + 1) * T)
        for p in range(R_HEADS // 2):
            ls = slice(p * LANES, (p + 1) * LANES)
            q = q_ref[r, ls]
            qf = q.astype(F32)
            q_stack = jnp.concatenate([jnp.where(lo, qf, 0.0), jnp.where(lo, 0.0, qf)], axis=0).astype(BF16)
            sc = lax.dot_general(q_stack, k_ref[r, ls], (((1,), (1,)), ((), ())), preferred_element_type=F32)
            att = (sc * tt_ref[2 * p * T:(2 * p + 2) * T, :]).astype(BF16)
            oh = jnp.dot(att, v_ref[r, ls], preferred_element_type=F32)
            o = jnp.where(lo, oh[:T], oh[T:])
            rs = slice(p * R_DIM, (p + 1) * R_DIM)
            states = jnp.concatenate([state_block(sf_ref[c, rs, :]), state_block(sb_ref[c, rs, :])], axis=1)
            oi = jnp.dot(q, states, preferred_element_type=F32)
            o = (o + oi[:, :LANES] * rows_ref[_ROW_DQ_F:_ROW_DQ_F + T, ls]
                 + oi[:, LANES:] * rows_ref[_ROW_DQ_B:_ROW_DQ_B + T, ls])
            o2 = o * o
            ms = jnp.where(lo, jnp.sum(jnp.where(lo, o2, 0.0), axis=-1, keepdims=True),
                           jnp.sum(jnp.where(lo, 0.0, o2), axis=-1, keepdims=True)) * (1.0 / R_DIM)
            g = g_ref[r, ls]
            o_ref[r, ls] = (o * lax.rsqrt(ms + EPS) * (g * jax.nn.sigmoid(g))).astype(BF16)


def _retention_out(rq, rk, rv, rg, sf, sb, tab_tt, tab_rows, *, n_rows):
    T = R_CHUNK
    W = R_HEADS * R_DIM
    rows = pl.BlockSpec((RET_OUT_CH * T, W), lambda c: (c, 0))
    st = pl.BlockSpec((RET_OUT_CH, W // 2, LANES), lambda c: (c, 0, 0))
    return pl.pallas_call(
        _ret_out_kernel,
        out_shape=jax.ShapeDtypeStruct((n_rows, W), BF16),
        grid=(n_rows // (RET_OUT_CH * T),),
        in_specs=[rows, rows, rows, rows, st, st, _const_spec(tab_tt.shape), _const_spec(tab_rows.shape)],
        out_specs=rows,
        compiler_params=_cparams("arbitrary"),
        name="retention_out",
    )(rq, rk, rv, rg, sf, sb, tab_tt, tab_rows)


def _hy_in_kernel(x_ref, xp_ref, xn_ref, mod_ref, gpre_ref, w_ref, bin_ref, wsh_ref, bsh_ref,
                  x0_ref, u_ref, *, tm, n_lat, tiles_per_seq, tiles_per_ctx):
    D = x_ref.shape[1]
    H = SUBLANES
    n_sub = x_ref.shape[0] // tm
    lat_tiles = n_lat // tm
    n_ext = tm + 2 * H
    for i in range(n_sub):
        t = pl.program_id(0) * n_sub + i
        rows = slice(i * tm, (i + 1) * tm)
        pos = jnp.where(t < lat_tiles, t % tiles_per_seq, (t - lat_tiles) % tiles_per_ctx)
        n_pos = jnp.where(t < lat_tiles, tiles_per_seq, tiles_per_ctx)
        prev = xp_ref[...] if i == 0 else x_ref[i * tm - H:i * tm, :]
        nxt = xn_ref[...] if i == n_sub - 1 else x_ref[(i + 1) * tm:(i + 1) * tm + H, :]
        xs = jnp.concatenate([prev, x_ref[rows, :], nxt], axis=0)
        hb = _adaln_in(xs, mod_ref, gpre_ref, 1).astype(BF16)
        keep_prev = jnp.where(pos == 0, 0.0, 1.0)
        keep_next = jnp.where(pos == n_pos - 1, 0.0, 1.0)

        def conv(c):
            cols = slice(c * D, (c + 1) * D)
            z = jnp.dot(hb, w_ref[:, cols], preferred_element_type=F32) + bin_ref[:, cols]
            z = jnp.concatenate([z[0:H] * keep_prev, z[H:H + tm], z[H + tm:] * keep_next], axis=0)
            zm = pltpu.roll(z, 1, 0)[H:H + tm]
            zp = pltpu.roll(z, n_ext - 1, 0)[H:H + tm]
            return (zm * wsh_ref[0:1, cols] + z[H:H + tm] * wsh_ref[1:2, cols] + zp * wsh_ref[2:3, cols]
                    + bsh_ref[:, cols])

        x0_ref[rows, :] = conv(0).astype(BF16)
        u_ref[rows, :] = (conv(2) * conv(1)).astype(BF16)


def _hy_in(s, mod, gpre, w_in, b_in, w_sh, b_sh, *, layer, n_rows, geom, ctx_len):
    n_lat, seq_len, _ = geom
    D = s.shape[1]
    tm = HY_TM
    H = SUBLANES
    step = HY_SUB * tm
    blocks_per_tile = step // H
    last_block = s.shape[0] // H - 1
    kern = partial(_hy_in_kernel, tm=tm, n_lat=n_lat, tiles_per_seq=seq_len // tm,
                   tiles_per_ctx=max(ctx_len // tm, 1))
    out = jax.ShapeDtypeStruct((n_rows, D), BF16)
    return pl.pallas_call(
        kern,
        out_shape=[out, out],
        grid=(n_rows // step,),
        in_specs=[
            pl.BlockSpec((step, D), lambda t: (t, 0)),
            pl.BlockSpec((H, D), lambda t: (jnp.maximum(t * blocks_per_tile - 1, 0), 0)),
            pl.BlockSpec((H, D), lambda t: (jnp.minimum((t + 1) * blocks_per_tile, last_block), 0)),
            pl.BlockSpec((1, 1, N_MOD, D), _mod_spec(layer, step, *geom)),
            pl.BlockSpec((1, 1, D), lambda t: (3 * layer + 1, 0, 0), pipeline_mode=pl.Buffered(1)),
            _const_spec((D, 3 * D)),
            _const_spec((1, 3 * D)),
            _const_spec((3, 3 * D)),
            _const_spec((1, 3 * D)),
        ],
        out_specs=[pl.BlockSpec((step, D), lambda t: (t, 0))] * 2,
        compiler_params=_cparams("arbitrary"),
        name="hyena_in",
    )(s, s, s, mod, gpre, w_in, b_in, w_sh, b_sh)


def _hy_filter_kernel(z_ref, f0_ref, fb0_ref, f1_ref, fb1_ref, f2_ref, fb2_ref, f3_ref, fr_ref, dl_ref,
                      kfb_ref):
    D = dl_ref.shape[1]
    hp = lax.Precision.HIGHEST
    z = z_ref[...]
    fr = fr_ref[...]
    a = jnp.sin(fr * (jnp.dot(z, f0_ref[...], precision=hp, preferred_element_type=F32) + fb0_ref[...]))
    a = jnp.sin(fr * (jnp.dot(a, f1_ref[...], precision=hp, preferred_element_type=F32) + fb1_ref[...]))
    a = jnp.sin(fr * (jnp.dot(a, f2_ref[...], precision=hp, preferred_element_type=F32) + fb2_ref[...]))
    k = jnp.dot(a.astype(BF16), f3_ref[...].astype(BF16), preferred_element_type=F32)
    decay = jnp.exp(-z[:, 0:1] * dl_ref[...])
    k_f = k[:, :D] * decay
    k_b = k[:, D:] * decay
    row = lax.broadcasted_iota(jnp.int32, (z.shape[0], 1), 0) + pl.program_id(0) * z.shape[0]
    k_b = jnp.where(row == 0, 0.0, k_b)
    kfb_ref[0] = k_f.astype(BF16)
    kfb_ref[1] = k_b.astype(BF16)


def _hy_filter(L, f0, fb0, f1, fb1, f2, fb2, f3, freq, D):
    t = np.linspace(0.0, 1.0, L)[:, None]
    bands = (HY_EMB - 1) // 2
    w = 2.0 * math.pi * np.arange(L)[:, None] / L
    f = np.linspace(1e-4, bands - 1, bands)[None]
    z = jnp.asarray(np.concatenate([t, np.cos(f * w), -np.sin(f * w), np.zeros((L, HY_EMB_PAD - HY_EMB))], -1),
                    dtype=F32)
    f0p = jnp.concatenate([f0, jnp.zeros((HY_EMB_PAD - HY_EMB, f0.shape[1]), F32)], 0)
    deltas = jnp.asarray(np.abs(np.linspace(HY_MIN_DECAY, HY_MAX_DECAY, D))[None], dtype=F32)
    tl = min(512, L)
    O = f0.shape[1]
    return pl.pallas_call(
        _hy_filter_kernel,
        out_shape=jax.ShapeDtypeStruct((2, L, D), BF16),
        grid=(L // tl,),
        in_specs=[pl.BlockSpec((tl, HY_EMB_PAD), lambda i: (i, 0)),
                  _const_spec((HY_EMB_PAD, O)), _const_spec((1, O)),
                  _const_spec((O, O)), _const_spec((1, O)),
                  _const_spec((O, O)), _const_spec((1, O)),
                  _const_spec((O, 2 * D)), _const_spec((1, O)), _const_spec((1, D))],
        out_specs=pl.BlockSpec((2, tl, D), lambda i: (0, i, 0)),
        compiler_params=_cparams("arbitrary"),
        name="hyena_filter",
    )(z, f0p, fb0[None], f1, fb1[None], f2, fb2[None], f3, freq[None], deltas)


def _dft_matrices(L, tf):
    k2 = 2 * np.arange(L, dtype=np.int64)[:, None] + 1
    ang = ((k2 * np.arange(L, dtype=np.int64)[None, :]) % (4 * L)) * (2.0 * math.pi / (4 * L))
    cos, nsin = np.cos(ang), -np.sin(ang)
    fwd = np.concatenate([cos.reshape(L // tf, tf, L), nsin.reshape(L // tf, tf, L)], axis=1).reshape(2 * L, L)
    inv = fwd.T * (1.0 / L)
    return jnp.asarray(fwd, dtype=F32).astype(BF16), jnp.asarray(inv, dtype=F32).astype(BF16)


def _spectrum_kernel(w_ref, kfb_ref, o_ref):
    tf = w_ref.shape[0] // 2
    k_f = kfb_ref[0].astype(F32)
    k_b = kfb_ref[1].astype(F32)
    o_ref[0:tf, :] = jnp.dot(w_ref[0:tf, :], (k_f + k_b).astype(BF16), preferred_element_type=F32)
    o_ref[tf:, :] = jnp.dot(w_ref[tf:, :], (k_f - k_b).astype(BF16), preferred_element_type=F32)


def _filter_spectrum(fwd, kfb, tf):
    _, L, D = kfb.shape
    return pl.pallas_call(
        _spectrum_kernel,
        out_shape=jax.ShapeDtypeStruct((2 * L, D), F32),
        grid=(L // tf,),
        in_specs=[pl.BlockSpec((2 * tf, L), lambda j: (j, 0)), _const_spec((2, L, D))],
        out_specs=pl.BlockSpec((2 * tf, D), lambda j: (j, 0)),
        compiler_params=_cparams("arbitrary"),
        name="filter_spectrum",
    )(fwd, kfb)


def _dft_fwd_kernel(w_ref, u_ref, kh_ref, o_ref):
    tf = w_ref.shape[0] // 2
    uh = jnp.dot(w_ref[...], u_ref[...], preferred_element_type=F32)
    ur, ui = uh[:tf], uh[tf:]
    kr, ki = kh_ref[0:tf, :], kh_ref[tf:, :]
    o_ref[0:tf, :] = (ur * kr - ui * ki).astype(BF16)
    o_ref[tf:, :] = (ur * ki + ui * kr).astype(BF16)


def _dft_forward(fwd, u, khat, *, n_batch, L, row_off, tf):
    D = u.shape[1]
    seq0 = row_off // L
    return pl.pallas_call(
        _dft_fwd_kernel,
        out_shape=jax.ShapeDtypeStruct((n_batch * 2 * L, D), BF16),
        grid=(n_batch, L // tf),
        in_specs=[pl.BlockSpec((2 * tf, L), lambda b, j: (j, 0)),
                  pl.BlockSpec((L, D), lambda b, j: (seq0 + b, 0)),
                  pl.BlockSpec((2 * tf, D), lambda b, j: (j, 0))],
        out_specs=pl.BlockSpec((2 * tf, D), lambda b, j: (b * (L // tf) + j, 0)),
        compiler_params=_cparams("arbitrary", "arbitrary"),
        name="dft_forward",
    )(fwd, u, khat)


def _dft_inv_kernel(w_ref, y_ref, u_ref, x0_ref, bias_ref, o_ref, acc):
    kk = pl.program_id(2)

    @pl.when(kk == 0)
    def _():
        acc[...] = jnp.zeros_like(acc)

    acc[...] += jnp.dot(w_ref[...], y_ref[...], preferred_element_type=F32)

    @pl.when(kk == pl.num_programs(2) - 1)
    def _():
        y = acc[...] + u_ref[...].astype(F32) * bias_ref[...]
        o_ref[...] = (x0_ref[...].astype(F32) * y).astype(BF16)


def _dft_inverse(inv, yhat, u, x0, bias, *, n_batch, L, row_off):
    D = u.shape[1]
    tm = min(1024, L)
    tk = min(2048, 2 * L)
    m_tiles = L // tm
    k_tiles = 2 * L // tk
    blk0 = row_off // tm
    rows_in = pl.BlockSpec((tm, D), lambda b, i, kk: (blk0 + b * m_tiles + i, 0))
    return pl.pallas_call(
        _dft_inv_kernel,
        out_shape=jax.ShapeDtypeStruct((n_batch * L, D), BF16),
        grid=(n_batch, m_tiles, k_tiles),
        in_specs=[pl.BlockSpec((tm, tk), lambda b, i, kk: (i, kk)),
                  pl.BlockSpec((tk, D), lambda b, i, kk: (b * k_tiles + kk, 0)),
                  rows_in, rows_in, _const_spec((1, D))],
        out_specs=pl.BlockSpec((tm, D), lambda b, i, kk: (b * m_tiles + i, 0)),
        scratch_shapes=[pltpu.VMEM((tm, D), F32)],
        compiler_params=_cparams("arbitrary", "arbitrary", "arbitrary"),
        name="dft_inverse",
    )(inv, yhat, u, x0, bias)


FFT_N2 = 256
FFT_J = 16
FFT_JB = 64


def _fft_expand_kernel(t_ref, tt_ref, ma_ref, mi_ref):
    J = FFT_J
    R, H1 = t_ref.shape[1], t_ref.shape[2]
    W = H1 * J
    Rp = tt_ref.shape[2]
    hp = lax.Precision.HIGHEST
    col = lax.broadcasted_iota(jnp.int32, (1, W), 1)
    spread = (lax.broadcasted_iota(jnp.int32, (H1, 1), 0) == col // J).astype(F32)
    row_j = lax.broadcasted_iota(jnp.int32, (R, 1), 0) % J
    ma = jnp.dot(t_ref[0], spread, precision=hp, preferred_element_type=F32)
    ma_ref[0] = jnp.where(row_j == col % J, ma, 0.0).astype(BF16)
    rowi = lax.broadcasted_iota(jnp.int32, (W, 1), 0)
    spread_t = (rowi // J == lax.broadcasted_iota(jnp.int32, (1, H1), 1)).astype(F32)
    col_j = lax.broadcasted_iota(jnp.int32, (1, Rp), 1) % J
    mi = jnp.dot(spread_t, tt_ref[0], precision=hp, preferred_element_type=F32)
    mi_ref[0] = jnp.where(rowi % J == col_j, mi, 0.0).astype(BF16)


def _fft_matrices(L):
    N = 2 * L
    N1 = N // FFT_N2
    H1, K1, J, A = N1 // 2, N1 // 2 + 1, FFT_J, FFT_N2 // FFT_J
    R = K1 * 2 * J
    Rp = R + (-R) % LANES
    a = np.arange(A, dtype=np.int64)[:, None, None, None]
    k1 = np.arange(K1, dtype=np.int64)[None, :, None, None]
    j = np.arange(J, dtype=np.int64)[None, None, :, None]
    n1 = np.arange(H1, dtype=np.int64)[None, None, None, :]
    ang = ((k1 * (FFT_N2 * n1 + J * a + j)) % N) * (2.0 * math.pi / N)
    base = np.stack([np.cos(ang), -np.sin(ang)], axis=2)
    kk = np.arange(K1)
    w = np.where((kk == 0) | (kk == N1 // 2), 1.0, 2.0) / N
    t = jnp.asarray(base.reshape(A, R, H1), dtype=F32)
    tt = np.transpose((base * w[None, :, None, None, None]).reshape(A, R, H1), (0, 2, 1))
    tt = jnp.asarray(np.pad(tt, ((0, 0), (0, 0), (0, Rp - R))), dtype=F32)
    ma, mi = pl.pallas_call(
        _fft_expand_kernel,
        out_shape=[jax.ShapeDtypeStruct((A, R, H1 * J), BF16), jax.ShapeDtypeStruct((A, H1 * J, Rp), BF16)],
        grid=(A,),
        in_specs=[pl.BlockSpec((1, R, H1), lambda i: (i, 0, 0)), pl.BlockSpec((1, H1, Rp), lambda i: (i, 0, 0))],
        out_specs=[pl.BlockSpec((1, R, H1 * J), lambda i: (i, 0, 0)),
                   pl.BlockSpec((1, H1 * J, Rp), lambda i: (i, 0, 0))],
        compiler_params=_cparams("arbitrary"),
        name="fft_matrices",
    )(t, tt)
    n2 = np.arange(FFT_N2, dtype=np.int64)
    angf = ((n2[:, None] * n2[None, :]) % FFT_N2) * (2.0 * math.pi / FFT_N2)
    fr, fi = np.cos(angf), -np.sin(angf)
    cat = lambda top, bot: jnp.asarray(np.concatenate([top, bot], axis=0), dtype=F32).astype(BF16)
    return dict(ma=ma, mi=mi, k1=K1, h1=H1,
                mre=cat(fr, fi), mim=cat(-fi, fr), gre=cat(fr, -fi), gim=cat(fi, fr))


def _fft_a_kernel(u_ref, ma_ref, z_ref):
    H1, JB, D = u_ref.shape
    K1 = z_ref.shape[1]
    J = FFT_J
    a = pl.program_id(1)
    for q in range(JB // J):
        rows = slice(q * J, (q + 1) * J)
        data = u_ref[:, rows, :].reshape(H1 * J, D)
        res = jnp.dot(ma_ref[a * (JB // J) + q], data, preferred_element_type=F32)
        z_ref[0, :, :, rows, :] = res.astype(BF16).reshape(K1, 2, J, D)


def _fft_a(u3, mats, *, n_batch):
    D = u3.shape[2]
    K1, H1 = mats["k1"], mats["h1"]
    return pl.pallas_call(
        _fft_a_kernel,
        out_shape=jax.ShapeDtypeStruct((n_batch, K1, 2, FFT_N2, D), BF16),
        grid=(n_batch, FFT_N2 // FFT_JB),
        in_specs=[pl.BlockSpec((H1, FFT_JB, D), lambda b, a: (b, a, 0)), _const_spec(mats["ma"].shape)],
        out_specs=pl.BlockSpec((1, K1, 2, FFT_JB, D), lambda b, a: (b, 0, 0, a, 0)),
        compiler_params=_cparams("arbitrary", "arbitrary"),
        name="fft_stage_a",
    )(u3, mats["ma"])


def _dft256(mre_ref, mim_ref, z_ref, idx):
    n2 = FFT_N2
    x = (jnp.dot(mre_ref[...], z_ref[idx + (0,)], preferred_element_type=F32)
         + jnp.dot(mim_ref[...], z_ref[idx + (1,)], preferred_element_type=F32))
    return x[:n2], x[n2:]


def _fft_spec_kernel(zf_ref, zb_ref, mre_ref, mim_ref, o_ref):
    fr, fi = _dft256(mre_ref, mim_ref, zf_ref, (0, 0))
    br, bi = _dft256(mre_ref, mim_ref, zb_ref, (0, 0))
    o_ref[0, 0] = fr + br
    o_ref[0, 1] = fi - bi


def _fft_spectrum(zfilt, mats):
    _, K1, _, n2, D = zfilt.shape
    blk = (1, 1, 2, n2, D)
    return pl.pallas_call(
        _fft_spec_kernel,
        out_shape=jax.ShapeDtypeStruct((K1, 2, n2, D), F32),
        grid=(K1,),
        in_specs=[pl.BlockSpec(blk, lambda k: (0, k, 0, 0, 0)), pl.BlockSpec(blk, lambda k: (1, k, 0, 0, 0)),
                  _const_spec(mats["mre"].shape), _const_spec(mats["mim"].shape)],
        out_specs=pl.BlockSpec((1, 2, n2, D), lambda k: (k, 0, 0, 0)),
        compiler_params=_cparams("arbitrary"),
        name="fft_filter_spectrum",
    )(zfilt, zfilt, mats["mre"], mats["mim"])


def _fft_mid_kernel(z_ref, kh_ref, mre_ref, mim_ref, gre_ref, gim_ref, o_ref):
    n2 = FFT_N2
    kr, ki = kh_ref[0, 0], kh_ref[0, 1]
    for b in range(z_ref.shape[0]):
        xr, xi = _dft256(mre_ref, mim_ref, z_ref, (b, 0))
        yr = (xr * kr - xi * ki).astype(BF16)
        yi = (xr * ki + xi * kr).astype(BF16)
        zp = (jnp.dot(gre_ref[...], yr, preferred_element_type=F32)
              + jnp.dot(gim_ref[...], yi, preferred_element_type=F32))
        o_ref[b, 0, 0] = zp[:n2].astype(BF16)
        o_ref[b, 0, 1] = zp[n2:].astype(BF16)


def _fft_mid(z, khat, mats):
    n_batch, K1, _, n2, D = z.shape
    nb = math.gcd(4, n_batch)
    blk = pl.BlockSpec((nb, 1, 2, n2, D), lambda k, b: (b, k, 0, 0, 0))
    mat = [_const_spec(mats[m].shape) for m in ("mre", "mim", "gre", "gim")]
    return pl.pallas_call(
        _fft_mid_kernel,
        out_shape=jax.ShapeDtypeStruct(z.shape, BF16),
        grid=(K1, n_batch // nb),
        in_specs=[blk, pl.BlockSpec((1, 2, n2, D), lambda k, b: (k, 0, 0, 0))] + mat,
        out_specs=blk,
        compiler_params=_cparams("arbitrary", "arbitrary"),
        name="fft_stage_b",
    )(z, khat, mats["mre"], mats["mim"], mats["gre"], mats["gim"])


def _fft_a_inv_kernel(zp_ref, mi_ref, u_ref, x0_ref, bias_ref, o_ref):
    H1, JB, D = u_ref.shape
    K1 = zp_ref.shape[1]
    J = FFT_J
    a = pl.program_id(1)
    k_pad = mi_ref.shape[2] - K1 * 2 * J
    for q in range(JB // J):
        rows = slice(q * J, (q + 1) * J)
        zz = zp_ref[0, :, :, rows, :].reshape(K1 * 2 * J, D)
        if k_pad:
            zz = jnp.concatenate([zz, jnp.zeros((k_pad, D), BF16)], axis=0)
        y = jnp.dot(mi_ref[a * (JB // J) + q], zz, preferred_element_type=F32).reshape(H1, J, D)
        y = y + u_ref[:, rows, :].astype(F32) * bias_ref[...]
        o_ref[:, rows, :] = (x0_ref[:, rows, :].astype(F32) * y).astype(BF16)


def _fft_a_inv(zp, mats, u3, x03, bias, *, n_blocks_out):
    n_batch, K1, _, n2, D = zp.shape
    H1 = mats["h1"]
    rows = pl.BlockSpec((H1, FFT_JB, D), lambda b, a: (b, a, 0))
    return pl.pallas_call(
        _fft_a_inv_kernel,
        out_shape=jax.ShapeDtypeStruct((n_blocks_out, n2, D), BF16),
        grid=(n_batch, n2 // FFT_JB),
        in_specs=[pl.BlockSpec((1, K1, 2, FFT_JB, D), lambda b, a: (b, 0, 0, a, 0)),
                  _const_spec(mats["mi"].shape), rows, rows, _const_spec((1, D))],
        out_specs=rows,
        compiler_params=_cparams("arbitrary", "arbitrary"),
        name="fft_stage_a_inv",
    )(zp, mats["mi"], u3, x03, bias)


def kernel(x, c, ctx, c_ctx, w_mod, b_mod, norm_pre, norm_post, ffn_gate, ffn_up, ffn_down,
           mix_w_in, attn_sink, ret_decay, mix_w_out, hy_w_in, hy_b_in, hy_short_w, hy_short_b,
           hy_f0, hy_fb0, hy_f1, hy_fb1, hy_f2, hy_fb2, hy_f3, hy_freq, hy_bias, hy_w_out):
    Bn, L, D = x.shape
    C = ctx.shape[1]
    n_lat = Bn * L
    n_all = n_lat + Bn * C
    geom = (n_lat, L, Bn)
    assert L % min(PROJ_TM, L) == 0 and L % HY_TM == 0 and n_lat % C == 0 and (Bn * C) % min(PROJ_TM, L) == 0
    assert L % min(FFN_TM, L) == 0 and (Bn * C) % min(FFN_TM, L) == 0
    assert C % R_CHUNK == 0 and L % R_CHUNK == 0 and C % HY_TM == 0
    assert n_lat % (RET_OUT_CH * R_CHUNK) == 0 and n_all % (RET_OUT_CH * R_CHUNK) == 0
    assert L % FFT_N2 == 0 and n_all % FFT_N2 == 0
    last_reader = DEPTH - 1 if (DEPTH - 1) % 2 == 0 else DEPTH - 2

    c_all = jnp.concatenate([c, c_ctx[None], jnp.zeros((SUBLANES - (Bn + 1) % SUBLANES, D), F32)], axis=0)
    mod = _modulation(c_all, w_mod, b_mod).reshape(DEPTH, c_all.shape[0], N_MOD, D)

    rope_a = _rope_tables(_axial_angles(L), min(PROJ_TM, L))
    rope_r = _rope_tables(_line_angles(L), min(PROJ_TM, L))
    fft = _fft_matrices(L)

    npre = norm_pre.reshape(DEPTH * 3, 1, D)
    npost = norm_post.reshape(DEPTH * 3, 1, D)
    wg, wu, wd = ffn_gate.astype(BF16), ffn_up.astype(BF16), ffn_down.astype(BF16)
    mix_wo, hy_wo = mix_w_out.astype(BF16), hy_w_out.astype(BF16)

    def ffn(s, l, i, j, n_rows, mix=None):
        return _ffn(s, mod, npre, npost, wg, wu, wd, layer=l, i=i, j=j, n_rows=n_rows, geom=geom, mix=mix)

    s = (x.reshape(n_lat, D), ctx.reshape(Bn * C, D))
    for l in range(DEPTH):
        ctx_live = l <= last_reader
        ctx_full = l < last_reader
        n_in = n_all if ctx_live else n_lat
        n_out = n_all if ctx_full else n_lat
        s = ffn(s, l, 0, 0, n_in)
        if l % 2 == 0:
            e = l // 2
            aq, rq, rg, akd, avd, rk, rv = _even_in(s, mod, npre, _pack_even_w_in(mix_w_in[e]), rope_a, rope_r,
                                                    layer=l, n_rows=n_in, geom=geom)
            sink_lanes = jnp.broadcast_to(attn_sink[e][:, None], (A_Q_HEADS, LANES))
            dec_lanes = jnp.repeat(ret_decay[e], R_DIM, axis=1)
            a = _attention(aq, akd, avd, sink_lanes, n_batch=Bn, seq_len=L, ctx_len=C,
                           with_ctx_queries=ctx_full)
            tab_tt, tab_rows = _retention_tables(dec_lanes)
            sf, sb = _retention_states(rk, rv, tab_rows, n_batch=Bn, seq_len=L, ctx_len=C)
            r = _retention_out(rq, rk, rv, rg, sf, sb, tab_tt, tab_rows, n_rows=n_out)
            mix = ([a, r], mix_wo, e)
        else:
            o = l // 2
            x0, u = _hy_in(s, mod, npre, hy_w_in[o].astype(BF16), hy_b_in[o][None], hy_short_w[o],
                           hy_short_b[o][None], layer=l, n_rows=n_in, geom=geom, ctx_len=C)
            filt = (hy_f0[o], hy_fb0[o], hy_f1[o], hy_fb1[o], hy_f2[o], hy_fb2[o], hy_f3[o], hy_freq[o])
            bias = hy_bias[o][None]
            blocks = lambda arr: arr.reshape(arr.shape[0] // FFT_N2, FFT_N2, D)
            kfb = _hy_filter(L, *filt, D)
            khat = _fft_spectrum(_fft_a(blocks(kfb.reshape(2 * L, D)), fft, n_batch=2), fft)
            zp = _fft_mid(_fft_a(blocks(u), fft, n_batch=Bn), khat, fft)
            yg = _fft_a_inv(zp, fft, blocks(u), blocks(x0), bias, n_blocks_out=n_lat // FFT_N2).reshape(n_lat, D)
            if ctx_full:
                tf = min(DFT_TF, C)
                fwd, inv = _dft_matrices(C, tf)
                khat_c = _filter_spectrum(fwd, _hy_filter(C, *filt, D), tf)
                yhat = _dft_forward(fwd, u, khat_c, n_batch=Bn, L=C, row_off=n_lat, tf=tf)
                yg = (yg, _dft_inverse(inv, yhat, u, x0, bias, n_batch=Bn, L=C, row_off=n_lat))
            mix = ([yg], hy_wo, o)
        s = ffn(s, l, 2, 1, n_out, mix)
    return s[:n_lat].reshape(Bn, L, D)
```

```python
import math
from functools import partial

import jax
import jax.numpy as jnp
import numpy as np
from jax import lax
from jax.experimental import pallas as pl
from jax.experimental.pallas import tpu as pltpu

F32 = jnp.float32
BF16 = jnp.bfloat16

DEPTH = 4
GRID_W = 64
EPS = 1e-6
NEG = -1e30
N_MOD = 9
FFN_RESIDUAL = 0.5
HEAD_DIM = 64
A_Q_HEADS = 8
A_KV_HEADS = 2
WINDOW = 128
BLOCK = 128
ROPE_BASE = 10000.0
R_DIM = 64
R_HEADS = 8
R_CHUNK = 128
HY_EMB = 33
HY_EMB_PAD = 64
HY_MAX_DECAY = math.log(1e-2) / 0.3
HY_MIN_DECAY = math.log(1e-2) / 1.5

LANES = 128
SUBLANES = 8
V7X_VMEM_LIMIT_BYTES = 56 * 1024 * 1024

FFN_TM = 1024
FFN_SUB = 2
FFN_TF = 256
PROJ_TM = 1024
HY_TM = 256
MOD_TN = 2304
DFT_TF = 256
RET_LG = 8
RET_OUT_CH = 4
ATT_RC = 256
ATT_QB = 2


def _cparams(*sem):
    return pltpu.CompilerParams(dimension_semantics=sem, vmem_limit_bytes=V7X_VMEM_LIMIT_BYTES)


def _const_spec(shape):
    zeros = (0,) * len(shape)
    return pl.BlockSpec(shape, lambda *_: zeros, pipeline_mode=pl.Buffered(1))


def _mod_spec(layer, tm, n_lat, seq_len, n_batch):
    def index(t, *_):
        return (layer, jnp.where(t * tm < n_lat, (t * tm) // seq_len, n_batch), 0, 0)
    return index


def _adaln_in(x, mod_ref, gpre_ref, i):
    shift = mod_ref[0, 0, 3 * i:3 * i + 1, :]
    scale = mod_ref[0, 0, 3 * i + 1:3 * i + 2, :]
    gain = gpre_ref[0] * (1.0 + scale)
    inv = lax.rsqrt(jnp.mean(x * x, axis=-1, keepdims=True) + EPS)
    return (x * inv) * gain + shift


def _adaln_out(x, y, mod_ref, gpost_ref, i, w):
    gate = mod_ref[0, 0, 3 * i + 2:3 * i + 3, :]
    gain = (w * gate) * gpost_ref[0]
    inv = lax.rsqrt(jnp.mean(y * y, axis=-1, keepdims=True) + EPS)
    return x + (y * inv) * gain


def _mod_kernel(c_ref, w_ref, b_ref, o_ref):
    c = c_ref[...]
    a = (c * jax.nn.sigmoid(c)).astype(BF16)
    o_ref[0] = jnp.dot(a, w_ref[0].astype(BF16), preferred_element_type=F32) + b_ref[0]


def _modulation(c_all, w_mod, b_mod):
    depth, D, W = w_mod.shape
    rows = c_all.shape[0]
    return pl.pallas_call(
        _mod_kernel,
        out_shape=jax.ShapeDtypeStruct((depth, rows, W), F32),
        grid=(depth, W // MOD_TN),
        in_specs=[
            pl.BlockSpec((rows, D), lambda l, j: (0, 0)),
            pl.BlockSpec((1, D, MOD_TN), lambda l, j: (l, 0, j)),
            pl.BlockSpec((1, 1, MOD_TN), lambda l, j: (l, 0, j)),
        ],
        out_specs=pl.BlockSpec((1, rows, MOD_TN), lambda l, j: (l, 0, j)),
        compiler_params=_cparams("arbitrary", "arbitrary"),
        name="modulation",
    )(c_all, w_mod, b_mod.reshape(depth, 1, W))


def _ffn_kernel(*refs, i, tf, n_chunks, n_parts, n_sub, split, lat_tiles):
    row_refs, k = [], 0
    for is_split in split:
        row_refs.append(refs[k:k + 2] if is_split else refs[k:k + 1])
        k += 2 if is_split else 1
    mod_ref, npre_ref, npost_ref = refs[k:k + 3]
    rest = refs[k + 3:]
    if n_parts:
        nmix_ref, wo_ref = rest[:2]
        rest = rest[2:]
    wg_ref, wu_ref, wd_ref, o_ref = rest
    is_ctx = (lax.broadcasted_iota(jnp.int32, (1, 1), 0) + pl.program_id(0)) >= lat_tiles

    def read(src, rows):
        if len(src) == 1:
            return src[0][rows, :]
        return jnp.where(is_ctx, src[1][rows, :], src[0][rows, :])

    sub = o_ref.shape[0] // n_sub
    for b in range(n_sub):
        rows = slice(b * sub, (b + 1) * sub)
        x = read(row_refs[-1], rows)
        if n_parts:
            y = None
            row = 0
            for src in row_refs[:-1]:
                width = src[0].shape[1]
                part = jnp.dot(read(src, rows), wo_ref[0, row:row + width, :], preferred_element_type=F32)
                y = part if y is None else y + part
                row += width
            x = _adaln_out(x, y, mod_ref, nmix_ref, 1, 1.0)
        hb = _adaln_in(x, mod_ref, npre_ref, i).astype(BF16)
        acts = []
        for j in range(n_chunks):
            cols = slice(j * tf, (j + 1) * tf)
            g = jnp.dot(hb, wg_ref[0, 0, :, cols], preferred_element_type=F32)
            u = jnp.dot(hb, wu_ref[0, 0, :, cols], preferred_element_type=F32)
            acts.append((g * jax.nn.sigmoid(g) * u).astype(BF16))
        y = jnp.dot(jnp.concatenate(acts, axis=1), wd_ref[0, 0], preferred_element_type=F32)
        o_ref[rows, :] = _adaln_out(x, y, mod_ref, npost_ref, i, FFN_RESIDUAL)


def _ffn(s, mod, npre, npost, wg, wu, wd, *, layer, i, j, n_rows, geom, mix=None):
    n_lat = geom[0]
    D = wd.shape[3]
    d_ff = wd.shape[2]
    tm = min(FFN_TM, geom[1])
    lat_tiles = n_lat // tm
    rows = lambda t: (t, 0)
    norm = lambda sub: pl.BlockSpec((1, 1, D), lambda t: (3 * layer + sub, 0, 0), pipeline_mode=pl.Buffered(1))
    weight = lambda shape: pl.BlockSpec((1, 1) + shape, lambda t: (layer, j, 0, 0), pipeline_mode=pl.Buffered(1))
    parts, extra_specs, extra_args = [], [], []
    if mix is not None:
        parts, w_out, e = mix
        extra_specs = [norm(1), pl.BlockSpec((1,) + w_out.shape[1:], lambda t: (e, 0, 0),
                                             pipeline_mode=pl.Buffered(1))]
        extra_args = [npost, w_out]
    row_specs, row_args, split = [], [], []
    for src in list(parts) + [s]:
        if isinstance(src, tuple):
            lat, ctx = src
            row_specs += [pl.BlockSpec((tm, lat.shape[1]), lambda t: (jnp.minimum(t, lat_tiles - 1), 0)),
                          pl.BlockSpec((tm, ctx.shape[1]), lambda t: (jnp.maximum(t - lat_tiles, 0), 0))]
            row_args += [lat, ctx]
        else:
            row_specs.append(pl.BlockSpec((tm, src.shape[1]), rows))
            row_args.append(src)
        split.append(isinstance(src, tuple))
    return pl.pallas_call(
        partial(_ffn_kernel, i=i, tf=FFN_TF, n_chunks=d_ff // FFN_TF, n_parts=len(parts), n_sub=FFN_SUB,
                split=tuple(split), lat_tiles=lat_tiles),
        out_shape=jax.ShapeDtypeStruct((n_rows, D), F32),
        grid=(n_rows // tm,),
        in_specs=row_specs + [pl.BlockSpec((1, 1, N_MOD, D), _mod_spec(layer, tm, *geom)), norm(i), norm(i)]
        + extra_specs + [weight((D, d_ff)), weight((D, d_ff)), weight((d_ff, D))],
        out_specs=pl.BlockSpec((tm, D), rows),
        compiler_params=_cparams("arbitrary"),
        name="ffn",
    )(*row_args, mod, npre, npost, *extra_args, wg, wu, wd)


_AQ = (0, 512)
_RQ = (512, 1024)
_RG = (1024, 1536)
_AK = (1536, 1792)
_AV = (1792, 2048)
_RK = (2048, 2560)
_RV = (2560, 3072)
EVEN_W = 3072


def _pack_even_w_in(w_in):
    aq, rq, rg = w_in[:, 0:512], w_in[:, 512:1024], w_in[:, 1024:1536]
    ak, av = w_in[:, 1536:1664], w_in[:, 1664:1792]
    rk, rv = w_in[:, 1792:2304], w_in[:, 2304:2816]

    def dup(a):
        g0, g1 = a[:, :HEAD_DIM], a[:, HEAD_DIM:]
        return jnp.concatenate([g0, g0, g1, g1], axis=1)

    return jnp.concatenate([aq, rq, rg, dup(ak), dup(av), rk, rv], axis=1).astype(BF16)


def _rope_tables(ang, ident_rows):
    L = ang.shape[0]
    cos, sin = jnp.cos(ang), jnp.sin(ang)
    zero = jnp.zeros_like(sin)
    cos_t = jnp.tile(cos, (1, 4))
    sin_a = jnp.tile(jnp.concatenate([-sin, zero], axis=1), (1, 2))
    sin_b = jnp.tile(jnp.concatenate([zero, sin], axis=1), (1, 2))
    pad_one = jnp.ones((ident_rows, LANES), F32)
    pad_zero = jnp.zeros((ident_rows, LANES), F32)
    return (jnp.concatenate([cos_t, pad_one], 0), jnp.concatenate([sin_a, pad_zero], 0),
            jnp.concatenate([sin_b, pad_zero], 0))


def _axial_angles(L):
    n_rows = L // GRID_W
    row = jnp.repeat(jnp.arange(n_rows, dtype=F32), GRID_W)
    col = jnp.tile(jnp.arange(GRID_W, dtype=F32), n_rows)
    nf = HEAD_DIM // 4
    inv = ROPE_BASE ** (-jnp.arange(nf, dtype=F32) / nf)
    return jnp.concatenate([row[:, None] * inv, col[:, None] * inv], -1)


def _line_angles(L):
    inv = ROPE_BASE ** (-jnp.linspace(0.0, 1.0, R_DIM // 2, dtype=F32))
    return jnp.arange(L, dtype=F32)[:, None] * inv


def _rope(z, cos, sin_a, sin_b):
    outs = []
    for c in range(z.shape[1] // LANES):
        zc = z[:, c * LANES:(c + 1) * LANES]
        outs.append(zc * cos + pltpu.roll(zc, 96, 1) * sin_a + pltpu.roll(zc, 32, 1) * sin_b)
    return outs[0] if len(outs) == 1 else jnp.concatenate(outs, axis=1)


def _even_in_kernel(x_ref, mod_ref, gpre_ref, w_ref, ca_ref, saa_ref, sab_ref, cr_ref, sra_ref, srb_ref,
                    aq_ref, rq_ref, rg_ref, ak_ref, av_ref, rk_ref, rv_ref):
    hb = _adaln_in(x_ref[...], mod_ref, gpre_ref, 1).astype(BF16)

    def proj(cols):
        return jnp.dot(hb, w_ref[:, cols[0]:cols[1]], preferred_element_type=F32)

    rope_a = (ca_ref[...], saa_ref[...], sab_ref[...])
    rope_r = (cr_ref[...], sra_ref[...], srb_ref[...])
    aq_ref[...] = (_rope(proj(_AQ), *rope_a) * HEAD_DIM ** -0.5).astype(BF16)
    rq_ref[...] = _rope(proj(_RQ), *rope_r).astype(BF16)
    rg_ref[...] = proj(_RG)
    ak_ref[...] = _rope(proj(_AK), *rope_a).astype(BF16)
    av_ref[...] = proj(_AV).astype(BF16)
    rk_ref[...] = (_rope(proj(_RK), *rope_r) * R_DIM ** -0.5).astype(BF16)
    rv_ref[...] = proj(_RV).astype(BF16)


def _even_in(s, mod, gpre, w_pack, rope_a, rope_r, *, layer, n_rows, geom):
    n_lat, seq_len, _ = geom
    D = s.shape[1]
    tm = min(PROJ_TM, seq_len)
    tiles_per_seq = seq_len // tm

    def rope_index(t):
        return (jnp.where(t * tm < n_lat, t % tiles_per_seq, tiles_per_seq), 0)

    rope_spec = pl.BlockSpec((tm, LANES), rope_index)

    def out(width, dtype):
        return jax.ShapeDtypeStruct((n_rows, width), dtype), pl.BlockSpec((tm, width), lambda t: (t, 0))

    outs = [out(512, BF16), out(512, BF16), out(512, F32), out(256, BF16), out(256, BF16),
            out(512, BF16), out(512, BF16)]
    return pl.pallas_call(
        _even_in_kernel,
        out_shape=[o[0] for o in outs],
        grid=(n_rows // tm,),
        in_specs=[
            pl.BlockSpec((tm, D), lambda t: (t, 0)),
            pl.BlockSpec((1, 1, N_MOD, D), _mod_spec(layer, tm, *geom)),
            pl.BlockSpec((1, 1, D), lambda t: (3 * layer + 1, 0, 0), pipeline_mode=pl.Buffered(1)),
            _const_spec((D, EVEN_W)),
        ] + [rope_spec] * 6,
        out_specs=[o[1] for o in outs],
        compiler_params=_cparams("arbitrary"),
        name="even_in",
    )(s, mod, gpre, w_pack, *rope_a, *rope_r)


def _attn_kernel(q_ref, k0_ref, k1_ref, k2_ref, k3_ref, kx_ref, v0_ref, v1_ref, v2_ref, v3_ref, vx_ref,
                 sink_ref, band_ref, o_ref, *, nb):
    T = BLOCK
    j = pl.program_id(1)
    kj = lax.broadcasted_iota(jnp.int32, (1, band_ref.shape[1]), 1)
    lo = lax.broadcasted_iota(jnp.int32, (1, LANES), 1) < HEAD_DIM
    k_refs = (k0_ref, k1_ref, k2_ref, k3_ref)
    v_refs = (v0_ref, v1_ref, v2_ref, v3_ref)
    def scores(i, g):
        n = ATT_QB * j + i
        k_lo = jnp.where(n > 0, 0, T)
        k_hi = jnp.where(n < nb - 1, 3 * T, jnp.where(n < nb, 2 * T, 0))
        exists = ((kj >= k_lo) & (kj < k_hi)) | (kj >= 3 * T)
        bias = band_ref[...] + jnp.where(exists, 0.0, NEG)
        gl = slice(g * LANES, (g + 1) * LANES)
        kd = jnp.concatenate([r[:, gl] for r in k_refs[i:i + 3]] + [kx_ref[:, gl]], axis=0)
        vd = jnp.concatenate([r[:, gl] for r in v_refs[i:i + 3]] + [vx_ref[:, gl]], axis=0)
        qs = []
        for c in range(2):
            cl = slice((2 * g + c) * LANES, (2 * g + c + 1) * LANES)
            qc = q_ref[i * T:(i + 1) * T, cl].astype(F32)
            qs.append(jnp.where(lo, qc, 0.0))
            qs.append(jnp.where(lo, 0.0, qc))
        q_stack = jnp.concatenate(qs, axis=0).astype(BF16)
        s_all = lax.dot_general(q_stack, kd, (((1,), (1,)), ((), ())), preferred_element_type=F32)
        return s_all, vd, bias

    def finish(i, g, s_all, vd, bias):
        ps, inv = [], []
        for rc in range(4 * T // ATT_RC):
            rows = slice(rc * ATT_RC, (rc + 1) * ATT_RC)
            heads = range(4 * g + (rc * ATT_RC) // T, 4 * g + ((rc + 1) * ATT_RC - 1) // T + 1)
            rows_per_head = min(ATT_RC, T)
            sink = jnp.concatenate(
                [jnp.broadcast_to(sink_ref[h:h + 1, 0:1], (rows_per_head, 1)) for h in heads], axis=0)
            s = s_all[rows] + bias[rows]
            m = jnp.maximum(jnp.max(s, axis=-1, keepdims=True), sink)
            p = jnp.exp(s - m)
            inv.append(1.0 / (jnp.sum(p, axis=-1, keepdims=True) + jnp.exp(sink - m)))
            ps.append(p.astype(BF16))
        p = jnp.concatenate(ps, axis=0)
        o = jnp.dot(p, vd, preferred_element_type=F32) * jnp.concatenate(inv, axis=0)
        for c in range(2):
            oc = jnp.where(lo, o[(2 * c) * T:(2 * c + 1) * T], o[(2 * c + 1) * T:(2 * c + 2) * T])
            o_ref[i * T:(i + 1) * T, (2 * g + c) * LANES:(2 * g + c + 1) * LANES] = oc.astype(BF16)

    items = [(i, g) for i in range(ATT_QB) for g in range(A_KV_HEADS)]
    ahead = 2
    pending = [scores(*it) for it in items[:ahead]]
    for k, it in enumerate(items):
        finish(*it, *pending[k])
        if k + ahead < len(items):
            pending.append(scores(*items[k + ahead]))


def _attention(aq, akd, avd, sink_lanes, *, n_batch, seq_len, ctx_len, with_ctx_queries):
    T = BLOCK
    QB = ATT_QB
    nb = seq_len // T
    ncb = ctx_len // T
    n_lat = n_batch * seq_len
    lat_steps, ctx_steps = nb // QB, ncb // QB
    n_steps = lat_steps + (ctx_steps if with_ctx_queries else 0)
    n_rows = n_lat + (n_batch * ctx_len if with_ctx_queries else 0)

    def q_index(b, j):
        return (jnp.where(j < lat_steps, b * lat_steps + j, n_lat // (QB * T) + b * ctx_steps + (j - lat_steps)), 0)

    def kv_index(off):
        def index(b, j):
            return (b * nb + jnp.clip(QB * j + off, 0, nb - 1), 0)
        return index

    def ctx_index(b, j):
        return (n_lat // ctx_len + b, 0)

    kv_specs = [pl.BlockSpec((T, 2 * LANES), kv_index(off)) for off in range(-1, QB + 1)]
    kv_specs.append(pl.BlockSpec((ctx_len, 2 * LANES), ctx_index))
    qi = np.arange(4 * T)[:, None] % T
    kj = np.arange(3 * T + ctx_len)[None, :]
    band = jnp.asarray(np.where((np.abs(kj - T - qi) <= WINDOW) | (kj >= 3 * T), 0.0, NEG), dtype=F32)
    n_kv = len(kv_specs)
    return pl.pallas_call(
        partial(_attn_kernel, nb=nb),
        out_shape=jax.ShapeDtypeStruct((n_rows, A_Q_HEADS * HEAD_DIM), BF16),
        grid=(n_batch, n_steps),
        in_specs=[pl.BlockSpec((QB * T, A_Q_HEADS * HEAD_DIM), q_index)] + kv_specs + kv_specs
        + [_const_spec((A_Q_HEADS, LANES)), _const_spec(band.shape)],
        out_specs=pl.BlockSpec((QB * T, A_Q_HEADS * HEAD_DIM), q_index),
        compiler_params=_cparams("arbitrary", "arbitrary"),
        name="attention",
    )(aq, *([akd] * n_kv), *([avd] * n_kv), sink_lanes, band)


def _log_sigmoid(x):
    return jnp.minimum(x, 0.0) - jnp.log(1.0 + jnp.exp(-jnp.abs(x)))


def _lo_head():
    return lax.broadcasted_iota(jnp.int32, (1, LANES), 1) < R_DIM


_ROW_DK_F, _ROW_DK_B, _ROW_DQ_F, _ROW_DQ_B, _ROW_DC = (i * R_CHUNK for i in range(5))
_RET_ROWS = 4 * R_CHUNK + SUBLANES


def _ret_tables_kernel(dec_ref, tt_ref, rows_ref):
    T = R_CHUNK
    lg_f = _log_sigmoid(dec_ref[0:1, :])
    lg_b = _log_sigmoid(dec_ref[1:2, :])
    t = lax.broadcasted_iota(jnp.int32, (T, 1), 0).astype(F32)
    rows_ref[_ROW_DK_F:_ROW_DK_F + T, :] = jnp.exp((T - 1.0 - t) * lg_f)
    rows_ref[_ROW_DK_B:_ROW_DK_B + T, :] = jnp.exp(t * lg_b)
    rows_ref[_ROW_DQ_F:_ROW_DQ_F + T, :] = jnp.exp((t + 1.0) * lg_f)
    rows_ref[_ROW_DQ_B:_ROW_DQ_B + T, :] = jnp.exp((T - t) * lg_b)
    rows_ref[_ROW_DC:_ROW_DC + SUBLANES, :] = jnp.concatenate(
        [jnp.exp(T * lg_f), jnp.exp(T * lg_b), jnp.zeros((SUBLANES - 2, lg_f.shape[1]), F32)], axis=0)
    d = t - lax.broadcasted_iota(jnp.int32, (1, T), 1).astype(F32)
    for h in range(R_HEADS):
        a = lg_f[0:1, h * R_DIM:h * R_DIM + 1]
        b = lg_b[0:1, h * R_DIM:h * R_DIM + 1]
        fwd = jnp.exp(jnp.maximum(d, 0.0) * a)
        bwd = jnp.exp(jnp.maximum(-d, 0.0) * b)
        tt_ref[h * T:(h + 1) * T, :] = jnp.where(d > 0, fwd, jnp.where(d < 0, bwd, 2.0))


def _retention_tables(dec_lanes):
    T = R_CHUNK
    W = dec_lanes.shape[1]
    return pl.pallas_call(
        _ret_tables_kernel,
        out_shape=[jax.ShapeDtypeStruct((R_HEADS * T, T), F32), jax.ShapeDtypeStruct((_RET_ROWS, W), F32)],
        grid=(1,),
        in_specs=[_const_spec((2, W))],
        out_specs=[pl.BlockSpec((R_HEADS * T, T), lambda i: (0, 0)), pl.BlockSpec((_RET_ROWS, W), lambda i: (0, 0))],
        compiler_params=_cparams("arbitrary"),
        name="retention_tables",
    )(dec_lanes)


def _ret_state_kernel(kc_ref, vc_ref, kf_ref, vf_ref, kb_ref, vb_ref, rows_ref,
                      sfc_ref, sbc_ref, sfl_ref, sbl_ref, sf_acc, sb_acc):
    T = R_CHUNK
    lo = _lo_head()
    dkf = rows_ref[_ROW_DK_F:_ROW_DK_F + T, :]
    dkb = rows_ref[_ROW_DK_B:_ROW_DK_B + T, :]
    dcf = rows_ref[_ROW_DC:_ROW_DC + 1, :]
    dcb = rows_ref[_ROW_DC + 1:_ROW_DC + 2, :]

    def chunk(acc, s_ref, k_ref, v_ref, ci, d_k, d_c):
        s_ref[ci] = acc[...].astype(BF16)
        r = slice(ci * T, (ci + 1) * T)
        kd = k_ref[r, :].astype(F32) * d_k
        for p in range(R_HEADS // 2):
            ls = slice(p * LANES, (p + 1) * LANES)
            kv = jnp.dot(kd[:, ls].T.astype(BF16), v_ref[r, ls], preferred_element_type=F32)
            rs = slice(p * R_DIM, (p + 1) * R_DIM)
            acc[rs, :] = acc[rs, :] * d_c[:, ls] + jnp.where(lo, kv[0:R_DIM], kv[R_DIM:])

    def sweep(k_f, v_f, s_f, k_b, v_b, s_b):
        n = s_f.shape[0]
        for ci in range(n):
            chunk(sf_acc, s_f, k_f, v_f, ci, dkf, dcf)
        for ci in reversed(range(n)):
            chunk(sb_acc, s_b, k_b, v_b, ci, dkb, dcb)

    @pl.when(pl.program_id(1) == 0)
    def _():
        sf_acc[...] = jnp.zeros_like(sf_acc)
        sb_acc[...] = jnp.zeros_like(sb_acc)
        sweep(kc_ref, vc_ref, sfc_ref, kc_ref, vc_ref, sbc_ref)

    @pl.when(pl.program_id(1) > 0)
    def _():
        sweep(kf_ref, vf_ref, sfl_ref, kb_ref, vb_ref, sbl_ref)


def _retention_states(rk, rv, tab_rows, *, n_batch, seq_len, ctx_len):
    T = R_CHUNK
    W = R_HEADS * R_DIM
    ncc, nlc = ctx_len // T, seq_len // T
    lg = math.gcd(RET_LG, nlc)
    nlg = nlc // lg
    ctx_rows = pl.BlockSpec((ctx_len, W), lambda b, j: ((n_batch * seq_len) // ctx_len + b, 0))
    fwd = lambda b, j: b * nlg + jnp.maximum(j - 1, 0)
    bwd = lambda b, j: b * nlg + (nlg - 1 - jnp.maximum(j - 1, 0))
    rows_f = pl.BlockSpec((lg * T, W), lambda b, j: (fwd(b, j), 0))
    rows_b = pl.BlockSpec((lg * T, W), lambda b, j: (bwd(b, j), 0))
    st = (W // 2, LANES)
    ctx_state = jax.ShapeDtypeStruct((n_batch * ncc,) + st, BF16)
    lat_state = jax.ShapeDtypeStruct((n_batch * nlc,) + st, BF16)
    sfc, sbc, sfl, sbl = pl.pallas_call(
        _ret_state_kernel,
        out_shape=[ctx_state, ctx_state, lat_state, lat_state],
        grid=(n_batch, 1 + nlg),
        in_specs=[ctx_rows, ctx_rows, rows_f, rows_f, rows_b, rows_b, _const_spec(tab_rows.shape)],
        out_specs=[pl.BlockSpec((ncc,) + st, lambda b, j: (b, 0, 0)),
                   pl.BlockSpec((ncc,) + st, lambda b, j: (b, 0, 0)),
                   pl.BlockSpec((lg,) + st, lambda b, j: (fwd(b, j), 0, 0)),
                   pl.BlockSpec((lg,) + st, lambda b, j: (bwd(b, j), 0, 0))],
        scratch_shapes=[pltpu.VMEM(st, F32), pltpu.VMEM(st, F32)],
        compiler_params=_cparams("arbitrary", "arbitrary"),
        name="retention_states",
    )(rk, rv, rk, rv, rk, rv, tab_rows)
    return jnp.concatenate([sfl, sfc], axis=0), jnp.concatenate([sbl, sbc], axis=0)


def _ret_out_kernel(q_ref, k_ref, v_ref, g_ref, sf_ref, sb_ref, tt_ref, rows_ref, o_ref):
    T = R_CHUNK
    lo = _lo_head()

    def state_block(s):
        s = s.astype(F32)
        return jnp.concatenate([jnp.where(lo, s, 0.0), jnp.where(lo, 0.0, s)], axis=0).astype(BF16)

    for c in range(RET_OUT_CH):
        r = slice(c * T, (c + 1) * T)
        for p in range(R_HEADS // 2):
            ls = slice(p * LANES, (p + 1) * LANES)
            q = q_ref[r, ls]
            qf = q.astype(F32)
            q_stack = jnp.concatenate([jnp.where(lo, qf, 0.0), jnp.where(lo, 0.0, qf)], axis=0).astype(BF16)
            sc = lax.dot_general(q_stack, k_ref[r, ls], (((1,), (1,)), ((), ())), preferred_element_type=F32)
            att = (sc * tt_ref[2 * p * T:(2 * p + 2) * T, :]).astype(BF16)
            oh = jnp.dot(att, v_ref[r, ls], preferred_element_type=F32)
            o = jnp.where(lo, oh[:T], oh[T:])
            rs = slice(p * R_DIM, (p + 1) * R_DIM)
            states = jnp.concatenate([state_block(sf_ref[c, rs, :]), state_block(sb_ref[c, rs, :])], axis=1)
            oi = jnp.dot(q, states, preferred_element_type=F32)
            o = (o + oi[:, :LANES] * rows_ref[_ROW_DQ_F:_ROW_DQ_F + T, ls]
                 + oi[:, LANES:] * rows_ref[_ROW_DQ_B:_ROW_DQ_B + T, ls])
            o2 = o * o
            ms = jnp.where(lo, jnp.sum(jnp.where(lo, o2, 0.0), axis=-1, keepdims=True),
                           jnp.sum(jnp.where(lo, 0.0, o2), axis=-1, keepdims=True)) * (1.0 / R_DIM)
            g = g_ref[r, ls]
            o_ref[r, ls] = (o * lax.rsqrt(ms + EPS) * (g * jax.nn.sigmoid(g))).astype(BF16)


def _retention_out(rq, rk, rv, rg, sf, sb, tab_tt, tab_rows, *, n_rows):
    T = R_CHUNK
    W = R_HEADS * R_DIM
    rows = pl.BlockSpec((RET_OUT_CH * T, W), lambda c: (c, 0))
    st = pl.BlockSpec((RET_OUT_CH, W // 2, LANES), lambda c: (c, 0, 0))
    return pl.pallas_call(
        _ret_out_kernel,
        out_shape=jax.ShapeDtypeStruct((n_rows, W), BF16),
        grid=(n_rows // (RET_OUT_CH * T),),
        in_specs=[rows, rows, rows, rows, st, st, _const_spec(tab_tt.shape), _const_spec(tab_rows.shape)],
        out_specs=rows,
        compiler_params=_cparams("arbitrary"),
        name="retention_out",
    )(rq, rk, rv, rg, sf, sb, tab_tt, tab_rows)


def _hy_in_kernel(x_ref, xp_ref, xn_ref, mod_ref, gpre_ref, w_ref, bin_ref, wsh_ref, bsh_ref,
                  x0_ref, u_ref, *, tm, n_lat, tiles_per_seq, tiles_per_ctx):
    D = x_ref.shape[1]
    H = SUBLANES
    t = pl.program_id(0)
    lat_tiles = n_lat // tm
    pos = jnp.where(t < lat_tiles, t % tiles_per_seq, (t - lat_tiles) % tiles_per_ctx)
    n_pos = jnp.where(t < lat_tiles, tiles_per_seq, tiles_per_ctx)
    xs = jnp.concatenate([xp_ref[...], x_ref[...], xn_ref[...]], axis=0)
    hb = _adaln_in(xs, mod_ref, gpre_ref, 1).astype(BF16)
    keep_prev = jnp.where(pos == 0, 0.0, 1.0)
    keep_next = jnp.where(pos == n_pos - 1, 0.0, 1.0)
    n_ext = tm + 2 * H

    sub = lax.broadcasted_iota(jnp.int32, (H, 1), 0)
    first_row = jnp.where(sub == 0, 1.0 - keep_prev, 0.0)
    last_row = jnp.where(sub == H - 1, 1.0 - keep_next, 0.0)

    def conv(c):
        cols = slice(c * D, (c + 1) * D)
        w0, w1, w2 = wsh_ref[0:1, cols], wsh_ref[1:2, cols], wsh_ref[2:3, cols]
        b_in = bin_ref[:, cols]
        bias = bsh_ref[:, cols] + b_in * (w0 + w1 + w2)
        z = jnp.dot(hb, w_ref[:, cols], preferred_element_type=F32)
        z = jnp.concatenate([z[0:H] * keep_prev, z[H:H + tm], z[H + tm:] * keep_next], axis=0)
        zm = pltpu.roll(z, 1, 0)[H:H + tm]
        zp = pltpu.roll(z, n_ext - 1, 0)[H:H + tm]
        out = zm * w0 + z[H:H + tm] * w1 + zp * w2 + bias
        top = out[0:H] - first_row * (b_in * w0)
        bot = out[tm - H:] - last_row * (b_in * w2)
        return jnp.concatenate([top, out[H:tm - H], bot], axis=0)

    x0_ref[...] = conv(0).astype(BF16)
    u_ref[...] = (conv(2) * conv(1)).astype(BF16)


def _hy_in(s, mod, gpre, w_in, b_in, w_sh, b_sh, *, layer, n_rows, geom, ctx_len):
    n_lat, seq_len, _ = geom
    D = s.shape[1]
    tm = HY_TM
    H = SUBLANES
    blocks_per_tile = tm // H
    last_block = s.shape[0] // H - 1
    kern = partial(_hy_in_kernel, tm=tm, n_lat=n_lat, tiles_per_seq=seq_len // tm,
                   tiles_per_ctx=max(ctx_len // tm, 1))
    out = jax.ShapeDtypeStruct((n_rows, D), BF16)
    return pl.pallas_call(
        kern,
        out_shape=[out, out],
        grid=(n_rows // tm,),
        in_specs=[
            pl.BlockSpec((tm, D), lambda t: (t, 0)),
            pl.BlockSpec((H, D), lambda t: (jnp.maximum(t * blocks_per_tile - 1, 0), 0)),
            pl.BlockSpec((H, D), lambda t: (jnp.minimum((t + 1) * blocks_per_tile, last_block), 0)),
            pl.BlockSpec((1, 1, N_MOD, D), _mod_spec(layer, tm, *geom)),
            pl.BlockSpec((1, 1, D), lambda t: (3 * layer + 1, 0, 0), pipeline_mode=pl.Buffered(1)),
            _const_spec((D, 3 * D)),
            _const_spec((1, 3 * D)),
            _const_spec((3, 3 * D)),
            _const_spec((1, 3 * D)),
        ],
        out_specs=[pl.BlockSpec((tm, D), lambda t: (t, 0))] * 2,
        compiler_params=_cparams("arbitrary"),
        name="hyena_in",
    )(s, s, s, mod, gpre, w_in, b_in, w_sh, b_sh)


def _hy_filter_kernel(z_ref, f0_ref, fb0_ref, f1_ref, fb1_ref, f2_ref, fb2_ref, f3_ref, fr_ref, dl_ref,
                      kfb_ref):
    D = dl_ref.shape[1]
    hp = lax.Precision.HIGHEST
    z = z_ref[...]
    fr = fr_ref[...]
    a = jnp.sin(fr * (jnp.dot(z, f0_ref[...], precision=hp, preferred_element_type=F32) + fb0_ref[...]))
    a = jnp.sin(fr * (jnp.dot(a, f1_ref[...], precision=hp, preferred_element_type=F32) + fb1_ref[...]))
    a = jnp.sin(fr * (jnp.dot(a, f2_ref[...], precision=hp, preferred_element_type=F32) + fb2_ref[...]))
    k = jnp.dot(a.astype(BF16), f3_ref[...].astype(BF16), preferred_element_type=F32)
    decay = jnp.exp(-z[:, 0:1] * dl_ref[...])
    k_f = k[:, :D] * decay
    k_b = k[:, D:] * decay
    row = lax.broadcasted_iota(jnp.int32, (z.shape[0], 1), 0) + pl.program_id(0) * z.shape[0]
    k_b = jnp.where(row == 0, 0.0, k_b)
    kfb_ref[0] = k_f.astype(BF16)
    kfb_ref[1] = k_b.astype(BF16)


def _hy_filter(L, f0, fb0, f1, fb1, f2, fb2, f3, freq, D):
    t = np.linspace(0.0, 1.0, L)[:, None]
    bands = (HY_EMB - 1) // 2
    w = 2.0 * math.pi * np.arange(L)[:, None] / L
    f = np.linspace(1e-4, bands - 1, bands)[None]
    z = jnp.asarray(np.concatenate([t, np.cos(f * w), -np.sin(f * w), np.zeros((L, HY_EMB_PAD - HY_EMB))], -1),
                    dtype=F32)
    f0p = jnp.concatenate([f0, jnp.zeros((HY_EMB_PAD - HY_EMB, f0.shape[1]), F32)], 0)
    deltas = jnp.asarray(np.abs(np.linspace(HY_MIN_DECAY, HY_MAX_DECAY, D))[None], dtype=F32)
    tl = min(512, L)
    O = f0.shape[1]
    return pl.pallas_call(
        _hy_filter_kernel,
        out_shape=jax.ShapeDtypeStruct((2, L, D), BF16),
        grid=(L // tl,),
        in_specs=[pl.BlockSpec((tl, HY_EMB_PAD), lambda i: (i, 0)),
                  _const_spec((HY_EMB_PAD, O)), _const_spec((1, O)),
                  _const_spec((O, O)), _const_spec((1, O)),
                  _const_spec((O, O)), _const_spec((1, O)),
                  _const_spec((O, 2 * D)), _const_spec((1, O)), _const_spec((1, D))],
        out_specs=pl.BlockSpec((2, tl, D), lambda i: (0, i, 0)),
        compiler_params=_cparams("arbitrary"),
        name="hyena_filter",
    )(z, f0p, fb0[None], f1, fb1[None], f2, fb2[None], f3, freq[None], deltas)


def _dft_matrices(L, tf):
    k2 = 2 * np.arange(L, dtype=np.int64)[:, None] + 1
    ang = ((k2 * np.arange(L, dtype=np.int64)[None, :]) % (4 * L)) * (2.0 * math.pi / (4 * L))
    cos, nsin = np.cos(ang), -np.sin(ang)
    fwd = np.concatenate([cos.reshape(L // tf, tf, L), nsin.reshape(L // tf, tf, L)], axis=1).reshape(2 * L, L)
    inv = fwd.T * (1.0 / L)
    return jnp.asarray(fwd, dtype=F32).astype(BF16), jnp.asarray(inv, dtype=F32).astype(BF16)


def _spectrum_kernel(w_ref, kfb_ref, o_ref):
    tf = w_ref.shape[0] // 2
    k_f = kfb_ref[0].astype(F32)
    k_b = kfb_ref[1].astype(F32)
    o_ref[0:tf, :] = jnp.dot(w_ref[0:tf, :], (k_f + k_b).astype(BF16), preferred_element_type=F32)
    o_ref[tf:, :] = jnp.dot(w_ref[tf:, :], (k_f - k_b).astype(BF16), preferred_element_type=F32)


def _filter_spectrum(fwd, kfb, tf):
    _, L, D = kfb.shape
    return pl.pallas_call(
        _spectrum_kernel,
        out_shape=jax.ShapeDtypeStruct((2 * L, D), F32),
        grid=(L // tf,),
        in_specs=[pl.BlockSpec((2 * tf, L), lambda j: (j, 0)), _const_spec((2, L, D))],
        out_specs=pl.BlockSpec((2 * tf, D), lambda j: (j, 0)),
        compiler_params=_cparams("arbitrary"),
        name="filter_spectrum",
    )(fwd, kfb)


def _dft_fwd_kernel(w_ref, u_ref, kh_ref, o_ref):
    tf = w_ref.shape[0] // 2
    uh = jnp.dot(w_ref[...], u_ref[...], preferred_element_type=F32)
    ur, ui = uh[:tf], uh[tf:]
    kr, ki = kh_ref[0:tf, :], kh_ref[tf:, :]
    o_ref[0:tf, :] = (ur * kr - ui * ki).astype(BF16)
    o_ref[tf:, :] = (ur * ki + ui * kr).astype(BF16)


def _dft_forward(fwd, u, khat, *, n_batch, L, row_off, tf):
    D = u.shape[1]
    seq0 = row_off // L
    return pl.pallas_call(
        _dft_fwd_kernel,
        out_shape=jax.ShapeDtypeStruct((n_batch * 2 * L, D), BF16),
        grid=(n_batch, L // tf),
        in_specs=[pl.BlockSpec((2 * tf, L), lambda b, j: (j, 0)),
                  pl.BlockSpec((L, D), lambda b, j: (seq0 + b, 0)),
                  pl.BlockSpec((2 * tf, D), lambda b, j: (j, 0))],
        out_specs=pl.BlockSpec((2 * tf, D), lambda b, j: (b * (L // tf) + j, 0)),
        compiler_params=_cparams("arbitrary", "arbitrary"),
        name="dft_forward",
    )(fwd, u, khat)


def _dft_inv_kernel(w_ref, y_ref, u_ref, x0_ref, bias_ref, o_ref, acc):
    kk = pl.program_id(2)

    @pl.when(kk == 0)
    def _():
        acc[...] = jnp.zeros_like(acc)

    acc[...] += jnp.dot(w_ref[...], y_ref[...], preferred_element_type=F32)

    @pl.when(kk == pl.num_programs(2) - 1)
    def _():
        y = acc[...] + u_ref[...].astype(F32) * bias_ref[...]
        o_ref[...] = (x0_ref[...].astype(F32) * y).astype(BF16)


def _dft_inverse(inv, yhat, u, x0, bias, *, n_batch, L, row_off):
    D = u.shape[1]
    tm = min(1024, L)
    tk = min(2048, 2 * L)
    m_tiles = L // tm
    k_tiles = 2 * L // tk
    blk0 = row_off // tm
    rows_in = pl.BlockSpec((tm, D), lambda b, i, kk: (blk0 + b * m_tiles + i, 0))
    return pl.pallas_call(
        _dft_inv_kernel,
        out_shape=jax.ShapeDtypeStruct((n_batch * L, D), BF16),
        grid=(n_batch, m_tiles, k_tiles),
        in_specs=[pl.BlockSpec((tm, tk), lambda b, i, kk: (i, kk)),
                  pl.BlockSpec((tk, D), lambda b, i, kk: (b * k_tiles + kk, 0)),
                  rows_in, rows_in, _const_spec((1, D))],
        out_specs=pl.BlockSpec((tm, D), lambda b, i, kk: (b * m_tiles + i, 0)),
        scratch_shapes=[pltpu.VMEM((tm, D), F32)],
        compiler_params=_cparams("arbitrary", "arbitrary", "arbitrary"),
        name="dft_inverse",
    )(inv, yhat, u, x0, bias)


FFT_N2 = 256
FFT_J = 16
FFT_JB = 64


def _fft_expand_kernel(t_ref, tt_ref, ma_ref, mi_ref):
    J = FFT_J
    R, H1 = t_ref.shape[1], t_ref.shape[2]
    W = H1 * J
    Rp = tt_ref.shape[2]
    hp = lax.Precision.HIGHEST
    col = lax.broadcasted_iota(jnp.int32, (1, W), 1)
    spread = (lax.broadcasted_iota(jnp.int32, (H1, 1), 0) == col // J).astype(F32)
    row_j = lax.broadcasted_iota(jnp.int32, (R, 1), 0) % J
    ma = jnp.dot(t_ref[0], spread, precision=hp, preferred_element_type=F32)
    ma_ref[0] = jnp.where(row_j == col % J, ma, 0.0).astype(BF16)
    rowi = lax.broadcasted_iota(jnp.int32, (W, 1), 0)
    spread_t = (rowi // J == lax.broadcasted_iota(jnp.int32, (1, H1), 1)).astype(F32)
    col_j = lax.broadcasted_iota(jnp.int32, (1, Rp), 1) % J
    mi = jnp.dot(spread_t, tt_ref[0], precision=hp, preferred_element_type=F32)
    mi_ref[0] = jnp.where(rowi % J == col_j, mi, 0.0).astype(BF16)


def _fft_matrices(L):
    N = 2 * L
    N1 = N // FFT_N2
    H1, K1, J, A = N1 // 2, N1 // 2 + 1, FFT_J, FFT_N2 // FFT_J
    R = K1 * 2 * J
    Rp = R + (-R) % LANES
    a = np.arange(A, dtype=np.int64)[:, None, None, None]
    k1 = np.arange(K1, dtype=np.int64)[None, :, None, None]
    j = np.arange(J, dtype=np.int64)[None, None, :, None]
    n1 = np.arange(H1, dtype=np.int64)[None, None, None, :]
    ang = ((k1 * (FFT_N2 * n1 + J * a + j)) % N) * (2.0 * math.pi / N)
    base = np.stack([np.cos(ang), -np.sin(ang)], axis=2)
    kk = np.arange(K1)
    w = np.where((kk == 0) | (kk == N1 // 2), 1.0, 2.0) / N
    t = jnp.asarray(base.reshape(A, R, H1), dtype=F32)
    tt = np.transpose((base * w[None, :, None, None, None]).reshape(A, R, H1), (0, 2, 1))
    tt = jnp.asarray(np.pad(tt, ((0, 0), (0, 0), (0, Rp - R))), dtype=F32)
    ma, mi = pl.pallas_call(
        _fft_expand_kernel,
        out_shape=[jax.ShapeDtypeStruct((A, R, H1 * J), BF16), jax.ShapeDtypeStruct((A, H1 * J, Rp), BF16)],
        grid=(A,),
        in_specs=[pl.BlockSpec((1, R, H1), lambda i: (i, 0, 0)), pl.BlockSpec((1, H1, Rp), lambda i: (i, 0, 0))],
        out_specs=[pl.BlockSpec((1, R, H1 * J), lambda i: (i, 0, 0)),
                   pl.BlockSpec((1, H1 * J, Rp), lambda i: (i, 0, 0))],
        compiler_params=_cparams("arbitrary"),
        name="fft_matrices",
    )(t, tt)
    n2 = np.arange(FFT_N2, dtype=np.int64)
    angf = ((n2[:, None] * n2[None, :]) % FFT_N2) * (2.0 * math.pi / FFT_N2)
    fr, fi = np.cos(angf), -np.sin(angf)
    cat = lambda top, bot: jnp.asarray(np.concatenate([top, bot], axis=0), dtype=F32).astype(BF16)
    return dict(ma=ma, mi=mi, k1=K1, h1=H1,
                mre=cat(fr, fi), mim=cat(-fi, fr), gre=cat(fr, -fi), gim=cat(fi, fr))


def _fft_a_kernel(u_ref, ma_ref, z_ref):
    H1, JB, D = u_ref.shape
    K1 = z_ref.shape[1]
    J = FFT_J
    a = pl.program_id(1)
    for q in range(JB // J):
        rows = slice(q * J, (q + 1) * J)
        data = u_ref[:, rows, :].reshape(H1 * J, D)
        res = jnp.dot(ma_ref[a * (JB // J) + q], data, preferred_element_type=F32)
        z_ref[0, :, :, rows, :] = res.astype(BF16).reshape(K1, 2, J, D)


def _fft_a(u3, mats, *, n_batch):
    D = u3.shape[2]
    K1, H1 = mats["k1"], mats["h1"]
    return pl.pallas_call(
        _fft_a_kernel,
        out_shape=jax.ShapeDtypeStruct((n_batch, K1, 2, FFT_N2, D), BF16),
        grid=(n_batch, FFT_N2 // FFT_JB),
        in_specs=[pl.BlockSpec((H1, FFT_JB, D), lambda b, a: (b, a, 0)), _const_spec(mats["ma"].shape)],
        out_specs=pl.BlockSpec((1, K1, 2, FFT_JB, D), lambda b, a: (b, 0, 0, a, 0)),
        compiler_params=_cparams("arbitrary", "arbitrary"),
        name="fft_stage_a",
    )(u3, mats["ma"])


def _dft256(mre_ref, mim_ref, z_ref, idx):
    n2 = FFT_N2
    x = (jnp.dot(mre_ref[...], z_ref[idx + (0,)], preferred_element_type=F32)
         + jnp.dot(mim_ref[...], z_ref[idx + (1,)], preferred_element_type=F32))
    return x[:n2], x[n2:]


def _fft_spec_kernel(zf_ref, zb_ref, mre_ref, mim_ref, o_ref):
    fr, fi = _dft256(mre_ref, mim_ref, zf_ref, (0, 0))
    br, bi = _dft256(mre_ref, mim_ref, zb_ref, (0, 0))
    o_ref[0, 0] = fr + br
    o_ref[0, 1] = fi - bi


def _fft_spectrum(zfilt, mats):
    _, K1, _, n2, D = zfilt.shape
    blk = (1, 1, 2, n2, D)
    return pl.pallas_call(
        _fft_spec_kernel,
        out_shape=jax.ShapeDtypeStruct((K1, 2, n2, D), F32),
        grid=(K1,),
        in_specs=[pl.BlockSpec(blk, lambda k: (0, k, 0, 0, 0)), pl.BlockSpec(blk, lambda k: (1, k, 0, 0, 0)),
                  _const_spec(mats["mre"].shape), _const_spec(mats["mim"].shape)],
        out_specs=pl.BlockSpec((1, 2, n2, D), lambda k: (k, 0, 0, 0)),
        compiler_params=_cparams("arbitrary"),
        name="fft_filter_spectrum",
    )(zfilt, zfilt, mats["mre"], mats["mim"])


def _fft_mid_kernel(z_ref, kh_ref, mre_ref, mim_ref, gre_ref, gim_ref, o_ref):
    n2 = FFT_N2
    kr, ki = kh_ref[0, 0], kh_ref[0, 1]
    for b in range(z_ref.shape[0]):
        xr, xi = _dft256(mre_ref, mim_ref, z_ref, (b, 0))
        yr = (xr * kr - xi * ki).astype(BF16)
        yi = (xr * ki + xi * kr).astype(BF16)
        zp = (jnp.dot(gre_ref[...], yr, preferred_element_type=F32)
              + jnp.dot(gim_ref[...], yi, preferred_element_type=F32))
        o_ref[b, 0, 0] = zp[:n2].astype(BF16)
        o_ref[b, 0, 1] = zp[n2:].astype(BF16)


def _fft_mid(z, khat, mats):
    n_batch, K1, _, n2, D = z.shape
    nb = math.gcd(4, n_batch)
    blk = pl.BlockSpec((nb, 1, 2, n2, D), lambda k, b: (b, k, 0, 0, 0))
    mat = [_const_spec(mats[m].shape) for m in ("mre", "mim", "gre", "gim")]
    return pl.pallas_call(
        _fft_mid_kernel,
        out_shape=jax.ShapeDtypeStruct(z.shape, BF16),
        grid=(K1, n_batch // nb),
        in_specs=[blk, pl.BlockSpec((1, 2, n2, D), lambda k, b: (k, 0, 0, 0))] + mat,
        out_specs=blk,
        compiler_params=_cparams("arbitrary", "arbitrary"),
        name="fft_stage_b",
    )(z, khat, mats["mre"], mats["mim"], mats["gre"], mats["gim"])


def _fft_a_inv_kernel(zp_ref, mi_ref, u_ref, x0_ref, bias_ref, o_ref):
    H1, JB, D = u_ref.shape
    K1 = zp_ref.shape[1]
    J = FFT_J
    a = pl.program_id(1)
    k_pad = mi_ref.shape[2] - K1 * 2 * J
    for q in range(JB // J):
        rows = slice(q * J, (q + 1) * J)
        zz = zp_ref[0, :, :, rows, :].reshape(K1 * 2 * J, D)
        if k_pad:
            zz = jnp.concatenate([zz, jnp.zeros((k_pad, D), BF16)], axis=0)
        y = jnp.dot(mi_ref[a * (JB // J) + q], zz, preferred_element_type=F32).reshape(H1, J, D)
        y = y + u_ref[:, rows, :].astype(F32) * bias_ref[...]
        o_ref[:, rows, :] = (x0_ref[:, rows, :].astype(F32) * y).astype(BF16)


def _fft_a_inv(zp, mats, u3, x03, bias, *, n_blocks_out):
    n_batch, K1, _, n2, D = zp.shape
    H1 = mats["h1"]
    rows = pl.BlockSpec((H1, FFT_JB, D), lambda b, a: (b, a, 0))
    return pl.pallas_call(
        _fft_a_inv_kernel,
        out_shape=jax.ShapeDtypeStruct((n_blocks_out, n2, D), BF16),
        grid=(n_batch, n2 // FFT_JB),
        in_specs=[pl.BlockSpec((1, K1, 2, FFT_JB, D), lambda b, a: (b, 0, 0, a, 0)),
                  _const_spec(mats["mi"].shape), rows, rows, _const_spec((1, D))],
        out_specs=rows,
        compiler_params=_cparams("arbitrary", "arbitrary"),
        name="fft_stage_a_inv",
    )(zp, mats["mi"], u3, x03, bias)


def kernel(x, c, ctx, c_ctx, w_mod, b_mod, norm_pre, norm_post, ffn_gate, ffn_up, ffn_down,
           mix_w_in, attn_sink, ret_decay, mix_w_out, hy_w_in, hy_b_in, hy_short_w, hy_short_b,
           hy_f0, hy_fb0, hy_f1, hy_fb1, hy_f2, hy_fb2, hy_f3, hy_freq, hy_bias, hy_w_out):
    Bn, L, D = x.shape
    C = ctx.shape[1]
    n_lat = Bn * L
    n_all = n_lat + Bn * C
    geom = (n_lat, L, Bn)
    assert L % min(PROJ_TM, L) == 0 and L % HY_TM == 0 and n_lat % C == 0 and (Bn * C) % min(PROJ_TM, L) == 0
    assert L % min(FFN_TM, L) == 0 and (Bn * C) % min(FFN_TM, L) == 0
    assert C % R_CHUNK == 0 and L % R_CHUNK == 0 and C % HY_TM == 0
    assert n_lat % (RET_OUT_CH * R_CHUNK) == 0 and n_all % (RET_OUT_CH * R_CHUNK) == 0
    assert L % FFT_N2 == 0 and n_all % FFT_N2 == 0
    last_reader = DEPTH - 1 if (DEPTH - 1) % 2 == 0 else DEPTH - 2

    c_all = jnp.concatenate([c, c_ctx[None], jnp.zeros((SUBLANES - (Bn + 1) % SUBLANES, D), F32)], axis=0)
    mod = _modulation(c_all, w_mod, b_mod).reshape(DEPTH, c_all.shape[0], N_MOD, D)

    rope_a = _rope_tables(_axial_angles(L), min(PROJ_TM, L))
    rope_r = _rope_tables(_line_angles(L), min(PROJ_TM, L))
    fft = _fft_matrices(L)

    npre = norm_pre.reshape(DEPTH * 3, 1, D)
    npost = norm_post.reshape(DEPTH * 3, 1, D)
    wg, wu, wd = ffn_gate.astype(BF16), ffn_up.astype(BF16), ffn_down.astype(BF16)
    mix_wo, hy_wo = mix_w_out.astype(BF16), hy_w_out.astype(BF16)

    def ffn(s, l, i, j, n_rows, mix=None):
        return _ffn(s, mod, npre, npost, wg, wu, wd, layer=l, i=i, j=j, n_rows=n_rows, geom=geom, mix=mix)

    s = (x.reshape(n_lat, D), ctx.reshape(Bn * C, D))
    for l in range(DEPTH):
        ctx_live = l <= last_reader
        ctx_full = l < last_reader
        n_in = n_all if ctx_live else n_lat
        n_out = n_all if ctx_full else n_lat
        s = ffn(s, l, 0, 0, n_in)
        if l % 2 == 0:
            e = l // 2
            aq, rq, rg, akd, avd, rk, rv = _even_in(s, mod, npre, _pack_even_w_in(mix_w_in[e]), rope_a, rope_r,
                                                    layer=l, n_rows=n_in, geom=geom)
            sink_lanes = jnp.broadcast_to(attn_sink[e][:, None], (A_Q_HEADS, LANES))
            dec_lanes = jnp.repeat(ret_decay[e], R_DIM, axis=1)
            a = _attention(aq, akd, avd, sink_lanes, n_batch=Bn, seq_len=L, ctx_len=C,
                           with_ctx_queries=ctx_full)
            tab_tt, tab_rows = _retention_tables(dec_lanes)
            sf, sb = _retention_states(rk, rv, tab_rows, n_batch=Bn, seq_len=L, ctx_len=C)
            r = _retention_out(rq, rk, rv, rg, sf, sb, tab_tt, tab_rows, n_rows=n_out)
            mix = ([a, r], mix_wo, e)
        else:
            o = l // 2
            x0, u = _hy_in(s, mod, npre, hy_w_in[o].astype(BF16), hy_b_in[o][None], hy_short_w[o],
                           hy_short_b[o][None], layer=l, n_rows=n_in, geom=geom, ctx_len=C)
            filt = (hy_f0[o], hy_fb0[o], hy_f1[o], hy_fb1[o], hy_f2[o], hy_fb2[o], hy_f3[o], hy_freq[o])
            bias = hy_bias[o][None]
            blocks = lambda arr: arr.reshape(arr.shape[0] // FFT_N2, FFT_N2, D)
            kfb = _hy_filter(L, *filt, D)
            khat = _fft_spectrum(_fft_a(blocks(kfb.reshape(2 * L, D)), fft, n_batch=2), fft)
            zp = _fft_mid(_fft_a(blocks(u), fft, n_batch=Bn), khat, fft)
            yg = _fft_a_inv(zp, fft, blocks(u), blocks(x0), bias, n_blocks_out=n_lat // FFT_N2).reshape(n_lat, D)
            if ctx_full:
                tf = min(DFT_TF, C)
                fwd, inv = _dft_matrices(C, tf)
                khat_c = _filter_spectrum(fwd, _hy_filter(C, *filt, D), tf)
                yhat = _dft_forward(fwd, u, khat_c, n_batch=Bn, L=C, row_off=n_lat, tf=tf)
                yg = (yg, _dft_inverse(inv, yhat, u, x0, bias, n_batch=Bn, L=C, row_off=n_lat))
            mix = ([yg], hy_wo, o)
        s = ffn(s, l, 2, 1, n_out, mix)
    return s[:n_lat].reshape(Bn, L, D)
```
